```python
import jax, jax.numpy as jnp
from jax import lax
import numpy as np

D_MODEL = 1024
BATCH = 8
SEQ = 4096
DEPTH = 2
DEC_BATCH = 16
DEC_SEQ = 16
PAST_LEN = 1024

CHUNK = 64
HEAD_DIM = 64
H_A = 8
H_B = 8
H_C = 8
H_D = 8
Q_RANK = 256
KV_RANK = 128
NOPE_DIM = 64
ROPE_DIM = 32
V_DIM_A = 64
ROPE_BASE = 10000.0
BAND_CHUNKS = 8
BAND_ROWS = BAND_CHUNKS * CHUNK
REL_CLIP = 128
D_FF = 2816
CONV_W = 3
Q_BLOCK = 128
EPS = 1e-6
NEG_INF = -1e30
N_EVEN = (DEPTH + 1) // 2
N_ODD = DEPTH // 2
AB_SECTIONS = (Q_RANK, KV_RANK, ROPE_DIM, H_B * HEAD_DIM, H_B * HEAD_DIM, H_B * HEAD_DIM, H_B)
AB_WIDTH = sum(AB_SECTIONS)
AB_SPLITS = tuple(int(v) for v in np.cumsum(AB_SECTIONS)[:-1])
AB_OUT = H_A * V_DIM_A + H_B * HEAD_DIM
CD_SECTIONS = (H_C * HEAD_DIM,) * 3 + (H_D * HEAD_DIM,) * 3
CD_WIDTH = sum(CD_SECTIONS)
CD_SPLITS = tuple(int(v) for v in np.cumsum(CD_SECTIONS)[:-1])
CD_OUT = (H_C + H_D) * HEAD_DIM
STATE_KEYS = ('a_ckv', 'a_krope', 'b_k', 'b_v', 'b_logf', 'c_k', 'c_v', 'd_k', 'd_v', 'ffn_conv')

kernel_name = 'hybrid_streaming_encoder_step'


def rmsnorm(x, g):
    xf = x.astype(jnp.float32)
    y = xf * lax.rsqrt(jnp.mean(xf * xf, axis=-1, keepdims=True) + EPS)
    return (y * g.astype(jnp.float32)).astype(x.dtype)


def modulate(h, shift, scale):
    return h * (1.0 + scale[:, None, :]) + shift[:, None, :]


def rope(x, pos):
    half = x.shape[-1] // 2
    inv = ROPE_BASE ** (-jnp.arange(half, dtype=jnp.float32) / half)
    ang = pos.astype(jnp.float32)[:, None] * inv[None, :]
    cos = jnp.cos(ang)[None, :, None, :].astype(x.dtype)
    sin = jnp.sin(ang)[None, :, None, :].astype(x.dtype)
    x1, x2 = x[..., :half], x[..., half:]
    return jnp.concatenate([x1 * cos - x2 * sin, x1 * sin + x2 * cos], axis=-1)


def softmax_av(s, mask, v):
    p = jax.nn.softmax(jnp.where(mask, s, NEG_INF), axis=-1).astype(v.dtype)
    o = jnp.einsum('bhqk,bkhd->bqhd', p, v)
    return o.reshape(o.shape[0], o.shape[1], -1)


def mla_attend(q_args, k_args, qpos, kpos):
    q_nope, q_rope = q_args
    k_nope, k_rope, v = k_args
    s = (jnp.einsum('bqhd,bkhd->bhqk', q_nope, k_nope)
         + jnp.einsum('bqhr,bkr->bhqk', q_rope, k_rope)).astype(jnp.float32) * (NOPE_DIM + ROPE_DIM) ** -0.5
    mask = (kpos[None, :] // CHUNK) <= (qpos[:, None] // CHUNK)
    return softmax_av(s, mask, v)


def fox_attend(q_args, k_args, qpos, kpos):
    q, cum_q = q_args
    k, v, cum_k = k_args
    s = jnp.einsum('bqhd,bkhd->bhqk', q, k).astype(jnp.float32) * HEAD_DIM ** -0.5
    s = s + jnp.swapaxes(cum_q, 1, 2)[..., :, None] - jnp.swapaxes(cum_k, 1, 2)[..., None, :]
    mask = kpos[None, :] <= qpos[:, None]
    return softmax_av(s, mask, v)


def band_attend(q_args, k_args, qpos, kpos, rel_table):
    (q,) = q_args
    k, v = k_args
    s = jnp.einsum('bqhd,bkhd->bhqk', q, k).astype(jnp.float32) * HEAD_DIM ** -0.5
    rel = jnp.clip(qpos[:, None] - kpos[None, :], -REL_CLIP, REL_CLIP) + REL_CLIP
    s = s + jnp.transpose(rel_table[rel], (2, 0, 1))[None].astype(jnp.float32)
    qc, kc = qpos // CHUNK, kpos // CHUNK
    mask = (kc[None, :] <= qc[:, None]) & (kc[None, :] >= qc[:, None] - BAND_CHUNKS) & (kpos[None, :] >= 0)
    return softmax_av(s, mask, v)


def sb_attend(q_args, k_args, qpos, kpos):
    (q,) = q_args
    k, v = k_args
    z = jnp.einsum('bqhd,bkhd->bhqk', q, k).astype(jnp.float32) * HEAD_DIM ** -0.5
    mask = kpos[None, :] < qpos[:, None]
    log_keep = jnp.where(mask, jax.nn.log_sigmoid(-z), 0.0)
    after = lax.cumsum(log_keep, axis=3, reverse=True) - log_keep
    w = jnp.where(mask, jnp.exp(jax.nn.log_sigmoid(z) + after), 0.0).astype(v.dtype)
    o = jnp.einsum('bhqk,bkhd->bqhd', w, v)
    return o.reshape(o.shape[0], o.shape[1], -1)


def attend(fn, q_args, k_args, qpos, kpos):
    T = qpos.shape[0]
    if T <= Q_BLOCK:
        return fn(q_args, k_args, qpos, kpos)
    nb = T // Q_BLOCK

    def split(a):
        return jnp.moveaxis(a.reshape((a.shape[0], nb, Q_BLOCK) + a.shape[2:]), 1, 0)

    qs = tuple(split(a) for a in q_args)
    out = lax.map(lambda xs: fn(xs[0], k_args, xs[1], kpos), (qs, qpos.reshape(nb, Q_BLOCK)))
    return jnp.moveaxis(out, 0, 1).reshape(out.shape[1], T, out.shape[-1])


def band_sweep(q, k, v, rel_table):
    B, T = q.shape[0], q.shape[1]
    nc = T // CHUNK
    pad = jnp.zeros((B, BAND_ROWS) + k.shape[2:], k.dtype)
    kp = jnp.concatenate([pad, k], axis=1)
    vp = jnp.concatenate([pad.astype(v.dtype), v], axis=1)
    qs = jnp.moveaxis(q.reshape((B, nc, CHUNK) + q.shape[2:]), 1, 0)

    def one(xs):
        qc, ci = xs
        start = ci * CHUNK
        kb = lax.dynamic_slice_in_dim(kp, start, BAND_ROWS + CHUNK, axis=1)
        vb = lax.dynamic_slice_in_dim(vp, start, BAND_ROWS + CHUNK, axis=1)
        qpos = start + jnp.arange(CHUNK)
        kpos = start - BAND_ROWS + jnp.arange(BAND_ROWS + CHUNK)
        return band_attend((qc,), (kb, vb), qpos, kpos, rel_table)

    out = lax.map(one, (qs, jnp.arange(nc)))
    return jnp.moveaxis(out, 0, 1).reshape(B, T, out.shape[-1])


def mixer_ab(h, qpos, past, W, i):
    B, T, _ = h.shape
    c_q, c_kv, k_r, q_b, k_b, v_b, f_b = jnp.split(h @ W['w_in_ab'][i], AB_SPLITS, axis=-1)
    q_a = (rmsnorm(c_q, W['q_a_g'][i]) @ W['w_uq'][i]).reshape(B, T, H_A, NOPE_DIM + ROPE_DIM)
    q_nope, q_rope = q_a[..., :NOPE_DIM], rope(q_a[..., NOPE_DIM:], qpos)
    c_kv = rmsnorm(c_kv, W['kv_a_g'][i])
    k_r = rope(k_r[:, :, None, :], qpos)[:, :, 0, :]
    q_b, k_b, v_b = (a.reshape(B, T, H_B, HEAD_DIM) for a in (q_b, k_b, v_b))
    logf = jax.nn.log_sigmoid(f_b.astype(jnp.float32) + W['b_f'][i].astype(jnp.float32)).astype(h.dtype)
    if past is None:
        ckv_all, kr_all, kb_all, vb_all, logf_all, kpos = c_kv, k_r, k_b, v_b, logf, qpos
    else:
        ckv_all = jnp.concatenate([past['a_ckv'][i], c_kv], axis=1)
        kr_all = jnp.concatenate([past['a_krope'][i], k_r], axis=1)
        kb_all = jnp.concatenate([past['b_k'][i], k_b], axis=1)
        vb_all = jnp.concatenate([past['b_v'][i], v_b], axis=1)
        logf_all = jnp.concatenate([past['b_logf'][i], logf], axis=1)
        kpos = jnp.arange(PAST_LEN + T)
    kv = (ckv_all @ W['w_ukv'][i]).reshape(B, -1, H_A, NOPE_DIM + V_DIM_A)
    cum = jnp.cumsum(logf_all.astype(jnp.float32), axis=1)
    o_a = attend(mla_attend, (q_nope, q_rope), (kv[..., :NOPE_DIM], kr_all, kv[..., NOPE_DIM:]), qpos, kpos)
    o_b = attend(fox_attend, (q_b, cum[:, -T:]), (kb_all, vb_all, cum), qpos, kpos)
    o = jnp.concatenate([o_a, o_b], axis=-1) @ W['w_out_ab'][i]
    return o, (c_kv, k_r, k_b, v_b, logf)


def mixer_cd(h, qpos, past, W, i):
    B, T, _ = h.shape
    parts = jnp.split(h @ W['w_in_cd'][i], CD_SPLITS, axis=-1)
    q_c, k_c, v_c = (a.reshape(B, T, H_C, HEAD_DIM) for a in parts[:3])
    q_d, k_d, v_d = (a.reshape(B, T, H_D, HEAD_DIM) for a in parts[3:])
    rel = W['rel_bias_c'][i]
    if past is None:
        o_c = band_sweep(q_c, k_c, v_c, rel)
        keep = min(BAND_ROWS, T)
        c_rows = (k_c[:, T - keep:], v_c[:, T - keep:])
        kd_all, vd_all, kpos = k_d, v_d, qpos
    else:
        n_c = past['c_k'].shape[2]
        kc_all = jnp.concatenate([past['c_k'][i], k_c], axis=1)
        vc_all = jnp.concatenate([past['c_v'][i], v_c], axis=1)
        kpos_c = PAST_LEN - n_c + jnp.arange(n_c + T)
        o_c = band_attend((q_c,), (kc_all, vc_all), qpos, kpos_c, rel)
        c_rows = (k_c, v_c)
        kd_all = jnp.concatenate([past['d_k'][i], k_d], axis=1)
        vd_all = jnp.concatenate([past['d_v'][i], v_d], axis=1)
        kpos = jnp.arange(PAST_LEN + T)
    o_d = attend(sb_attend, (q_d,), (kd_all, vd_all), qpos, kpos)
    o = jnp.concatenate([o_c, o_d], axis=-1) @ W['w_out_cd'][i]
    return o, (c_rows[0], c_rows[1], k_d, v_d)


def conv_ffn(h, prev, w_gate, w_up, conv_w, conv_b, w_down):
    T = h.shape[1]
    g = h @ w_gate
    u = h @ w_up
    g_ext = jnp.concatenate([prev.astype(g.dtype), g], axis=1)
    gc = conv_b + sum(conv_w[j] * g_ext[:, j:j + T] for j in range(CONV_W))
    out = (jax.nn.silu(gc) * u) @ w_down
    return out, g_ext[:, T:]


def trunk(x, c, past, W):
    B, T, _ = x.shape
    qpos = (0 if past is None else PAST_LEN) + jnp.arange(T)
    cond = jax.nn.silu(c)
    new = {n: [] for n in STATE_KEYS}
    for l in range(DEPTH):
        sh_m, sc_m, g_m, sh_f, sc_f, g_f = jnp.split(cond @ W['ada_w'][l] + W['ada_b'][l], 6, axis=-1)
        h = modulate(rmsnorm(x, W['mix_pre_g'][l]), sh_m, sc_m)
        if l % 2 == 0:
            o, rows = mixer_ab(h, qpos, past, W, l // 2)
            names = STATE_KEYS[:5]
        else:
            o, rows = mixer_cd(h, qpos, past, W, l // 2)
            names = STATE_KEYS[5:9]
        for n, r in zip(names, rows):
            new[n].append(r)
        x = x + g_m[:, None, :] * rmsnorm(o, W['mix_post_g'][l])
        h = modulate(rmsnorm(x, W['ffn_pre_g'][l]), sh_f, sc_f)
        prev = jnp.zeros((B, CONV_W - 1, D_FF), x.dtype) if past is None else past['ffn_conv'][l]
        o, conv_rows = conv_ffn(h, prev, W['ffn_w_gate'][l], W['ffn_w_up'][l], W['ffn_conv_w'][l],
                                W['ffn_conv_b'][l], W['ffn_w_down'][l])
        new['ffn_conv'].append(conv_rows)
        x = x + g_f[:, None, :] * rmsnorm(o, W['ffn_post_g'][l])
    return x, {n: jnp.stack(v) for n, v in new.items()}


def setup_inputs(seed: int = 0) -> dict:
    key = jax.random.key(seed)
    ks = iter(jax.random.split(key, 64))

    def nrm(shape, scale=1.0):
        return scale * jax.random.normal(next(ks), shape, jnp.float32)

    def gain(shape):
        return 1.0 + nrm(shape, 0.05)

    c_rows = min(BAND_ROWS, PAST_LEN)
    hd = HEAD_DIM
    return {
        'x_prompt': nrm((BATCH, SEQ, D_MODEL)),
        'x_sample': nrm((DEC_BATCH, DEC_SEQ, D_MODEL)),
        'c_prompt': nrm((BATCH, D_MODEL)),
        'c_sample': nrm((DEC_BATCH, D_MODEL)),
        'cache_a_ckv': nrm((N_EVEN, DEC_BATCH, PAST_LEN, KV_RANK)),
        'cache_a_krope': nrm((N_EVEN, DEC_BATCH, PAST_LEN, ROPE_DIM)),
        'cache_b_k': nrm((N_EVEN, DEC_BATCH, PAST_LEN, H_B, hd)),
        'cache_b_v': nrm((N_EVEN, DEC_BATCH, PAST_LEN, H_B, hd)),
        'cache_b_logf': jax.nn.log_sigmoid(3.0 + nrm((N_EVEN, DEC_BATCH, PAST_LEN, H_B))),
        'cache_c_k': nrm((N_ODD, DEC_BATCH, c_rows, H_C, hd)),
        'cache_c_v': nrm((N_ODD, DEC_BATCH, c_rows, H_C, hd)),
        'cache_d_k': nrm((N_ODD, DEC_BATCH, PAST_LEN, H_D, hd)),
        'cache_d_v': nrm((N_ODD, DEC_BATCH, PAST_LEN, H_D, hd)),
        'state_ffn_conv': nrm((DEPTH, DEC_BATCH, CONV_W - 1, D_FF), 0.5),
        'ada_w': nrm((DEPTH, D_MODEL, 6 * D_MODEL), 0.5 * D_MODEL ** -0.5),
        'ada_b': nrm((DEPTH, 6 * D_MODEL), 0.02),
        'mix_pre_g': gain((DEPTH, D_MODEL)),
        'mix_post_g': gain((DEPTH, D_MODEL)),
        'ffn_pre_g': gain((DEPTH, D_MODEL)),
        'ffn_post_g': gain((DEPTH, D_MODEL)),
        'w_in_ab': nrm((N_EVEN, D_MODEL, AB_WIDTH), D_MODEL ** -0.5),
        'b_f': jnp.linspace(1.0, 4.0, H_B, dtype=jnp.float32)[None, :] + nrm((N_EVEN, H_B), 0.1),
        'q_a_g': gain((N_EVEN, Q_RANK)),
        'kv_a_g': gain((N_EVEN, KV_RANK)),
        'w_uq': nrm((N_EVEN, Q_RANK, H_A * (NOPE_DIM + ROPE_DIM)), Q_RANK ** -0.5),
        'w_ukv': nrm((N_EVEN, KV_RANK, H_A * (NOPE_DIM + V_DIM_A)), KV_RANK ** -0.5),
        'w_out_ab': nrm((N_EVEN, AB_OUT, D_MODEL), AB_OUT ** -0.5),
        'w_in_cd': nrm((N_ODD, D_MODEL, CD_WIDTH), D_MODEL ** -0.5),
        'rel_bias_c': nrm((N_ODD, 2 * REL_CLIP + 1, H_C), 0.5),
        'w_out_cd': nrm((N_ODD, CD_OUT, D_MODEL), CD_OUT ** -0.5),
        'ffn_w_gate': nrm((DEPTH, D_MODEL, D_FF), D_MODEL ** -0.5),
        'ffn_w_up': nrm((DEPTH, D_MODEL, D_FF), D_MODEL ** -0.5),
        'ffn_conv_w': nrm((DEPTH, CONV_W, D_FF), CONV_W ** -0.5),
        'ffn_conv_b': nrm((DEPTH, D_FF), 0.02),
        'ffn_w_down': nrm((DEPTH, D_FF, D_MODEL), D_FF ** -0.5),
    }


def reference(x_prompt, x_sample, c_prompt, c_sample, cache_a_ckv, cache_a_krope, cache_b_k, cache_b_v,
              cache_b_logf, cache_c_k, cache_c_v, cache_d_k, cache_d_v, state_ffn_conv, ada_w, ada_b,
              mix_pre_g, mix_post_g, ffn_pre_g, ffn_post_g, w_in_ab, b_f, q_a_g, kv_a_g, w_uq, w_ukv,
              w_out_ab, w_in_cd, rel_bias_c, w_out_cd, ffn_w_gate, ffn_w_up, ffn_conv_w, ffn_conv_b,
              ffn_w_down):
    W = {'ada_w': ada_w, 'ada_b': ada_b, 'mix_pre_g': mix_pre_g, 'mix_post_g': mix_post_g,
         'ffn_pre_g': ffn_pre_g, 'ffn_post_g': ffn_post_g, 'w_in_ab': w_in_ab, 'b_f': b_f,
         'q_a_g': q_a_g, 'kv_a_g': kv_a_g, 'w_uq': w_uq, 'w_ukv': w_ukv, 'w_out_ab': w_out_ab,
         'w_in_cd': w_in_cd, 'rel_bias_c': rel_bias_c, 'w_out_cd': w_out_cd, 'ffn_w_gate': ffn_w_gate,
         'ffn_w_up': ffn_w_up, 'ffn_conv_w': ffn_conv_w, 'ffn_conv_b': ffn_conv_b, 'ffn_w_down': ffn_w_down}
    past = {'a_ckv': cache_a_ckv, 'a_krope': cache_a_krope, 'b_k': cache_b_k, 'b_v': cache_b_v,
            'b_logf': cache_b_logf, 'c_k': cache_c_k, 'c_v': cache_c_v, 'd_k': cache_d_k, 'd_v': cache_d_v,
            'ffn_conv': state_ffn_conv}
    y_prompt, sp = trunk(x_prompt, c_prompt, None, W)
    y_sample, ss = trunk(x_sample, c_sample, past, W)
    return (y_prompt, y_sample,
            sp['a_ckv'], ss['a_ckv'], sp['a_krope'], ss['a_krope'],
            sp['b_k'], ss['b_k'], sp['b_v'], ss['b_v'], sp['b_logf'], ss['b_logf'],
            sp['c_k'], ss['c_k'], sp['c_v'], ss['c_v'],
            sp['d_k'], ss['d_k'], sp['d_v'], ss['d_v'],
            sp['ffn_conv'], ss['ffn_conv'])
```

```python
import functools

import jax
import jax.numpy as jnp
import numpy as np
from jax import lax
from jax.experimental import pallas as pl
from jax.experimental.pallas import tpu as pltpu

D_MODEL = 1024
DEPTH = 2
PAST_LEN = 1024
CHUNK = 64
HEAD_DIM = 64
N_HEADS = 8
N_PAIRS = N_HEADS // 2
Q_RANK = 256
KV_RANK = 128
NOPE_DIM = 64
ROPE_DIM = 32
ROPE_BASE = 10000.0
BAND_CHUNKS = 8
REL_CLIP = 128
D_FF = 2816
CONV_W = 3
EPS = 1e-6
NEG_INF = -1e30

LANES = 128
FF_CHUNK = 256
N_FF_CHUNKS = D_FF // FF_CHUNK
ATT_BLOCK = 256
BAND_WIN = 640
VMEM_LIMIT_BYTES = 56 * 1024 * 1024

F32 = jnp.float32
BF16 = jnp.bfloat16

AB_CQ, AB_CKV, AB_KR, AB_KRROT, AB_QB, AB_KB, AB_VB, AB_FB, AB_COLS = 0, 256, 384, 512, 640, 1152, 1664, 2176, 2304


def _params(n_axes):
    return pltpu.CompilerParams(dimension_semantics=("arbitrary",) * n_axes, vmem_limit_bytes=VMEM_LIMIT_BYTES)


def _rms(x, g):
    return x * lax.rsqrt(jnp.mean(x * x, axis=-1, keepdims=True) + EPS) * g


def _log_sigmoid(x):
    return jnp.minimum(x, 0.0) - jnp.log1p(jnp.exp(-jnp.abs(x)))


def _dot(a, b):
    return jnp.dot(a, b, preferred_element_type=F32)


def _dot_nt(a, b):
    return lax.dot_general(a, b, (((1,), (1,)), ((), ())), preferred_element_type=F32)


def _split_bf16(x, n):
    parts = []
    for _ in range(n):
        p = x.astype(BF16)
        parts.append(p)
        x = x - p.astype(F32)
    return parts


def _tile_lanes(x, n):
    return x if n == 1 else jnp.concatenate([x] * n, axis=1)


def _ada_kernel(c_ref, w_ref, b_ref, o_ref):
    c = c_ref[...]
    cond = c * jax.nn.sigmoid(c)
    o_ref[0] = _dot(cond.astype(BF16), w_ref[0].astype(BF16)) + b_ref[0]


def _ada(c_all, ada_w, ada_b):
    rows = c_all.shape[0]
    tn = 1536
    return pl.pallas_call(
        _ada_kernel,
        grid=(DEPTH, 6 * D_MODEL // tn),
        in_specs=[
            pl.BlockSpec((rows, D_MODEL), lambda l, j: (0, 0)),
            pl.BlockSpec((1, D_MODEL, tn), lambda l, j: (l, 0, j)),
            pl.BlockSpec((1, 1, tn), lambda l, j: (l, 0, j)),
        ],
        out_specs=pl.BlockSpec((1, rows, tn), lambda l, j: (l, 0, j)),
        out_shape=jax.ShapeDtypeStruct((DEPTH, rows, 6 * D_MODEL), F32),
        compiler_params=_params(2),
        name="ada",
    )(c_all, ada_w, ada_b.reshape(DEPTH, 1, 6 * D_MODEL))


def _in_ab_kernel(x_ref, sh_ref, sc_ref, gpre_ref, w_ref, wft_ref, bf_ref, bft_ref, gq_ref, gkv_ref, wq_ref,
                  tqc_ref, tqs_ref, tkc_ref, tks_ref,
                  qcomb_ref, ckv_ref, krope_ref, ckr_ref, qb_ref, kb16_ref, vb16_ref, kb32_ref, vb32_ref,
                  logf_ref, logf128_ref, logft_ref):
    x = x_ref[0]
    h = _rms(x, gpre_ref[...]) * (1.0 + sc_ref[0]) + sh_ref[0]
    hb = h.astype(BF16)
    y = _dot(hb, w_ref[...])
    cq = _rms(y[:, AB_CQ:AB_CQ + Q_RANK], gq_ref[...]).astype(BF16)
    qa = _dot(cq, wq_ref[...])
    tqc = tqc_ref[...]
    tqs = tqs_ref[...]
    half = N_HEADS * LANES
    for hh in range(N_HEADS):
        lo = hh * LANES
        qcomb_ref[0, :, lo:lo + LANES] = (qa[:, lo:lo + LANES] * tqc + qa[:, half + lo:half + lo + LANES] * tqs).astype(BF16)
    ckv = _rms(y[:, AB_CKV:AB_CKV + KV_RANK], gkv_ref[...])
    ckv_ref[0] = ckv
    kr = y[:, AB_KR:AB_KR + LANES] * tkc_ref[...] + y[:, AB_KRROT:AB_KRROT + LANES] * tks_ref[...]
    krope_ref[0] = kr[:, :ROPE_DIM]
    ckr_ref[0, :, 0:LANES] = ckv.astype(BF16)
    ckr_ref[0, :, LANES:2 * LANES] = kr.astype(BF16)
    qb_ref[0] = (y[:, AB_QB:AB_QB + 512] * 0.125).astype(BF16)
    kb = y[:, AB_KB:AB_KB + 512]
    vb = y[:, AB_VB:AB_VB + 512]
    kb32_ref[0] = kb
    vb32_ref[0] = vb
    kb16_ref[0] = kb.astype(BF16)
    vb16_ref[0] = vb.astype(BF16)
    logf = _log_sigmoid(y[:, AB_FB:AB_FB + LANES] + bf_ref[...])
    logf128_ref[0] = logf
    logf_ref[0] = logf[:, :N_HEADS]
    ft = _dot_nt(wft_ref[...], hb)
    logft_ref[0] = _log_sigmoid(ft[:N_HEADS] + bft_ref[...])


def _in_ab(x, sh, sc, gpre, P, tabs, tm):
    nb, rows, _ = x.shape
    nt = rows // tm
    row3 = lambda w: pl.BlockSpec((1, tm, w), lambda b, t: (b, t, 0))
    mod = pl.BlockSpec((1, 1, D_MODEL), lambda b, t: (b, 0, 0))
    const = lambda a: pl.BlockSpec(a.shape, lambda b, t: (0,) * a.ndim)
    tab = pl.BlockSpec((tm, LANES), lambda b, t: (t, 0))
    out_shapes = [
        jax.ShapeDtypeStruct((nb, rows, N_HEADS * LANES), BF16),
        jax.ShapeDtypeStruct((nb, rows, KV_RANK), F32),
        jax.ShapeDtypeStruct((nb, rows, ROPE_DIM), F32),
        jax.ShapeDtypeStruct((nb, rows, 2 * LANES), BF16),
        jax.ShapeDtypeStruct((nb, rows, 512), BF16),
        jax.ShapeDtypeStruct((nb, rows, 512), BF16),
        jax.ShapeDtypeStruct((nb, rows, 512), BF16),
        jax.ShapeDtypeStruct((nb, rows, 512), F32),
        jax.ShapeDtypeStruct((nb, rows, 512), F32),
        jax.ShapeDtypeStruct((nb, rows, N_HEADS), F32),
        jax.ShapeDtypeStruct((nb, rows, LANES), F32),
        jax.ShapeDtypeStruct((nb, N_HEADS, rows), F32),
    ]
    out_specs = [row3(N_HEADS * LANES), row3(KV_RANK), row3(ROPE_DIM), row3(2 * LANES), row3(512), row3(512),
                 row3(512), row3(512), row3(512), row3(N_HEADS), row3(LANES),
                 pl.BlockSpec((1, N_HEADS, tm), lambda b, t: (b, 0, t))]
    consts = [P["w_ab"], P["w_ft"], P["bf128"], P["bft"], P["gq"], P["gkv"], P["wq"]]
    return pl.pallas_call(
        _in_ab_kernel,
        grid=(nb, nt),
        in_specs=[row3(D_MODEL), mod, mod, const(gpre)] + [const(a) for a in consts] + [tab] * 4,
        out_specs=out_specs,
        out_shape=out_shapes,
        compiler_params=_params(2),
        name="in_ab",
    )(x, sh, sc, gpre, *consts, *tabs)


def _kvup_kernel(ckr_ref, w_ref, k_ref, v_ref):
    y = _dot(ckr_ref[0], w_ref[...])
    k_ref[0] = y[:, :N_HEADS * LANES].astype(BF16)
    v_ref[0] = y[:, N_HEADS * LANES:].astype(BF16)


def _kvup(ckr, w, tm):
    nb, rows, _ = ckr.shape
    return pl.pallas_call(
        _kvup_kernel,
        grid=(nb, rows // tm),
        in_specs=[pl.BlockSpec((1, tm, 2 * LANES), lambda b, t: (b, t, 0)),
                  pl.BlockSpec(w.shape, lambda b, t: (0, 0))],
        out_specs=[pl.BlockSpec((1, tm, N_HEADS * LANES), lambda b, t: (b, t, 0)),
                   pl.BlockSpec((1, tm, 512), lambda b, t: (b, t, 0))],
        out_shape=[jax.ShapeDtypeStruct((nb, rows, N_HEADS * LANES), BF16),
                   jax.ShapeDtypeStruct((nb, rows, 512), BF16)],
        compiler_params=_params(2),
        name="kvup",
    )(ckr, w)


def _cum_kernel(x_ref, xt_ref, c_ref, ct_ref, *, n_chunks, tc):
    r = lax.broadcasted_iota(jnp.int32, (tc, tc), 0)
    c = lax.broadcasted_iota(jnp.int32, (tc, tc), 1)
    lower = jnp.where(c <= r, 1.0, 0.0).astype(BF16)
    upper = jnp.where(r <= c, 1.0, 0.0).astype(BF16)
    carry = jnp.zeros((1, LANES), F32)
    carry_t = jnp.zeros((N_HEADS, 1), F32)
    for ci in range(n_chunks):
        sl = slice(ci * tc, (ci + 1) * tc)
        cs = carry
        for p in _split_bf16(x_ref[0, sl, :], 3):
            cs = cs + _dot(lower, p)
        c_ref[0, sl, :] = cs
        carry = cs[tc - 1:tc, :]
        xt = jnp.concatenate([xt_ref[0, :, sl], jnp.zeros((N_HEADS, tc), F32)], axis=0)
        cst = jnp.zeros((2 * N_HEADS, tc), F32)
        for p in _split_bf16(xt, 3):
            cst = cst + _dot(p, upper)
        cst = cst[:N_HEADS] + carry_t
        ct_ref[0, :, sl] = cst
        carry_t = cst[:, tc - 1:tc]


def _cum(logf128, logft):
    nb, tk, _ = logf128.shape
    tc = ATT_BLOCK
    return pl.pallas_call(
        functools.partial(_cum_kernel, n_chunks=tk // tc, tc=tc),
        grid=(nb,),
        in_specs=[pl.BlockSpec((1, tk, LANES), lambda b: (b, 0, 0)),
                  pl.BlockSpec((1, N_HEADS, tk), lambda b: (b, 0, 0))],
        out_specs=[pl.BlockSpec((1, tk, LANES), lambda b: (b, 0, 0)),
                   pl.BlockSpec((1, N_HEADS, tk), lambda b: (b, 0, 0))],
        out_shape=[jax.ShapeDtypeStruct((nb, tk, LANES), F32), jax.ShapeDtypeStruct((nb, N_HEADS, tk), F32)],
        compiler_params=_params(1),
        name="cum",
    )(logf128, logft)


def _flash_kernel(*refs, mode, tq, tk, q_off, n_valid):
    if mode == "fox":
        q_ref, k_ref, v_ref, cq_ref, ckt_ref, o_ref, m_s, l_s, acc_s = refs
    else:
        q_ref, k_ref, v_ref, o_ref, m_s, l_s, acc_s = refs
    i = pl.program_id(2)
    qpos0 = q_off + i * tq
    n_full = qpos0 // tk
    nrep = tk // LANES
    lane = lax.broadcasted_iota(jnp.int32, (tq, LANES), 1)
    row = lax.broadcasted_iota(jnp.int32, (tq, tk), 0)
    col = lax.broadcasted_iota(jnp.int32, (tq, tk), 1)
    kpos = n_full * tk + col
    qpos = qpos0 + row
    if mode == "fox":
        mask = kpos <= qpos
    else:
        mask = (lax.shift_right_logical(kpos, 6) <= lax.shift_right_logical(qpos, 6)) & (kpos < n_valid)

    for h in range(2):
        if mode == "fox":
            q = q_ref[0]
            qh = jnp.where((lane < HEAD_DIM) if h == 0 else (lane >= HEAD_DIM), q, jnp.zeros_like(q))
            kcols = slice(0, LANES)
            cq_tile = _tile_lanes(jnp.broadcast_to(cq_ref[0, 0][:, h:h + 1], (tq, LANES)), nrep)
        else:
            qh = q_ref[0, :, h * LANES:(h + 1) * LANES]
            kcols = slice(h * LANES, (h + 1) * LANES)
        m_s[h] = jnp.full((tq, LANES), NEG_INF, F32)
        l_s[h] = jnp.zeros((tq, LANES), F32)
        acc_s[h] = jnp.zeros((tq, LANES), F32)

        def step(j, masked, h=h, qh=qh, kcols=kcols):
            r0 = pl.multiple_of(j * tk, tk)
            k = k_ref[0, pl.ds(r0, tk), kcols]
            v = v_ref[0, pl.ds(r0, tk), :]
            s = _dot_nt(qh, k)
            if mode == "fox":
                s = s + (cq_tile - ckt_ref[0, 0, j][h:h + 1, :])
            if masked:
                s = jnp.where(mask, s, NEG_INF)
            m_prev = m_s[h]
            m_next = jnp.maximum(m_prev, jnp.max(s, axis=1, keepdims=True))
            p = jnp.exp(s - _tile_lanes(m_next, nrep))
            alpha = jnp.exp(m_prev - m_next)
            l_s[h] = alpha * l_s[h] + jnp.sum(p, axis=1, keepdims=True)
            acc_s[h] = alpha * acc_s[h] + _dot(p.astype(BF16), v)
            m_s[h] = m_next

        def body(j, carry):
            step(j, False)
            return carry

        lax.fori_loop(0, n_full, body, 0)
        step(n_full, True)

    o0 = acc_s[0] / l_s[0]
    o1 = acc_s[1] / l_s[1]
    o_ref[0] = jnp.where(lane < HEAD_DIM, o0, o1).astype(BF16)


def _flash(mode, q, k, v, cq=None, ckt=None, *, tq, q_off, n_valid):
    nb, rows, _ = q.shape
    tkk = k.shape[1]
    tk = ATT_BLOCK
    kw = 2 * LANES if mode == "mla" else LANES
    in_specs = [pl.BlockSpec((1, tq, kw), lambda b, p, i: (b, i, p)),
                pl.BlockSpec((1, tkk, kw), lambda b, p, i: (b, 0, p)),
                pl.BlockSpec((1, tkk, LANES), lambda b, p, i: (b, 0, p))]
    args = [q, k, v]
    if mode == "fox":
        in_specs += [pl.BlockSpec((1, 1, tq, 2), lambda b, p, i: (b, p, i, 0)),
                     pl.BlockSpec((1, 1, tkk // tk, 2, tk), lambda b, p, i: (b, p, 0, 0, 0))]
        args += [cq, ckt]
    return pl.pallas_call(
        functools.partial(_flash_kernel, mode=mode, tq=tq, tk=tk, q_off=q_off, n_valid=n_valid),
        grid=(nb, N_PAIRS, rows // tq),
        in_specs=in_specs,
        out_specs=pl.BlockSpec((1, tq, LANES), lambda b, p, i: (b, i, p)),
        out_shape=jax.ShapeDtypeStruct((nb, rows, 512), BF16),
        scratch_shapes=[pltpu.VMEM((2, tq, LANES), F32)] * 3,
        compiler_params=_params(3),
        name="attn_" + mode,
    )(*args)


def _sb_kernel(q_ref, k_ref, v_ref, o_ref, acc_s, r_s, u_s, *, tq, tk, q_off):
    i = pl.program_id(2)
    qpos0 = q_off + i * tq
    n_full = qpos0 // tk
    nrep = tk // LANES
    lane = lax.broadcasted_iota(jnp.int32, (tq, LANES), 1)
    ur = lax.broadcasted_iota(jnp.int32, (tk, tk), 0)
    uc = lax.broadcasted_iota(jnp.int32, (tk, tk), 1)
    u_s[...] = jnp.where(ur > uc, 1.0, 0.0).astype(BF16)
    row = lax.broadcasted_iota(jnp.int32, (tq, tk), 0)
    col = lax.broadcasted_iota(jnp.int32, (tq, tk), 1)
    mask = (n_full * tk + col) < (qpos0 + row)
    q = q_ref[0]

    for h in range(2):
        qh = jnp.where((lane < HEAD_DIM) if h == 0 else (lane >= HEAD_DIM), q, jnp.zeros_like(q))
        acc_s[h] = jnp.zeros((tq, LANES), F32)
        r_s[...] = jnp.zeros((tq, LANES), F32)

        def step(j, masked, h=h, qh=qh):
            r0 = pl.multiple_of(j * tk, tk)
            k = k_ref[0, pl.ds(r0, tk), :]
            v = v_ref[0, pl.ds(r0, tk), :]
            z = _dot_nt(qh, k)
            t = jnp.log1p(jnp.exp(-jnp.abs(z)))
            lsz = jnp.minimum(z, 0.0) - t
            lk = -(jnp.maximum(z, 0.0) + t)
            if masked:
                lk = jnp.where(mask, lk, 0.0)
            hi, lo = _split_bf16(lk, 2)
            u = u_s[...]
            after = _dot(hi, u) + _dot(lo, u) + _tile_lanes(r_s[...], nrep)
            w = jnp.exp(lsz + after)
            if masked:
                w = jnp.where(mask, w, 0.0)
            acc_s[h] = acc_s[h] + _dot(w.astype(BF16), v)
            r_s[...] = r_s[...] + jnp.sum(lk, axis=1, keepdims=True)

        step(n_full, True)

        def body(jj, carry):
            step(n_full - 1 - jj, False)
            return carry

        lax.fori_loop(0, n_full, body, 0)

    o_ref[0] = jnp.where(lane < HEAD_DIM, acc_s[0], acc_s[1]).astype(BF16)


def _sb(q, k, v, *, tq, q_off):
    nb, rows, _ = q.shape
    tkk = k.shape[1]
    tk = ATT_BLOCK
    return pl.pallas_call(
        functools.partial(_sb_kernel, tq=tq, tk=tk, q_off=q_off),
        grid=(nb, N_PAIRS, rows // tq),
        in_specs=[pl.BlockSpec((1, tq, LANES), lambda b, p, i: (b, i, p)),
                  pl.BlockSpec((1, tkk, LANES), lambda b, p, i: (b, 0, p)),
                  pl.BlockSpec((1, tkk, LANES), lambda b, p, i: (b, 0, p))],
        out_specs=pl.BlockSpec((1, tq, LANES), lambda b, p, i: (b, i, p)),
        out_shape=jax.ShapeDtypeStruct((nb, rows, 512), BF16),
        scratch_shapes=[pltpu.VMEM((2, tq, LANES), F32), pltpu.VMEM((tq, LANES), F32), pltpu.VMEM((tk, tk), BF16)],
        compiler_params=_params(3),
        name="attn_sb",
    )(q, k, v)


def _bias_kernel(tab_ref, o_ref, *, nq, nk, q0, k0):
    h = pl.program_id(0)
    row = lax.broadcasted_iota(jnp.int32, (nq, nk), 0)
    col = lax.broadcasted_iota(jnp.int32, (nq, nk), 1)
    idx = jnp.clip((q0 + row) - (k0 + col), -REL_CLIP, REL_CLIP) + REL_CLIP

    def body(r, acc):
        return jnp.where(idx == r, tab_ref[r * N_HEADS + h], acc)

    o_ref[0] = lax.fori_loop(0, 2 * REL_CLIP + 1, body, jnp.zeros((nq, nk), F32))


def _band_bias(rel_table, nq, nk, q0, k0):
    return pl.pallas_call(
        functools.partial(_bias_kernel, nq=nq, nk=nk, q0=q0, k0=k0),
        grid=(N_HEADS,),
        in_specs=[pl.BlockSpec(memory_space=pltpu.SMEM)],
        out_specs=pl.BlockSpec((1, nq, nk), lambda h: (h, 0, 0)),
        out_shape=jax.ShapeDtypeStruct((N_HEADS, nq, nk), F32),
        compiler_params=_params(1),
        name="band_bias",
    )(rel_table.reshape(-1))


def _band_kernel(q_ref, k_ref, v_ref, bias_ref, o_ref, *, cq, n_chunk, win, min_col, pad_rows):
    i = pl.program_id(2)
    lane = lax.broadcasted_iota(jnp.int32, (cq, LANES), 1)
    col = lax.broadcasted_iota(jnp.int32, (cq, win), 1)
    for c in range(n_chunk):
        cg = i * n_chunk + c
        start = pl.multiple_of(cg * cq, cq)
        k = k_ref[0, pl.ds(start, win), :]
        v = v_ref[0, pl.ds(start, win), :]
        q = q_ref[0, c * cq:(c + 1) * cq, :]
        valid = col >= jnp.maximum(min_col, pad_rows - cg * cq)
        outs = []
        for h in range(2):
            qh = jnp.where((lane < HEAD_DIM) if h == 0 else (lane >= HEAD_DIM), q, jnp.zeros_like(q))
            s = jnp.where(valid, _dot_nt(qh, k) + bias_ref[h], NEG_INF)
            p = jnp.exp(s - jnp.max(s, axis=1, keepdims=True))
            outs.append(_dot(p.astype(BF16), v) / jnp.sum(p, axis=1, keepdims=True))
        o_ref[0, c * cq:(c + 1) * cq, :] = jnp.where(lane < HEAD_DIM, outs[0], outs[1]).astype(BF16)


def _band(q, k_pad, v_pad, bias, *, cq, n_chunk, min_col, pad_rows):
    nb, rows, _ = q.shape
    tkk = k_pad.shape[1]
    tq = cq * n_chunk
    return pl.pallas_call(
        functools.partial(_band_kernel, cq=cq, n_chunk=n_chunk, win=BAND_WIN, min_col=min_col, pad_rows=pad_rows),
        grid=(nb, N_PAIRS, rows // tq),
        in_specs=[pl.BlockSpec((1, tq, LANES), lambda b, p, i: (b, i, p)),
                  pl.BlockSpec((1, tkk, LANES), lambda b, p, i: (b, 0, p)),
                  pl.BlockSpec((1, tkk, LANES), lambda b, p, i: (b, 0, p)),
                  pl.BlockSpec((2, cq, BAND_WIN), lambda b, p, i: (p, 0, 0))],
        out_specs=pl.BlockSpec((1, tq, LANES), lambda b, p, i: (b, i, p)),
        out_shape=jax.ShapeDtypeStruct((nb, rows, 512), BF16),
        compiler_params=_params(3),
        name="attn_band",
    )(q, k_pad, v_pad, bias)


def _in_cd_kernel(x_ref, sh_ref, sc_ref, gpre_ref, w_ref,
                  qc_ref, kc16_ref, vc16_ref, qd_ref, kd16_ref, vd16_ref, kc32_ref, vc32_ref, kd32_ref, vd32_ref):
    x = x_ref[0]
    h = _rms(x, gpre_ref[...]) * (1.0 + sc_ref[0]) + sh_ref[0]
    y = _dot(h.astype(BF16), w_ref[...])
    qc_ref[0] = (y[:, 0:512] * 0.125).astype(BF16)
    qd_ref[0] = (y[:, 1536:2048] * 0.125).astype(BF16)
    for lo, r32, r16 in ((512, kc32_ref, kc16_ref), (1024, vc32_ref, vc16_ref),
                         (2048, kd32_ref, kd16_ref), (2560, vd32_ref, vd16_ref)):
        part = y[:, lo:lo + 512]
        r32[0] = part
        r16[0] = part.astype(BF16)


def _in_cd(x, sh, sc, gpre, w, tm):
    nb, rows, _ = x.shape
    row3 = lambda wd: pl.BlockSpec((1, tm, wd), lambda b, t: (b, t, 0))
    mod = pl.BlockSpec((1, 1, D_MODEL), lambda b, t: (b, 0, 0))
    const = lambda a: pl.BlockSpec(a.shape, lambda b, t: (0,) * a.ndim)
    return pl.pallas_call(
        _in_cd_kernel,
        grid=(nb, rows // tm),
        in_specs=[row3(D_MODEL), mod, mod, const(gpre), const(w)],
        out_specs=[row3(512)] * 10,
        out_shape=[jax.ShapeDtypeStruct((nb, rows, 512), BF16)] * 6 + [jax.ShapeDtypeStruct((nb, rows, 512), F32)] * 4,
        compiler_params=_params(2),
        name="in_cd",
    )(x, sh, sc, gpre, w)


def _out_kernel(o1_ref, o2_ref, w1_ref, w2_ref, x_ref, gate_ref, gpost_ref, xo_ref):
    o = _dot(o1_ref[0], w1_ref[...]) + _dot(o2_ref[0], w2_ref[...])
    xo_ref[0] = x_ref[0] + gate_ref[0] * _rms(o, gpost_ref[...])


def _out_proj(o1, o2, w1, w2, x, gate, gpost, tm):
    nb, rows, _ = x.shape
    row3 = lambda wd: pl.BlockSpec((1, tm, wd), lambda b, t: (b, t, 0))
    mod = pl.BlockSpec((1, 1, D_MODEL), lambda b, t: (b, 0, 0))
    const = lambda a: pl.BlockSpec(a.shape, lambda b, t: (0,) * a.ndim)
    return pl.pallas_call(
        _out_kernel,
        grid=(nb, rows // tm),
        in_specs=[row3(512), row3(512), const(w1), const(w2), row3(D_MODEL), mod, const(gpost)],
        out_specs=row3(D_MODEL),
        out_shape=jax.ShapeDtypeStruct((nb, rows, D_MODEL), F32),
        compiler_params=_params(2),
        name="out_proj",
    )(o1, o2, w1, w2, x, gate, gpost)


def _ffn_kernel(x_ref, sh_ref, sc_ref, gate_ref, gpre_ref, gpost_ref, prev_ref, wg_ref, wu_ref, cw_ref, cb_ref, wd_ref,
                xo_ref, conv_ref, gbuf_ref, *, tm):
    t = pl.program_id(1)
    x = x_ref[0]
    h = (_rms(x, gpre_ref[...]) * (1.0 + sc_ref[0]) + sh_ref[0]).astype(BF16)

    @pl.when(t == 0)
    def _():
        gbuf_ref[0:8, :] = jnp.zeros((8, D_FF), F32)
        gbuf_ref[8 - (CONV_W - 1):8, :] = prev_ref[0]

    acc = jnp.zeros((tm, D_MODEL), F32)
    for c in range(N_FF_CHUNKS):
        cols = slice(c * FF_CHUNK, (c + 1) * FF_CHUNK)
        g = _dot(h, wg_ref[c])
        u = _dot(h, wu_ref[c])
        gbuf_ref[8:8 + tm, cols] = g
        g1 = gbuf_ref[7:7 + tm, cols]
        g2 = gbuf_ref[6:6 + tm, cols]
        cw = cw_ref[c]
        gc = cb_ref[c] + (cw[0:1] * g2 + cw[1:2] * g1 + cw[2:3] * g)
        a = (gc * jax.nn.sigmoid(gc)) * u
        acc = acc + _dot(a.astype(BF16), wd_ref[c])
    conv_ref[0] = gbuf_ref[tm + 6:tm + 8, :]
    gbuf_ref[0:8, :] = gbuf_ref[tm:tm + 8, :]
    xo_ref[0] = x + gate_ref[0] * _rms(acc, gpost_ref[...])


def _ffn(x, sh, sc, gate, gpre, gpost, prev, P, tm):
    nb, rows, _ = x.shape
    row3 = lambda wd: pl.BlockSpec((1, tm, wd), lambda b, t: (b, t, 0))
    mod = pl.BlockSpec((1, 1, D_MODEL), lambda b, t: (b, 0, 0))
    const = lambda a: pl.BlockSpec(a.shape, lambda b, t: (0,) * a.ndim)
    conv = pl.BlockSpec((1, CONV_W - 1, D_FF), lambda b, t: (b, 0, 0))
    consts = [P["wg"], P["wu"], P["cw"], P["cb"], P["wd"]]
    resident = lambda a: pl.BlockSpec(a.shape, lambda b, t: (0,) * a.ndim, pipeline_mode=pl.Buffered(1))
    return pl.pallas_call(
        functools.partial(_ffn_kernel, tm=tm),
        grid=(nb, rows // tm),
        in_specs=[row3(D_MODEL), mod, mod, mod, const(gpre), const(gpost), conv] + [resident(a) for a in consts],
        out_specs=[row3(D_MODEL), conv],
        out_shape=[jax.ShapeDtypeStruct((nb, rows, D_MODEL), F32),
                   jax.ShapeDtypeStruct((nb, CONV_W - 1, D_FF), F32)],
        scratch_shapes=[pltpu.VMEM((tm + 8, D_FF), F32)],
        compiler_params=_params(2),
        name="ffn",
    )(x, sh, sc, gate, gpre, gpost, prev, *consts)


def _rot_half_cols(w):
    half = ROPE_DIM // 2
    return jnp.concatenate([-w[..., half:], w[..., :half]], axis=-1)


def _prep_ab(w_in_ab, b_f, q_a_g, kv_a_g, w_uq, w_ukv, w_out_ab):
    z = lambda n: jnp.zeros((D_MODEL, n), F32)
    c_q, c_kv, k_r = w_in_ab[:, 0:256], w_in_ab[:, 256:384], w_in_ab[:, 384:416]
    q_b, k_b, v_b, f_b = w_in_ab[:, 416:928], w_in_ab[:, 928:1440], w_in_ab[:, 1440:1952], w_in_ab[:, 1952:1960]
    w_ab = jnp.concatenate([c_q, c_kv, k_r, z(LANES - ROPE_DIM), _rot_half_cols(k_r), z(LANES - ROPE_DIM),
                            q_b, k_b, v_b, f_b, z(LANES - N_HEADS)], axis=1).astype(BF16)
    w_ft = jnp.concatenate([f_b.T, jnp.zeros((N_HEADS, D_MODEL), F32)], axis=0).astype(BF16)
    uq = w_uq.reshape(Q_RANK, N_HEADS, NOPE_DIM + ROPE_DIM)
    nope, rope_w = uq[..., :NOPE_DIM], uq[..., NOPE_DIM:]
    zq = lambda n: jnp.zeros((Q_RANK, N_HEADS, n), F32)
    wq_main = jnp.concatenate([nope, rope_w, zq(LANES - NOPE_DIM - ROPE_DIM)], axis=-1)
    wq_rot = jnp.concatenate([zq(NOPE_DIM), _rot_half_cols(rope_w), zq(LANES - NOPE_DIM - ROPE_DIM)], axis=-1)
    wq = jnp.concatenate([wq_main.reshape(Q_RANK, -1), wq_rot.reshape(Q_RANK, -1)], axis=1).astype(BF16)
    ukv = w_ukv.reshape(KV_RANK, N_HEADS, 2 * HEAD_DIM)
    wk = jnp.concatenate([ukv[..., :NOPE_DIM], jnp.zeros((KV_RANK, N_HEADS, LANES - NOPE_DIM), F32)], axis=-1)
    place = np.zeros((LANES, N_HEADS, LANES), np.float32)
    for r in range(ROPE_DIM):
        place[r, :, NOPE_DIM + r] = 1.0
    wkv = jnp.concatenate([
        jnp.concatenate([wk.reshape(KV_RANK, -1), ukv[..., NOPE_DIM:].reshape(KV_RANK, -1)], axis=1),
        jnp.concatenate([jnp.asarray(place.reshape(LANES, -1)), jnp.zeros((LANES, 512), F32)], axis=1),
    ], axis=0).astype(BF16)
    return {
        "w_ab": w_ab, "w_ft": w_ft, "wq": wq, "wkv": wkv,
        "bf128": jnp.concatenate([b_f, jnp.zeros((LANES - N_HEADS,), F32)]).reshape(1, LANES),
        "bft": b_f.reshape(N_HEADS, 1),
        "gq": q_a_g.reshape(1, Q_RANK), "gkv": kv_a_g.reshape(1, KV_RANK),
        "wo1": w_out_ab[:512].astype(BF16), "wo2": w_out_ab[512:].astype(BF16),
    }


def _prep_ffn(w_gate, w_up, conv_w, conv_b, w_down):
    chunk_cols = lambda w: w.reshape(D_MODEL, N_FF_CHUNKS, FF_CHUNK).transpose(1, 0, 2).astype(BF16)
    return {
        "wg": chunk_cols(w_gate), "wu": chunk_cols(w_up),
        "cw": conv_w.reshape(CONV_W, N_FF_CHUNKS, FF_CHUNK).transpose(1, 0, 2),
        "cb": conv_b.reshape(N_FF_CHUNKS, 1, FF_CHUNK),
        "wd": w_down.reshape(N_FF_CHUNKS, FF_CHUNK, D_MODEL).astype(BF16),
    }


def _rope_tables(pos):
    half = ROPE_DIM // 2
    inv = ROPE_BASE ** (-jnp.arange(half, dtype=F32) / half)
    ang = pos.astype(F32)[:, None] * inv[None, :]
    cos2 = jnp.concatenate([jnp.cos(ang)] * 2, axis=1)
    sin2 = jnp.concatenate([jnp.sin(ang)] * 2, axis=1)
    n = pos.shape[0]
    scale = (NOPE_DIM + ROPE_DIM) ** -0.5
    zeros = lambda w: jnp.zeros((n, w), F32)
    tqc = scale * jnp.concatenate([jnp.ones((n, NOPE_DIM), F32), cos2, zeros(LANES - NOPE_DIM - ROPE_DIM)], axis=1)
    tqs = scale * jnp.concatenate([zeros(NOPE_DIM), sin2, zeros(LANES - NOPE_DIM - ROPE_DIM)], axis=1)
    tkc = jnp.concatenate([cos2, zeros(LANES - ROPE_DIM)], axis=1)
    tks = jnp.concatenate([sin2, zeros(LANES - ROPE_DIM)], axis=1)
    return tqc, tqs, tkc, tks


def _pad_rows(a, front, total):
    return jnp.pad(a, ((0, 0), (front, total - front - a.shape[1])) + ((0, 0),) * (a.ndim - 2))


def _trunk(x, mods, past, PA, PF, w_cd, wo_cd, rel, gains, tm, tq):
    nb, rows, _ = x.shape
    q_off = 0 if past is None else PAST_LEN
    pos = q_off + jnp.arange(rows)
    tabs = _rope_tables(pos)
    mix_pre_g, mix_post_g, ffn_pre_g, ffn_post_g = gains
    out = {}

    def mod6(l):
        m = mods[l].reshape(nb, 6, 1, D_MODEL)
        return [m[:, j] for j in range(6)]

    sh_m, sc_m, g_m, sh_f, sc_f, g_f = mod6(0)
    (qcomb, ckv, krope, ckr, qb, kb16, vb16, kb32, vb32, logf, logf128, logft) = _in_ab(
        x, sh_m, sc_m, mix_pre_g[0:1], PA, tabs, tm)
    out["a_ckv"], out["a_krope"], out["b_k"], out["b_v"], out["b_logf"] = ckv, krope, kb32, vb32, logf
    if past is None:
        tkk = rows
        ckr_all, kb_all, vb_all, lf_all, lft_all = ckr, kb16, vb16, logf128, logft
    else:
        tkk = -(-(PAST_LEN + rows) // ATT_BLOCK) * ATT_BLOCK
        cat = lambda p, n: _pad_rows(jnp.concatenate([p, n], axis=1), 0, tkk)
        ckr_past = jnp.concatenate([past["a_ckv"].astype(BF16),
                                    jnp.pad(past["a_krope"].astype(BF16), ((0, 0), (0, 0), (0, LANES - ROPE_DIM)))],
                                   axis=-1)
        ckr_all = cat(ckr_past, ckr)
        kb_all = cat(past["b_k"].reshape(nb, PAST_LEN, 512).astype(BF16), kb16)
        vb_all = cat(past["b_v"].reshape(nb, PAST_LEN, 512).astype(BF16), vb16)
        lf_all = cat(jnp.pad(past["b_logf"], ((0, 0), (0, 0), (0, LANES - N_HEADS))), logf128)
        lft_all = jnp.pad(jnp.concatenate([jnp.swapaxes(past["b_logf"], 1, 2), logft], axis=2),
                          ((0, 0), (0, 0), (0, tkk - PAST_LEN - rows)))
    kcomb, va = _kvup(ckr_all, PA["wkv"], min(tm * 2, tkk) if past is None else ATT_BLOCK)
    cum, cumt = _cum(lf_all, lft_all)
    cq = cum[:, q_off:q_off + rows, :N_HEADS].reshape(nb, rows, N_PAIRS, 2).transpose(0, 2, 1, 3)
    nkb = tkk // ATT_BLOCK
    ckt = cumt.reshape(nb, N_PAIRS, 2, nkb, ATT_BLOCK).transpose(0, 1, 3, 2, 4)
    o_a = _flash("mla", qcomb, kcomb, va, tq=tq, q_off=q_off, n_valid=q_off + rows)
    o_b = _flash("fox", qb, kb_all, vb_all, cq, ckt, tq=tq, q_off=q_off, n_valid=q_off + rows)
    x = _out_proj(o_a, o_b, PA["wo1"], PA["wo2"], x, g_m, mix_post_g[0:1], tm)
    prev = jnp.zeros((nb, CONV_W - 1, D_FF), F32) if past is None else past["ffn_conv"][0]
    x, conv0 = _ffn(x, sh_f, sc_f, g_f, ffn_pre_g[0:1], ffn_post_g[0:1], prev, PF[0], tm)

    sh_m, sc_m, g_m, sh_f, sc_f, g_f = mod6(1)
    qc, kc16, vc16, qd, kd16, vd16, kc32, vc32, kd32, vd32 = _in_cd(x, sh_m, sc_m, mix_pre_g[1:2], w_cd, tm)
    out["d_k"], out["d_v"] = kd32, vd32
    if past is None:
        keep = min(BAND_CHUNKS * CHUNK, rows)
        out["c_k"], out["c_v"] = kc32[:, rows - keep:], vc32[:, rows - keep:]
        front = BAND_WIN - CHUNK
        kc_all = _pad_rows(kc16, front, front + rows)
        vc_all = _pad_rows(vc16, front, front + rows)
        bias = _band_bias(rel, CHUNK, BAND_WIN, 0, -front)
        o_c = _band(qc, kc_all, vc_all, bias, cq=CHUNK, n_chunk=tq // CHUNK, min_col=CHUNK, pad_rows=front)
        kd_all, vd_all = kd16, vd16
    else:
        out["c_k"], out["c_v"] = kc32, vc32
        n_c = past["c_k"].shape[1]
        front = BAND_WIN - n_c - rows
        kc_all = _pad_rows(jnp.concatenate([past["c_k"].reshape(nb, n_c, 512).astype(BF16), kc16], axis=1), front, BAND_WIN)
        vc_all = _pad_rows(jnp.concatenate([past["c_v"].reshape(nb, n_c, 512).astype(BF16), vc16], axis=1), front, BAND_WIN)
        bias = _band_bias(rel, rows, BAND_WIN, q_off, PAST_LEN - n_c - front)
        o_c = _band(qc, kc_all, vc_all, bias, cq=rows, n_chunk=1, min_col=front, pad_rows=0)
        kd_all = _pad_rows(jnp.concatenate([past["d_k"].reshape(nb, PAST_LEN, 512).astype(BF16), kd16], axis=1), 0, tkk)
        vd_all = _pad_rows(jnp.concatenate([past["d_v"].reshape(nb, PAST_LEN, 512).astype(BF16), vd16], axis=1), 0, tkk)
    o_d = _sb(qd, kd_all, vd_all, tq=tq, q_off=q_off)
    x = _out_proj(o_c, o_d, wo_cd[0], wo_cd[1], x, g_m, mix_post_g[1:2], tm)
    prev = jnp.zeros((nb, CONV_W - 1, D_FF), F32) if past is None else past["ffn_conv"][1]
    x, conv1 = _ffn(x, sh_f, sc_f, g_f, ffn_pre_g[1:2], ffn_post_g[1:2], prev, PF[1], tm)
    out["ffn_conv"] = jnp.stack([conv0, conv1])
    return x, out


def kernel(x_prompt, x_sample, c_prompt, c_sample, cache_a_ckv, cache_a_krope, cache_b_k, cache_b_v, cache_b_logf, cache_c_k, cache_c_v, cache_d_k, cache_d_v, state_ffn_conv, ada_w, ada_b, mix_pre_g, mix_post_g, ffn_pre_g, ffn_post_g, w_in_ab, b_f, q_a_g, kv_a_g, w_uq, w_ukv, w_out_ab, w_in_cd, rel_bias_c, w_out_cd, ffn_w_gate, ffn_w_up, ffn_conv_w, ffn_conv_b, ffn_w_down):
    nbp, nbs = x_prompt.shape[0], x_sample.shape[0]
    c_all = jnp.concatenate([c_prompt, c_sample, jnp.zeros((32 - nbp - nbs, D_MODEL), F32)], axis=0)
    mods = _ada(c_all, ada_w, ada_b)
    PA = _prep_ab(w_in_ab[0], b_f[0], q_a_g[0], kv_a_g[0], w_uq[0], w_ukv[0], w_out_ab[0])
    PF = [_prep_ffn(ffn_w_gate[l], ffn_w_up[l], ffn_conv_w[l], ffn_conv_b[l], ffn_w_down[l]) for l in range(DEPTH)]
    w_cd = w_in_cd[0].astype(BF16)
    wo_cd = (w_out_cd[0][:512].astype(BF16), w_out_cd[0][512:].astype(BF16))
    gains = (mix_pre_g, mix_post_g, ffn_pre_g, ffn_post_g)
    past = {"a_ckv": cache_a_ckv[0], "a_krope": cache_a_krope[0], "b_k": cache_b_k[0], "b_v": cache_b_v[0],
            "b_logf": cache_b_logf[0], "c_k": cache_c_k[0], "c_v": cache_c_v[0], "d_k": cache_d_k[0],
            "d_v": cache_d_v[0], "ffn_conv": state_ffn_conv}
    y_p, sp = _trunk(x_prompt, mods[:, :nbp], None, PA, PF, w_cd, wo_cd, rel_bias_c[0], gains, 512, ATT_BLOCK)
    y_s, ss = _trunk(x_sample, mods[:, nbp:nbp + nbs], past, PA, PF, w_cd, wo_cd, rel_bias_c[0], gains,
                     x_sample.shape[1], x_sample.shape[1])

    def heads(a):
        return a.reshape(a.shape[0], a.shape[1], N_HEADS, HEAD_DIM)[None]

    return (y_p, y_s,
            sp["a_ckv"][None], ss["a_ckv"][None], sp["a_krope"][None], ss["a_krope"][None],
            heads(sp["b_k"]), heads(ss["b_k"]), heads(sp["b_v"]), heads(ss["b_v"]),
            sp["b_logf"][None], ss["b_logf"][None],
            heads(sp["c_k"]), heads(ss["c_k"]), heads(sp["c_v"]), heads(ss["c_v"]),
            heads(sp["d_k"]), heads(ss["d_k"]), heads(sp["d_v"]), heads(ss["d_v"]),
            sp["ffn_conv"], ss["ffn_conv"])
```

```python
import functools

import jax
import jax.numpy as jnp
import numpy as np
from jax import lax
from jax.experimental import pallas as pl
from jax.experimental.pallas import tpu as pltpu

D_MODEL = 1024
DEPTH = 2
PAST_LEN = 1024
CHUNK = 64
HEAD_DIM = 64
N_HEADS = 8
N_PAIRS = N_HEADS // 2
Q_RANK = 256
KV_RANK = 128
NOPE_DIM = 64
ROPE_DIM = 32
ROPE_BASE = 10000.0
BAND_CHUNKS = 8
REL_CLIP = 128
D_FF = 2816
CONV_W = 3
EPS = 1e-6
NEG_INF = -1e30

LANES = 128
FF_CHUNK = 256
N_FF_CHUNKS = D_FF // FF_CHUNK
ATT_BLOCK = 512
ATT_SUB = 256
BAND_SAMPLE_WIN = 640
VMEM_LIMIT_BYTES = 56 * 1024 * 1024

F32 = jnp.float32
BF16 = jnp.bfloat16

AB_CQ, AB_CKV, AB_KR, AB_KRROT, AB_QB, AB_KB, AB_VB, AB_FB, AB_COLS = 0, 256, 384, 512, 640, 1152, 1664, 2176, 2304


def _params(n_axes):
    return pltpu.CompilerParams(dimension_semantics=("arbitrary",) * n_axes, vmem_limit_bytes=VMEM_LIMIT_BYTES)


def _rms(x, g):
    return x * lax.rsqrt(jnp.mean(x * x, axis=-1, keepdims=True) + EPS) * g


def _log_sigmoid(x):
    return jnp.minimum(x, 0.0) - jnp.log1p(jnp.exp(-jnp.abs(x)))


def _dot(a, b):
    return jnp.dot(a, b, preferred_element_type=F32)


def _dot_nt(a, b):
    return lax.dot_general(a, b, (((1,), (1,)), ((), ())), preferred_element_type=F32)


def _split_bf16(x, n):
    parts = []
    for _ in range(n):
        p = x.astype(BF16)
        parts.append(p)
        x = x - p.astype(F32)
    return parts


def _tile_lanes(x, n):
    return x if n == 1 else jnp.concatenate([x] * n, axis=1)


def _ada_kernel(c_ref, w_ref, b_ref, o_ref):
    c = c_ref[...]
    cond = c * jax.nn.sigmoid(c)
    o_ref[0] = _dot(cond.astype(BF16), w_ref[0].astype(BF16)) + b_ref[0]


def _ada(c_all, ada_w, ada_b):
    rows = c_all.shape[0]
    tn = 1536
    return pl.pallas_call(
        _ada_kernel,
        grid=(DEPTH, 6 * D_MODEL // tn),
        in_specs=[
            pl.BlockSpec((rows, D_MODEL), lambda l, j: (0, 0)),
            pl.BlockSpec((1, D_MODEL, tn), lambda l, j: (l, 0, j)),
            pl.BlockSpec((1, 1, tn), lambda l, j: (l, 0, j)),
        ],
        out_specs=pl.BlockSpec((1, rows, tn), lambda l, j: (l, 0, j)),
        out_shape=jax.ShapeDtypeStruct((DEPTH, rows, 6 * D_MODEL), F32),
        compiler_params=_params(2),
        name="ada",
    )(c_all, ada_w, ada_b.reshape(DEPTH, 1, 6 * D_MODEL))


def _in_ab_kernel(x_ref, sh_ref, sc_ref, gpre_ref, w_ref, wft_ref, bf_ref, bft_ref, gq_ref, gkv_ref, wq_ref,
                  tqc_ref, tqs_ref, tkc_ref, tks_ref,
                  qcomb_ref, ckv_ref, krope_ref, ckr_ref, qb_ref, kb16_ref, vb16_ref, kb32_ref, vb32_ref,
                  logf_ref, logf128_ref, logft_ref):
    x = x_ref[0]
    h = _rms(x, gpre_ref[...]) * (1.0 + sc_ref[0]) + sh_ref[0]
    hb = h.astype(BF16)
    y = _dot(hb, w_ref[...])
    cq = _rms(y[:, AB_CQ:AB_CQ + Q_RANK], gq_ref[...]).astype(BF16)
    qa = _dot(cq, wq_ref[...])
    tqc = tqc_ref[...]
    tqs = tqs_ref[...]
    half = N_HEADS * LANES
    for hh in range(N_HEADS):
        lo = hh * LANES
        qcomb_ref[0, :, lo:lo + LANES] = (qa[:, lo:lo + LANES] * tqc + qa[:, half + lo:half + lo + LANES] * tqs).astype(BF16)
    ckv = _rms(y[:, AB_CKV:AB_CKV + KV_RANK], gkv_ref[...])
    ckv_ref[0] = ckv
    kr = y[:, AB_KR:AB_KR + LANES] * tkc_ref[...] + y[:, AB_KRROT:AB_KRROT + LANES] * tks_ref[...]
    krope_ref[0] = kr[:, :ROPE_DIM]
    ckr_ref[0, :, 0:LANES] = ckv.astype(BF16)
    ckr_ref[0, :, LANES:2 * LANES] = kr.astype(BF16)
    qb_ref[0] = (y[:, AB_QB:AB_QB + 512] * 0.125).astype(BF16)
    kb = y[:, AB_KB:AB_KB + 512]
    vb = y[:, AB_VB:AB_VB + 512]
    kb32_ref[0] = kb
    vb32_ref[0] = vb
    kb16_ref[0] = kb.astype(BF16)
    vb16_ref[0] = vb.astype(BF16)
    logf = _log_sigmoid(y[:, AB_FB:AB_FB + LANES] + bf_ref[...])
    logf128_ref[0] = logf
    logf_ref[0] = logf[:, :N_HEADS]
    ft = _dot_nt(wft_ref[...], hb)
    logft_ref[0] = _log_sigmoid(ft[:N_HEADS] + bft_ref[...])


def _in_ab(x, sh, sc, gpre, P, tabs, tm):
    nb, rows, _ = x.shape
    nt = rows // tm
    row3 = lambda w: pl.BlockSpec((1, tm, w), lambda b, t: (b, t, 0))
    mod = pl.BlockSpec((1, 1, D_MODEL), lambda b, t: (b, 0, 0))
    const = lambda a: pl.BlockSpec(a.shape, lambda b, t: (0,) * a.ndim)
    tab = pl.BlockSpec((tm, LANES), lambda b, t: (t, 0))
    out_shapes = [
        jax.ShapeDtypeStruct((nb, rows, N_HEADS * LANES), BF16),
        jax.ShapeDtypeStruct((nb, rows, KV_RANK), F32),
        jax.ShapeDtypeStruct((nb, rows, ROPE_DIM), F32),
        jax.ShapeDtypeStruct((nb, rows, 2 * LANES), BF16),
        jax.ShapeDtypeStruct((nb, rows, 512), BF16),
        jax.ShapeDtypeStruct((nb, rows, 512), BF16),
        jax.ShapeDtypeStruct((nb, rows, 512), BF16),
        jax.ShapeDtypeStruct((nb, rows, 512), F32),
        jax.ShapeDtypeStruct((nb, rows, 512), F32),
        jax.ShapeDtypeStruct((nb, rows, N_HEADS), F32),
        jax.ShapeDtypeStruct((nb, rows, LANES), F32),
        jax.ShapeDtypeStruct((nb, N_HEADS, rows), F32),
    ]
    out_specs = [row3(N_HEADS * LANES), row3(KV_RANK), row3(ROPE_DIM), row3(2 * LANES), row3(512), row3(512),
                 row3(512), row3(512), row3(512), row3(N_HEADS), row3(LANES),
                 pl.BlockSpec((1, N_HEADS, tm), lambda b, t: (b, 0, t))]
    consts = [P["w_ab"], P["w_ft"], P["bf128"], P["bft"], P["gq"], P["gkv"], P["wq"]]
    return pl.pallas_call(
        _in_ab_kernel,
        grid=(nb, nt),
        in_specs=[row3(D_MODEL), mod, mod, const(gpre)] + [const(a) for a in consts] + [tab] * 4,
        out_specs=out_specs,
        out_shape=out_shapes,
        compiler_params=_params(2),
        name="in_ab",
    )(x, sh, sc, gpre, *consts, *tabs)


def _kvup_kernel(ckr_ref, w_ref, k_ref, v_ref):
    y = _dot(ckr_ref[0], w_ref[...])
    k_ref[0] = y[:, :N_HEADS * LANES].astype(BF16)
    v_ref[0] = y[:, N_HEADS * LANES:].astype(BF16)


def _kvup(ckr, w, tm):
    nb, rows, _ = ckr.shape
    return pl.pallas_call(
        _kvup_kernel,
        grid=(nb, rows // tm),
        in_specs=[pl.BlockSpec((1, tm, 2 * LANES), lambda b, t: (b, t, 0)),
                  pl.BlockSpec(w.shape, lambda b, t: (0, 0))],
        out_specs=[pl.BlockSpec((1, tm, N_HEADS * LANES), lambda b, t: (b, t, 0)),
                   pl.BlockSpec((1, tm, 512), lambda b, t: (b, t, 0))],
        out_shape=[jax.ShapeDtypeStruct((nb, rows, N_HEADS * LANES), BF16),
                   jax.ShapeDtypeStruct((nb, rows, 512), BF16)],
        compiler_params=_params(2),
        name="kvup",
    )(ckr, w)


def _cum_kernel(x_ref, xt_ref, c_ref, ct_ref, *, n_chunks, tc):
    r = lax.broadcasted_iota(jnp.int32, (tc, tc), 0)
    c = lax.broadcasted_iota(jnp.int32, (tc, tc), 1)
    lower = jnp.where(c <= r, 1.0, 0.0).astype(BF16)
    upper = jnp.where(r <= c, 1.0, 0.0).astype(BF16)
    carry = jnp.zeros((1, LANES), F32)
    carry_t = jnp.zeros((N_HEADS, 1), F32)
    for ci in range(n_chunks):
        sl = slice(ci * tc, (ci + 1) * tc)
        cs = carry
        for p in _split_bf16(x_ref[0, sl, :], 3):
            cs = cs + _dot(lower, p)
        c_ref[0, sl, :] = cs
        carry = cs[tc - 1:tc, :]
        xt = jnp.concatenate([xt_ref[0, :, sl], jnp.zeros((N_HEADS, tc), F32)], axis=0)
        cst = jnp.zeros((2 * N_HEADS, tc), F32)
        for p in _split_bf16(xt, 3):
            cst = cst + _dot(p, upper)
        cst = cst[:N_HEADS] + carry_t
        ct_ref[0, :, sl] = cst
        carry_t = cst[:, tc - 1:tc]


def _cum(logf128, logft):
    nb, tk, _ = logf128.shape
    tc = ATT_SUB
    return pl.pallas_call(
        functools.partial(_cum_kernel, n_chunks=tk // tc, tc=tc),
        grid=(nb,),
        in_specs=[pl.BlockSpec((1, tk, LANES), lambda b: (b, 0, 0)),
                  pl.BlockSpec((1, N_HEADS, tk), lambda b: (b, 0, 0))],
        out_specs=[pl.BlockSpec((1, tk, LANES), lambda b: (b, 0, 0)),
                   pl.BlockSpec((1, N_HEADS, tk), lambda b: (b, 0, 0))],
        out_shape=[jax.ShapeDtypeStruct((nb, tk, LANES), F32), jax.ShapeDtypeStruct((nb, N_HEADS, tk), F32)],
        compiler_params=_params(1),
        name="cum",
    )(logf128, logft)


def _flash_kernel(*refs, mode, tq, tk, q_off, n_valid, tail_widths):
    if mode == "fox":
        q_ref, k_ref, v_ref, cq_ref, ckt_ref, o_ref, m_s, l_s, acc_s = refs
    else:
        q_ref, k_ref, v_ref, o_ref, m_s, l_s, acc_s = refs
    i = pl.program_id(2)
    qpos0 = q_off + i * tq
    n_full = qpos0 // tk
    rem = qpos0 + tq - n_full * tk
    lane = lax.broadcasted_iota(jnp.int32, (tq, LANES), 1)

    qs, kcols, cq_tiles = [], [], []
    for h in range(2):
        if mode == "fox":
            q = q_ref[0]
            qs.append(jnp.where((lane < HEAD_DIM) if h == 0 else (lane >= HEAD_DIM), q, jnp.zeros_like(q)))
            kcols.append(slice(0, LANES))
            cq_tiles.append(jnp.broadcast_to(cq_ref[0, 0][:, h:h + 1], (tq, LANES)))
        else:
            qs.append(q_ref[0, :, h * LANES:(h + 1) * LANES])
            kcols.append(slice(h * LANES, (h + 1) * LANES))
        m_s[h] = jnp.full((tq, LANES), NEG_INF, F32)
        l_s[h] = jnp.zeros((tq, LANES), F32)
        acc_s[h] = jnp.zeros((tq, LANES), F32)

    def step(j, width, masked):
        nrep = width // LANES
        r0 = pl.multiple_of(j * tk, tk)
        v = v_ref[0, pl.ds(r0, width), :]
        if masked:
            kpos = n_full * tk + lax.broadcasted_iota(jnp.int32, (tq, width), 1)
            qpos = qpos0 + lax.broadcasted_iota(jnp.int32, (tq, width), 0)
            if mode == "fox":
                mask = kpos <= qpos
            else:
                mask = (lax.shift_right_logical(kpos, 6) <= lax.shift_right_logical(qpos, 6)) & (kpos < n_valid)
        for h in range(2):
            s = _dot_nt(qs[h], k_ref[0, pl.ds(r0, width), kcols[h]])
            if mode == "fox":
                s = s + (_tile_lanes(cq_tiles[h], nrep) - ckt_ref[0, 0, j][h:h + 1, :width])
            if masked:
                s = jnp.where(mask, s, NEG_INF)
            m_prev = m_s[h]
            m_next = jnp.maximum(m_prev, jnp.max(s, axis=1, keepdims=True))
            p = jnp.exp(s - _tile_lanes(m_next, nrep))
            alpha = jnp.exp(m_prev - m_next)
            l_s[h] = alpha * l_s[h] + jnp.sum(p, axis=1, keepdims=True)
            acc_s[h] = alpha * acc_s[h] + _dot(p.astype(BF16), v)
            m_s[h] = m_next

    def body(j, carry):
        step(j, tk, False)
        return carry

    lax.fori_loop(0, n_full, body, 0)
    if len(tail_widths) == 1:
        step(n_full, tail_widths[0], True)
    else:
        lo_w, hi_w = tail_widths
        pl.when(rem <= lo_w)(lambda: step(n_full, lo_w, True))
        pl.when(rem > lo_w)(lambda: step(n_full, hi_w, True))

    o0 = acc_s[0] / l_s[0]
    o1 = acc_s[1] / l_s[1]
    o_ref[0] = jnp.where(lane < HEAD_DIM, o0, o1).astype(BF16)


def _tail_widths(tk):
    return (ATT_SUB, tk) if tk > ATT_SUB else (tk,)


def _flash(mode, q, k, v, cq=None, ckt=None, *, tq, q_off, n_valid):
    nb, rows, _ = q.shape
    tkk = k.shape[1]
    tk = ATT_BLOCK
    kw = 2 * LANES if mode == "mla" else LANES
    in_specs = [pl.BlockSpec((1, tq, kw), lambda b, p, i: (b, i, p)),
                pl.BlockSpec((1, tkk, kw), lambda b, p, i: (b, 0, p)),
                pl.BlockSpec((1, tkk, LANES), lambda b, p, i: (b, 0, p))]
    args = [q, k, v]
    if mode == "fox":
        in_specs += [pl.BlockSpec((1, 1, tq, 2), lambda b, p, i: (b, p, i, 0)),
                     pl.BlockSpec((1, 1, tkk // tk, 2, tk), lambda b, p, i: (b, p, 0, 0, 0))]
        args += [cq, ckt]
    return pl.pallas_call(
        functools.partial(_flash_kernel, mode=mode, tq=tq, tk=tk, q_off=q_off, n_valid=n_valid,
                          tail_widths=_tail_widths(tk)),
        grid=(nb, N_PAIRS, rows // tq),
        in_specs=in_specs,
        out_specs=pl.BlockSpec((1, tq, LANES), lambda b, p, i: (b, i, p)),
        out_shape=jax.ShapeDtypeStruct((nb, rows, 512), BF16),
        scratch_shapes=[pltpu.VMEM((2, tq, LANES), F32)] * 3,
        compiler_params=_params(3),
        name="attn_" + mode,
    )(*args)


def _sb_kernel(q_ref, k_ref, v_ref, o_ref, acc_s, r_s, u_s, *, tq, tk, q_off, tail_widths):
    i = pl.program_id(2)
    qpos0 = q_off + i * tq
    n_full = qpos0 // tk
    rem = qpos0 + tq - n_full * tk
    nrep = ATT_SUB // LANES
    lane = lax.broadcasted_iota(jnp.int32, (tq, LANES), 1)
    ur = lax.broadcasted_iota(jnp.int32, (ATT_SUB, ATT_SUB), 0)
    uc = lax.broadcasted_iota(jnp.int32, (ATT_SUB, ATT_SUB), 1)
    u_s[...] = jnp.where(ur > uc, 1.0, 0.0).astype(BF16)
    q = q_ref[0]
    qs = [jnp.where((lane < HEAD_DIM) if h == 0 else (lane >= HEAD_DIM), q, jnp.zeros_like(q)) for h in range(2)]
    acc_s[...] = jnp.zeros((2, tq, LANES), F32)
    r_s[...] = jnp.zeros((2, tq, LANES), F32)

    def step(j, width, masked):
        r0 = pl.multiple_of(j * tk, tk)
        k = k_ref[0, pl.ds(r0, width), :]
        v = v_ref[0, pl.ds(r0, width), :]
        if masked:
            kpos = n_full * tk + lax.broadcasted_iota(jnp.int32, (tq, width), 1)
            qpos = qpos0 + lax.broadcasted_iota(jnp.int32, (tq, width), 0)
            mask = kpos < qpos
        u = u_s[...]
        for h in range(2):
            z = _dot_nt(qs[h], k)
            t = jnp.log1p(jnp.exp(-jnp.abs(z)))
            lsz = jnp.minimum(z, 0.0) - t
            lk = -(jnp.maximum(z, 0.0) + t)
            if masked:
                lk = jnp.where(mask, lk, 0.0)
            r = r_s[h]
            ws = []
            for sb in reversed(range(width // ATT_SUB)):
                cols = slice(sb * ATT_SUB, (sb + 1) * ATT_SUB)
                lk_sb = lk[:, cols]
                hi, lo = _split_bf16(lk_sb, 2)
                after = _dot(hi, u) + _dot(lo, u) + _tile_lanes(r, nrep)
                w = jnp.exp(lsz[:, cols] + after)
                if masked:
                    w = jnp.where(mask[:, cols], w, 0.0)
                ws.append(w.astype(BF16))
                r = r + jnp.sum(lk_sb, axis=1, keepdims=True)
            w_all = ws[0] if len(ws) == 1 else jnp.concatenate(ws[::-1], axis=1)
            acc_s[h] = acc_s[h] + _dot(w_all, v)
            r_s[h] = r

    if len(tail_widths) == 1:
        step(n_full, tail_widths[0], True)
    else:
        lo_w, hi_w = tail_widths
        pl.when(rem <= lo_w)(lambda: step(n_full, lo_w, True))
        pl.when(rem > lo_w)(lambda: step(n_full, hi_w, True))

    def body(jj, carry):
        step(n_full - 1 - jj, tk, False)
        return carry

    lax.fori_loop(0, n_full, body, 0)
    o_ref[0] = jnp.where(lane < HEAD_DIM, acc_s[0], acc_s[1]).astype(BF16)


def _sb(q, k, v, *, tq, q_off):
    nb, rows, _ = q.shape
    tkk = k.shape[1]
    tk = ATT_BLOCK
    return pl.pallas_call(
        functools.partial(_sb_kernel, tq=tq, tk=tk, q_off=q_off, tail_widths=_tail_widths(tk)),
        grid=(nb, N_PAIRS, rows // tq),
        in_specs=[pl.BlockSpec((1, tq, LANES), lambda b, p, i: (b, i, p)),
                  pl.BlockSpec((1, tkk, LANES), lambda b, p, i: (b, 0, p)),
                  pl.BlockSpec((1, tkk, LANES), lambda b, p, i: (b, 0, p))],
        out_specs=pl.BlockSpec((1, tq, LANES), lambda b, p, i: (b, i, p)),
        out_shape=jax.ShapeDtypeStruct((nb, rows, 512), BF16),
        scratch_shapes=[pltpu.VMEM((2, tq, LANES), F32), pltpu.VMEM((2, tq, LANES), F32),
                        pltpu.VMEM((ATT_SUB, ATT_SUB), BF16)],
        compiler_params=_params(3),
        name="attn_sb",
    )(q, k, v)


def _bias_kernel(tab_ref, o_ref, *, nq, nk, q0, k0):
    h = pl.program_id(0)
    qpos = q0 + lax.broadcasted_iota(jnp.int32, (nq, nk), 0)
    kpos = k0 + lax.broadcasted_iota(jnp.int32, (nq, nk), 1)
    idx = jnp.clip(qpos - kpos, -REL_CLIP, REL_CLIP) + REL_CLIP
    qc = lax.shift_right_arithmetic(qpos, 6)
    kc = lax.shift_right_arithmetic(kpos, 6)
    band = (kc <= qc) & (kc >= qc - BAND_CHUNKS)

    def body(r, acc):
        return jnp.where(idx == r, tab_ref[r * N_HEADS + h], acc)

    bias = lax.fori_loop(0, 2 * REL_CLIP + 1, body, jnp.zeros((nq, nk), F32))
    o_ref[0] = jnp.where(band, bias, NEG_INF)


def _band_bias(rel_table, nq, nk, q0, k0):
    return pl.pallas_call(
        functools.partial(_bias_kernel, nq=nq, nk=nk, q0=q0, k0=k0),
        grid=(N_HEADS,),
        in_specs=[pl.BlockSpec(memory_space=pltpu.SMEM)],
        out_specs=pl.BlockSpec((1, nq, nk), lambda h: (h, 0, 0)),
        out_shape=jax.ShapeDtypeStruct((N_HEADS, nq, nk), F32),
        compiler_params=_params(1),
        name="band_bias",
    )(rel_table.reshape(-1))


def _band_kernel(q_ref, k_ref, v_ref, bias_ref, o_ref, *, tq, win, pad_rows):
    i = pl.program_id(2)
    start = pl.multiple_of(i * tq, tq)
    k = k_ref[0, pl.ds(start, win), :]
    v = v_ref[0, pl.ds(start, win), :]
    lane = lax.broadcasted_iota(jnp.int32, (tq, LANES), 1)
    valid = lax.broadcasted_iota(jnp.int32, (tq, win), 1) >= pad_rows - i * tq
    q = q_ref[0]
    outs = []
    for h in range(2):
        qh = jnp.where((lane < HEAD_DIM) if h == 0 else (lane >= HEAD_DIM), q, jnp.zeros_like(q))
        s = jnp.where(valid, _dot_nt(qh, k) + bias_ref[h], NEG_INF)
        p = jnp.exp(s - jnp.max(s, axis=1, keepdims=True))
        outs.append(_dot(p.astype(BF16), v) / jnp.sum(p, axis=1, keepdims=True))
    o_ref[0] = jnp.where(lane < HEAD_DIM, outs[0], outs[1]).astype(BF16)


def _band(q, k_pad, v_pad, bias, *, tq, pad_rows):
    nb, rows, _ = q.shape
    tkk = k_pad.shape[1]
    win = bias.shape[2]
    return pl.pallas_call(
        functools.partial(_band_kernel, tq=tq, win=win, pad_rows=pad_rows),
        grid=(nb, N_PAIRS, rows // tq),
        in_specs=[pl.BlockSpec((1, tq, LANES), lambda b, p, i: (b, i, p)),
                  pl.BlockSpec((1, tkk, LANES), lambda b, p, i: (b, 0, p)),
                  pl.BlockSpec((1, tkk, LANES), lambda b, p, i: (b, 0, p)),
                  pl.BlockSpec((2, tq, win), lambda b, p, i: (p, 0, 0))],
        out_specs=pl.BlockSpec((1, tq, LANES), lambda b, p, i: (b, i, p)),
        out_shape=jax.ShapeDtypeStruct((nb, rows, 512), BF16),
        compiler_params=_params(3),
        name="attn_band",
    )(q, k_pad, v_pad, bias)


def _in_cd_kernel(x_ref, sh_ref, sc_ref, gpre_ref, w_ref,
                  qc_ref, kc16_ref, vc16_ref, qd_ref, kd16_ref, vd16_ref, kc32_ref, vc32_ref, kd32_ref, vd32_ref):
    x = x_ref[0]
    h = _rms(x, gpre_ref[...]) * (1.0 + sc_ref[0]) + sh_ref[0]
    y = _dot(h.astype(BF16), w_ref[...])
    qc_ref[0] = (y[:, 0:512] * 0.125).astype(BF16)
    qd_ref[0] = (y[:, 1536:2048] * 0.125).astype(BF16)
    for lo, r32, r16 in ((512, kc32_ref, kc16_ref), (1024, vc32_ref, vc16_ref),
                         (2048, kd32_ref, kd16_ref), (2560, vd32_ref, vd16_ref)):
        part = y[:, lo:lo + 512]
        r32[0] = part
        r16[0] = part.astype(BF16)


def _in_cd(x, sh, sc, gpre, w, tm):
    nb, rows, _ = x.shape
    row3 = lambda wd: pl.BlockSpec((1, tm, wd), lambda b, t: (b, t, 0))
    mod = pl.BlockSpec((1, 1, D_MODEL), lambda b, t: (b, 0, 0))
    const = lambda a: pl.BlockSpec(a.shape, lambda b, t: (0,) * a.ndim)
    return pl.pallas_call(
        _in_cd_kernel,
        grid=(nb, rows // tm),
        in_specs=[row3(D_MODEL), mod, mod, const(gpre), const(w)],
        out_specs=[row3(512)] * 10,
        out_shape=[jax.ShapeDtypeStruct((nb, rows, 512), BF16)] * 6 + [jax.ShapeDtypeStruct((nb, rows, 512), F32)] * 4,
        compiler_params=_params(2),
        name="in_cd",
    )(x, sh, sc, gpre, w)


def _out_kernel(o1_ref, o2_ref, w1_ref, w2_ref, x_ref, gate_ref, gpost_ref, xo_ref):
    o = _dot(o1_ref[0], w1_ref[...]) + _dot(o2_ref[0], w2_ref[...])
    xo_ref[0] = x_ref[0] + gate_ref[0] * _rms(o, gpost_ref[...])


def _out_proj(o1, o2, w1, w2, x, gate, gpost, tm):
    nb, rows, _ = x.shape
    row3 = lambda wd: pl.BlockSpec((1, tm, wd), lambda b, t: (b, t, 0))
    mod = pl.BlockSpec((1, 1, D_MODEL), lambda b, t: (b, 0, 0))
    const = lambda a: pl.BlockSpec(a.shape, lambda b, t: (0,) * a.ndim)
    return pl.pallas_call(
        _out_kernel,
        grid=(nb, rows // tm),
        in_specs=[row3(512), row3(512), const(w1), const(w2), row3(D_MODEL), mod, const(gpost)],
        out_specs=row3(D_MODEL),
        out_shape=jax.ShapeDtypeStruct((nb, rows, D_MODEL), F32),
        compiler_params=_params(2),
        name="out_proj",
    )(o1, o2, w1, w2, x, gate, gpost)


def _ffn_kernel(x_ref, sh_ref, sc_ref, gate_ref, gpre_ref, gpost_ref, prev_ref, wg_ref, wu_ref, cw_ref, cb_ref, wd_ref,
                xo_ref, conv_ref, gbuf_ref, *, tm):
    t = pl.program_id(1)
    x = x_ref[0]
    h = (_rms(x, gpre_ref[...]) * (1.0 + sc_ref[0]) + sh_ref[0]).astype(BF16)

    @pl.when(t == 0)
    def _():
        gbuf_ref[0:8, :] = jnp.zeros((8, D_FF), F32)
        gbuf_ref[8 - (CONV_W - 1):8, :] = prev_ref[0]

    acc = jnp.zeros((tm, D_MODEL), F32)
    for c in range(N_FF_CHUNKS):
        cols = slice(c * FF_CHUNK, (c + 1) * FF_CHUNK)
        g = _dot(h, wg_ref[c])
        u = _dot(h, wu_ref[c])
        gbuf_ref[8:8 + tm, cols] = g
        g1 = gbuf_ref[7:7 + tm, cols]
        g2 = gbuf_ref[6:6 + tm, cols]
        cw = cw_ref[c]
        gc = cb_ref[c] + (cw[0:1] * g2 + cw[1:2] * g1 + cw[2:3] * g)
        a = (gc * jax.nn.sigmoid(gc)) * u
        acc = acc + _dot(a.astype(BF16), wd_ref[c])
    conv_ref[0] = gbuf_ref[tm + 6:tm + 8, :]
    gbuf_ref[0:8, :] = gbuf_ref[tm:tm + 8, :]
    xo_ref[0] = x + gate_ref[0] * _rms(acc, gpost_ref[...])


def _ffn(x, sh, sc, gate, gpre, gpost, prev, P, tm):
    nb, rows, _ = x.shape
    row3 = lambda wd: pl.BlockSpec((1, tm, wd), lambda b, t: (b, t, 0))
    mod = pl.BlockSpec((1, 1, D_MODEL), lambda b, t: (b, 0, 0))
    const = lambda a: pl.BlockSpec(a.shape, lambda b, t: (0,) * a.ndim)
    conv = pl.BlockSpec((1, CONV_W - 1, D_FF), lambda b, t: (b, 0, 0))
    consts = [P["wg"], P["wu"], P["cw"], P["cb"], P["wd"]]
    resident = lambda a: pl.BlockSpec(a.shape, lambda b, t: (0,) * a.ndim, pipeline_mode=pl.Buffered(1))
    return pl.pallas_call(
        functools.partial(_ffn_kernel, tm=tm),
        grid=(nb, rows // tm),
        in_specs=[row3(D_MODEL), mod, mod, mod, const(gpre), const(gpost), conv] + [resident(a) for a in consts],
        out_specs=[row3(D_MODEL), conv],
        out_shape=[jax.ShapeDtypeStruct((nb, rows, D_MODEL), F32),
                   jax.ShapeDtypeStruct((nb, CONV_W - 1, D_FF), F32)],
        scratch_shapes=[pltpu.VMEM((tm + 8, D_FF), F32)],
        compiler_params=_params(2),
        name="ffn",
    )(x, sh, sc, gate, gpre, gpost, prev, *consts)


def _rot_half_cols(w):
    half = ROPE_DIM // 2
    return jnp.concatenate([-w[..., half:], w[..., :half]], axis=-1)


def _prep_ab(w_in_ab, b_f, q_a_g, kv_a_g, w_uq, w_ukv, w_out_ab):
    z = lambda n: jnp.zeros((D_MODEL, n), F32)
    c_q, c_kv, k_r = w_in_ab[:, 0:256], w_in_ab[:, 256:384], w_in_ab[:, 384:416]
    q_b, k_b, v_b, f_b = w_in_ab[:, 416:928], w_in_ab[:, 928:1440], w_in_ab[:, 1440:1952], w_in_ab[:, 1952:1960]
    w_ab = jnp.concatenate([c_q, c_kv, k_r, z(LANES - ROPE_DIM), _rot_half_cols(k_r), z(LANES - ROPE_DIM),
                            q_b, k_b, v_b, f_b, z(LANES - N_HEADS)], axis=1).astype(BF16)
    w_ft = jnp.concatenate([f_b.T, jnp.zeros((N_HEADS, D_MODEL), F32)], axis=0).astype(BF16)
    uq = w_uq.reshape(Q_RANK, N_HEADS, NOPE_DIM + ROPE_DIM)
    nope, rope_w = uq[..., :NOPE_DIM], uq[..., NOPE_DIM:]
    zq = lambda n: jnp.zeros((Q_RANK, N_HEADS, n), F32)
    wq_main = jnp.concatenate([nope, rope_w, zq(LANES - NOPE_DIM - ROPE_DIM)], axis=-1)
    wq_rot = jnp.concatenate([zq(NOPE_DIM), _rot_half_cols(rope_w), zq(LANES - NOPE_DIM - ROPE_DIM)], axis=-1)
    wq = jnp.concatenate([wq_main.reshape(Q_RANK, -1), wq_rot.reshape(Q_RANK, -1)], axis=1).astype(BF16)
    ukv = w_ukv.reshape(KV_RANK, N_HEADS, 2 * HEAD_DIM)
    wk = jnp.concatenate([ukv[..., :NOPE_DIM], jnp.zeros((KV_RANK, N_HEADS, LANES - NOPE_DIM), F32)], axis=-1)
    place = np.zeros((LANES, N_HEADS, LANES), np.float32)
    for r in range(ROPE_DIM):
        place[r, :, NOPE_DIM + r] = 1.0
    wkv = jnp.concatenate([
        jnp.concatenate([wk.reshape(KV_RANK, -1), ukv[..., NOPE_DIM:].reshape(KV_RANK, -1)], axis=1),
        jnp.concatenate([jnp.asarray(place.reshape(LANES, -1)), jnp.zeros((LANES, 512), F32)], axis=1),
    ], axis=0).astype(BF16)
    return {
        "w_ab": w_ab, "w_ft": w_ft, "wq": wq, "wkv": wkv,
        "bf128": jnp.concatenate([b_f, jnp.zeros((LANES - N_HEADS,), F32)]).reshape(1, LANES),
        "bft": b_f.reshape(N_HEADS, 1),
        "gq": q_a_g.reshape(1, Q_RANK), "gkv": kv_a_g.reshape(1, KV_RANK),
        "wo1": w_out_ab[:512].astype(BF16), "wo2": w_out_ab[512:].astype(BF16),
    }


def _prep_ffn(w_gate, w_up, conv_w, conv_b, w_down):
    chunk_cols = lambda w: w.reshape(D_MODEL, N_FF_CHUNKS, FF_CHUNK).transpose(1, 0, 2).astype(BF16)
    return {
        "wg": chunk_cols(w_gate), "wu": chunk_cols(w_up),
        "cw": conv_w.reshape(CONV_W, N_FF_CHUNKS, FF_CHUNK).transpose(1, 0, 2),
        "cb": conv_b.reshape(N_FF_CHUNKS, 1, FF_CHUNK),
        "wd": w_down.reshape(N_FF_CHUNKS, FF_CHUNK, D_MODEL).astype(BF16),
    }


def _rope_tables(pos):
    half = ROPE_DIM // 2
    inv = ROPE_BASE ** (-jnp.arange(half, dtype=F32) / half)
    ang = pos.astype(F32)[:, None] * inv[None, :]
    cos2 = jnp.concatenate([jnp.cos(ang)] * 2, axis=1)
    sin2 = jnp.concatenate([jnp.sin(ang)] * 2, axis=1)
    n = pos.shape[0]
    scale = (NOPE_DIM + ROPE_DIM) ** -0.5
    zeros = lambda w: jnp.zeros((n, w), F32)
    tqc = scale * jnp.concatenate([jnp.ones((n, NOPE_DIM), F32), cos2, zeros(LANES - NOPE_DIM - ROPE_DIM)], axis=1)
    tqs = scale * jnp.concatenate([zeros(NOPE_DIM), sin2, zeros(LANES - NOPE_DIM - ROPE_DIM)], axis=1)
    tkc = jnp.concatenate([cos2, zeros(LANES - ROPE_DIM)], axis=1)
    tks = jnp.concatenate([sin2, zeros(LANES - ROPE_DIM)], axis=1)
    return tqc, tqs, tkc, tks


def _pad_rows(a, front, total):
    return jnp.pad(a, ((0, 0), (front, total - front - a.shape[1])) + ((0, 0),) * (a.ndim - 2))


def _trunk(x, mods, past, PA, PF, w_cd, wo_cd, rel, gains, tm, tq):
    nb, rows, _ = x.shape
    q_off = 0 if past is None else PAST_LEN
    pos = q_off + jnp.arange(rows)
    tabs = _rope_tables(pos)
    mix_pre_g, mix_post_g, ffn_pre_g, ffn_post_g = gains
    out = {}

    def mod6(l):
        m = mods[l].reshape(nb, 6, 1, D_MODEL)
        return [m[:, j] for j in range(6)]

    sh_m, sc_m, g_m, sh_f, sc_f, g_f = mod6(0)
    (qcomb, ckv, krope, ckr, qb, kb16, vb16, kb32, vb32, logf, logf128, logft) = _in_ab(
        x, sh_m, sc_m, mix_pre_g[0:1], PA, tabs, tm)
    out["a_ckv"], out["a_krope"], out["b_k"], out["b_v"], out["b_logf"] = ckv, krope, kb32, vb32, logf
    if past is None:
        tkk = rows
        ckr_all, kb_all, vb_all, lf_all, lft_all = ckr, kb16, vb16, logf128, logft
    else:
        tkk = -(-(PAST_LEN + rows) // ATT_BLOCK) * ATT_BLOCK
        cat = lambda p, n: _pad_rows(jnp.concatenate([p, n], axis=1), 0, tkk)
        ckr_past = jnp.concatenate([past["a_ckv"].astype(BF16),
                                    jnp.pad(past["a_krope"].astype(BF16), ((0, 0), (0, 0), (0, LANES - ROPE_DIM)))],
                                   axis=-1)
        ckr_all = cat(ckr_past, ckr)
        kb_all = cat(past["b_k"].reshape(nb, PAST_LEN, 512).astype(BF16), kb16)
        vb_all = cat(past["b_v"].reshape(nb, PAST_LEN, 512).astype(BF16), vb16)
        lf_all = cat(jnp.pad(past["b_logf"], ((0, 0), (0, 0), (0, LANES - N_HEADS))), logf128)
        lft_all = jnp.pad(jnp.concatenate([jnp.swapaxes(past["b_logf"], 1, 2), logft], axis=2),
                          ((0, 0), (0, 0), (0, tkk - PAST_LEN - rows)))
    kcomb, va = _kvup(ckr_all, PA["wkv"], min(tm * 2, tkk) if past is None else ATT_SUB)
    cum, cumt = _cum(lf_all, lft_all)
    cq = cum[:, q_off:q_off + rows, :N_HEADS].reshape(nb, rows, N_PAIRS, 2).transpose(0, 2, 1, 3)
    nkb = tkk // ATT_BLOCK
    ckt = cumt.reshape(nb, N_PAIRS, 2, nkb, ATT_BLOCK).transpose(0, 1, 3, 2, 4)
    o_a = _flash("mla", qcomb, kcomb, va, tq=tq, q_off=q_off, n_valid=q_off + rows)
    o_b = _flash("fox", qb, kb_all, vb_all, cq, ckt, tq=tq, q_off=q_off, n_valid=q_off + rows)
    x = _out_proj(o_a, o_b, PA["wo1"], PA["wo2"], x, g_m, mix_post_g[0:1], tm)
    prev = jnp.zeros((nb, CONV_W - 1, D_FF), F32) if past is None else past["ffn_conv"][0]
    x, conv0 = _ffn(x, sh_f, sc_f, g_f, ffn_pre_g[0:1], ffn_post_g[0:1], prev, PF[0], tm)

    sh_m, sc_m, g_m, sh_f, sc_f, g_f = mod6(1)
    qc, kc16, vc16, qd, kd16, vd16, kc32, vc32, kd32, vd32 = _in_cd(x, sh_m, sc_m, mix_pre_g[1:2], w_cd, tm)
    out["d_k"], out["d_v"] = kd32, vd32
    if past is None:
        keep = min(BAND_CHUNKS * CHUNK, rows)
        out["c_k"], out["c_v"] = kc32[:, rows - keep:], vc32[:, rows - keep:]
        front = BAND_CHUNKS * CHUNK
        kc_all = _pad_rows(kc16, front, front + rows)
        vc_all = _pad_rows(vc16, front, front + rows)
        bias = _band_bias(rel, tq, front + tq, 0, -front)
        o_c = _band(qc, kc_all, vc_all, bias, tq=tq, pad_rows=front)
        kd_all, vd_all = kd16, vd16
    else:
        out["c_k"], out["c_v"] = kc32, vc32
        n_c = past["c_k"].shape[1]
        front = BAND_SAMPLE_WIN - n_c - rows
        kc_all = _pad_rows(jnp.concatenate([past["c_k"].reshape(nb, n_c, 512).astype(BF16), kc16], axis=1), front,
                           BAND_SAMPLE_WIN)
        vc_all = _pad_rows(jnp.concatenate([past["c_v"].reshape(nb, n_c, 512).astype(BF16), vc16], axis=1), front,
                           BAND_SAMPLE_WIN)
        bias = _band_bias(rel, rows, BAND_SAMPLE_WIN, q_off, PAST_LEN - n_c - front)
        o_c = _band(qc, kc_all, vc_all, bias, tq=rows, pad_rows=0)
        kd_all = _pad_rows(jnp.concatenate([past["d_k"].reshape(nb, PAST_LEN, 512).astype(BF16), kd16], axis=1), 0, tkk)
        vd_all = _pad_rows(jnp.concatenate([past["d_v"].reshape(nb, PAST_LEN, 512).astype(BF16), vd16], axis=1), 0, tkk)
    o_d = _sb(qd, kd_all, vd_all, tq=tq, q_off=q_off)
    x = _out_proj(o_c, o_d, wo_cd[0], wo_cd[1], x, g_m, mix_post_g[1:2], tm)
    prev = jnp.zeros((nb, CONV_W - 1, D_FF), F32) if past is None else past["ffn_conv"][1]
    x, conv1 = _ffn(x, sh_f, sc_f, g_f, ffn_pre_g[1:2], ffn_post_g[1:2], prev, PF[1], tm)
    out["ffn_conv"] = jnp.stack([conv0, conv1])
    return x, out


def kernel(x_prompt, x_sample, c_prompt, c_sample, cache_a_ckv, cache_a_krope, cache_b_k, cache_b_v, cache_b_logf, cache_c_k, cache_c_v, cache_d_k, cache_d_v, state_ffn_conv, ada_w, ada_b, mix_pre_g, mix_post_g, ffn_pre_g, ffn_post_g, w_in_ab, b_f, q_a_g, kv_a_g, w_uq, w_ukv, w_out_ab, w_in_cd, rel_bias_c, w_out_cd, ffn_w_gate, ffn_w_up, ffn_conv_w, ffn_conv_b, ffn_w_down):
    nbp, nbs = x_prompt.shape[0], x_sample.shape[0]
    c_all = jnp.concatenate([c_prompt, c_sample, jnp.zeros((32 - nbp - nbs, D_MODEL), F32)], axis=0)
    mods = _ada(c_all, ada_w, ada_b)
    PA = _prep_ab(w_in_ab[0], b_f[0], q_a_g[0], kv_a_g[0], w_uq[0], w_ukv[0], w_out_ab[0])
    PF = [_prep_ffn(ffn_w_gate[l], ffn_w_up[l], ffn_conv_w[l], ffn_conv_b[l], ffn_w_down[l]) for l in range(DEPTH)]
    w_cd = w_in_cd[0].astype(BF16)
    wo_cd = (w_out_cd[0][:512].astype(BF16), w_out_cd[0][512:].astype(BF16))
    gains = (mix_pre_g, mix_post_g, ffn_pre_g, ffn_post_g)
    past = {"a_ckv": cache_a_ckv[0], "a_krope": cache_a_krope[0], "b_k": cache_b_k[0], "b_v": cache_b_v[0],
            "b_logf": cache_b_logf[0], "c_k": cache_c_k[0], "c_v": cache_c_v[0], "d_k": cache_d_k[0],
            "d_v": cache_d_v[0], "ffn_conv": state_ffn_conv}
    y_p, sp = _trunk(x_prompt, mods[:, :nbp], None, PA, PF, w_cd, wo_cd, rel_bias_c[0], gains, 512, ATT_SUB)
    y_s, ss = _trunk(x_sample, mods[:, nbp:nbp + nbs], past, PA, PF, w_cd, wo_cd, rel_bias_c[0], gains,
                     x_sample.shape[1], x_sample.shape[1])

    def heads(a):
        return a.reshape(a.shape[0], a.shape[1], N_HEADS, HEAD_DIM)[None]

    return (y_p, y_s,
            sp["a_ckv"][None], ss["a_ckv"][None], sp["a_krope"][None], ss["a_krope"][None],
            heads(sp["b_k"]), heads(ss["b_k"]), heads(sp["b_v"]), heads(ss["b_v"]),
            sp["b_logf"][None], ss["b_logf"][None],
            heads(sp["c_k"]), heads(ss["c_k"]), heads(sp["c_v"]), heads(ss["c_v"]),
            heads(sp["d_k"]), heads(ss["d_k"]), heads(sp["d_v"]), heads(ss["d_v"]),
            sp["ffn_conv"], ss["ffn_conv"])
```

```python
import functools

import jax
import jax.numpy as jnp
import numpy as np
from jax import lax
from jax.experimental import pallas as pl
from jax.experimental.pallas import tpu as pltpu

D_MODEL = 1024
DEPTH = 2
PAST_LEN = 1024
CHUNK = 64
HEAD_DIM = 64
N_HEADS = 8
N_PAIRS = N_HEADS // 2
Q_RANK = 256
KV_RANK = 128
NOPE_DIM = 64
ROPE_DIM = 32
ROPE_BASE = 10000.0
BAND_CHUNKS = 8
REL_CLIP = 128
D_FF = 2816
CONV_W = 3
EPS = 1e-6
NEG_INF = -1e30
LOG2E = 1.4426950408889634
QK_SCALE_64 = HEAD_DIM ** -0.5 * LOG2E
QK_SCALE_A = (NOPE_DIM + ROPE_DIM) ** -0.5 * LOG2E

LANES = 128
FF_CHUNK = 256
N_FF_CHUNKS = D_FF // FF_CHUNK
ATT_BLOCK = 512
ATT_SUB = 256
BAND_SAMPLE_WIN = 640
VMEM_LIMIT_BYTES = 56 * 1024 * 1024

F32 = jnp.float32
BF16 = jnp.bfloat16

AB_CQ, AB_CKV, AB_KR, AB_KRROT, AB_QB, AB_KB, AB_VB, AB_FB, AB_COLS = 0, 256, 384, 512, 640, 1152, 1664, 2176, 2304


def _params(n_axes):
    return pltpu.CompilerParams(dimension_semantics=("arbitrary",) * n_axes, vmem_limit_bytes=VMEM_LIMIT_BYTES)


def _rms(x, g):
    return x * lax.rsqrt(jnp.mean(x * x, axis=-1, keepdims=True) + EPS) * g


def _log_sigmoid(x):
    return jnp.minimum(x, 0.0) - jnp.log1p(jnp.exp(-jnp.abs(x)))


def _log2_sigmoids(z2):
    l2 = jnp.log2(1.0 + jnp.exp2(-jnp.abs(z2)))
    return jnp.minimum(z2, 0.0) - l2, jnp.maximum(z2, 0.0) + l2


def _dot(a, b):
    return jnp.dot(a, b, preferred_element_type=F32)


def _dot_nt(a, b):
    return lax.dot_general(a, b, (((1,), (1,)), ((), ())), preferred_element_type=F32)


def _split_bf16(x, n):
    parts = []
    for _ in range(n):
        p = x.astype(BF16)
        parts.append(p)
        x = x - p.astype(F32)
    return parts


def _tile_lanes(x, n):
    return x if n == 1 else jnp.concatenate([x] * n, axis=1)


def _ada_kernel(c_ref, w_ref, b_ref, o_ref):
    c = c_ref[...]
    cond = c * jax.nn.sigmoid(c)
    o_ref[0] = _dot(cond.astype(BF16), w_ref[0].astype(BF16)) + b_ref[0]


def _ada(c_all, ada_w, ada_b):
    rows = c_all.shape[0]
    tn = 1536
    return pl.pallas_call(
        _ada_kernel,
        grid=(DEPTH, 6 * D_MODEL // tn),
        in_specs=[
            pl.BlockSpec((rows, D_MODEL), lambda l, j: (0, 0)),
            pl.BlockSpec((1, D_MODEL, tn), lambda l, j: (l, 0, j)),
            pl.BlockSpec((1, 1, tn), lambda l, j: (l, 0, j)),
        ],
        out_specs=pl.BlockSpec((1, rows, tn), lambda l, j: (l, 0, j)),
        out_shape=jax.ShapeDtypeStruct((DEPTH, rows, 6 * D_MODEL), F32),
        compiler_params=_params(2),
        name="ada",
    )(c_all, ada_w, ada_b.reshape(DEPTH, 1, 6 * D_MODEL))


def _store_vt(vt_ref, v):
    vt = v.T
    for p in range(N_PAIRS):
        vt_ref[0, p, 0] = vt[p * LANES:(p + 1) * LANES, :].astype(BF16)


def _in_ab_kernel(x_ref, sh_ref, sc_ref, gpre_ref, w_ref, wft_ref, bf_ref, bft_ref, gq_ref, gkv_ref, wq_ref,
                  tqc_ref, tqs_ref, tkc_ref, tks_ref,
                  qcomb_ref, ckv_ref, krope_ref, ckr_ref, qb_ref, kb16_ref, vb16_ref, kb32_ref, vb32_ref,
                  logf_ref, logf128_ref, logft_ref, vbt_ref=None):
    x = x_ref[0]
    h = _rms(x, gpre_ref[...]) * (1.0 + sc_ref[0]) + sh_ref[0]
    hb = h.astype(BF16)
    y = _dot(hb, w_ref[...])
    cq = _rms(y[:, AB_CQ:AB_CQ + Q_RANK], gq_ref[...]).astype(BF16)
    qa = _dot(cq, wq_ref[...])
    tqc = tqc_ref[...]
    tqs = tqs_ref[...]
    half = N_HEADS * LANES
    for hh in range(N_HEADS):
        lo = hh * LANES
        qcomb_ref[0, :, lo:lo + LANES] = (qa[:, lo:lo + LANES] * tqc + qa[:, half + lo:half + lo + LANES] * tqs).astype(BF16)
    ckv = _rms(y[:, AB_CKV:AB_CKV + KV_RANK], gkv_ref[...])
    ckv_ref[0] = ckv
    kr = y[:, AB_KR:AB_KR + LANES] * tkc_ref[...] + y[:, AB_KRROT:AB_KRROT + LANES] * tks_ref[...]
    krope_ref[0] = kr[:, :ROPE_DIM]
    ckr_ref[0, :, 0:LANES] = ckv.astype(BF16)
    ckr_ref[0, :, LANES:2 * LANES] = kr.astype(BF16)
    qb_ref[0] = (y[:, AB_QB:AB_QB + 512] * QK_SCALE_64).astype(BF16)
    kb = y[:, AB_KB:AB_KB + 512]
    vb = y[:, AB_VB:AB_VB + 512]
    kb32_ref[0] = kb
    vb32_ref[0] = vb
    kb16_ref[0] = kb.astype(BF16)
    vb16_ref[0] = vb.astype(BF16)
    if vbt_ref is not None:
        _store_vt(vbt_ref, vb)
    logf = _log_sigmoid(y[:, AB_FB:AB_FB + LANES] + bf_ref[...])
    logf128_ref[0] = logf
    logf_ref[0] = logf[:, :N_HEADS]
    ft = _dot_nt(wft_ref[...], hb)
    logft_ref[0] = _log_sigmoid(ft[:N_HEADS] + bft_ref[...])


def _vt_out(nb, rows, tm):
    return (jax.ShapeDtypeStruct((nb, N_PAIRS, rows // tm, LANES, tm), BF16),
            pl.BlockSpec((1, N_PAIRS, 1, LANES, tm), lambda b, t: (b, 0, t, 0, 0)))


def _in_ab(x, sh, sc, gpre, P, tabs, tm, with_vt):
    nb, rows, _ = x.shape
    nt = rows // tm
    row3 = lambda w: pl.BlockSpec((1, tm, w), lambda b, t: (b, t, 0))
    mod = pl.BlockSpec((1, 1, D_MODEL), lambda b, t: (b, 0, 0))
    const = lambda a: pl.BlockSpec(a.shape, lambda b, t: (0,) * a.ndim)
    tab = pl.BlockSpec((tm, LANES), lambda b, t: (t, 0))
    out_shapes = [
        jax.ShapeDtypeStruct((nb, rows, N_HEADS * LANES), BF16),
        jax.ShapeDtypeStruct((nb, rows, KV_RANK), F32),
        jax.ShapeDtypeStruct((nb, rows, ROPE_DIM), F32),
        jax.ShapeDtypeStruct((nb, rows, 2 * LANES), BF16),
        jax.ShapeDtypeStruct((nb, rows, 512), BF16),
        jax.ShapeDtypeStruct((nb, rows, 512), BF16),
        jax.ShapeDtypeStruct((nb, rows, 512), BF16),
        jax.ShapeDtypeStruct((nb, rows, 512), F32),
        jax.ShapeDtypeStruct((nb, rows, 512), F32),
        jax.ShapeDtypeStruct((nb, rows, N_HEADS), F32),
        jax.ShapeDtypeStruct((nb, rows, LANES), F32),
        jax.ShapeDtypeStruct((nb, N_HEADS, rows), F32),
    ]
    out_specs = [row3(N_HEADS * LANES), row3(KV_RANK), row3(ROPE_DIM), row3(2 * LANES), row3(512), row3(512),
                 row3(512), row3(512), row3(512), row3(N_HEADS), row3(LANES),
                 pl.BlockSpec((1, N_HEADS, tm), lambda b, t: (b, 0, t))]
    if with_vt:
        vt_shape, vt_spec = _vt_out(nb, rows, tm)
        out_shapes.append(vt_shape)
        out_specs.append(vt_spec)
    consts = [P["w_ab"], P["w_ft"], P["bf128"], P["bft"], P["gq"], P["gkv"], P["wq"]]
    return pl.pallas_call(
        _in_ab_kernel,
        grid=(nb, nt),
        in_specs=[row3(D_MODEL), mod, mod, const(gpre)] + [const(a) for a in consts] + [tab] * 4,
        out_specs=out_specs,
        out_shape=out_shapes,
        compiler_params=_params(2),
        name="in_ab",
    )(x, sh, sc, gpre, *consts, *tabs)


def _kvup_kernel(ckr_ref, w_ref, k_ref, v_ref):
    y = _dot(ckr_ref[0], w_ref[...])
    k_ref[0] = y[:, :N_HEADS * LANES].astype(BF16)
    v_ref[0] = y[:, N_HEADS * LANES:].astype(BF16)


def _kvup_t_kernel(ckr_ref, w_ref, k_ref, vt_ref):
    y = _dot(ckr_ref[0], w_ref[...])
    k_ref[0] = y[:, :N_HEADS * LANES].astype(BF16)
    _store_vt(vt_ref, y[:, N_HEADS * LANES:])


def _kvup(ckr, w, tm, with_vt):
    nb, rows, _ = ckr.shape
    if with_vt:
        v_shape, v_spec = _vt_out(nb, rows, tm)
    else:
        v_shape = jax.ShapeDtypeStruct((nb, rows, 512), BF16)
        v_spec = pl.BlockSpec((1, tm, 512), lambda b, t: (b, t, 0))
    return pl.pallas_call(
        _kvup_t_kernel if with_vt else _kvup_kernel,
        grid=(nb, rows // tm),
        in_specs=[pl.BlockSpec((1, tm, 2 * LANES), lambda b, t: (b, t, 0)),
                  pl.BlockSpec(w.shape, lambda b, t: (0, 0))],
        out_specs=[pl.BlockSpec((1, tm, N_HEADS * LANES), lambda b, t: (b, t, 0)), v_spec],
        out_shape=[jax.ShapeDtypeStruct((nb, rows, N_HEADS * LANES), BF16), v_shape],
        compiler_params=_params(2),
        name="kvup",
    )(ckr, w)


def _cum_kernel(x_ref, xt_ref, c_ref, ct_ref, rep_ref=None, *, n_chunks, tc):
    r = lax.broadcasted_iota(jnp.int32, (tc, tc), 0)
    c = lax.broadcasted_iota(jnp.int32, (tc, tc), 1)
    lower = jnp.where(c <= r, 1.0, 0.0).astype(BF16)
    upper = jnp.where(r <= c, 1.0, 0.0).astype(BF16)
    carry = jnp.zeros((1, LANES), F32)
    carry_t = jnp.zeros((N_HEADS, 1), F32)
    for ci in range(n_chunks):
        sl = slice(ci * tc, (ci + 1) * tc)
        cs = carry
        for p in _split_bf16(x_ref[0, sl, :], 3):
            cs = cs + _dot(lower, p)
        c_ref[0, sl, :] = cs * LOG2E
        if rep_ref is not None:
            for h in range(N_HEADS):
                rep_ref[0, h, sl, :] = jnp.broadcast_to(cs[:, h:h + 1] * LOG2E, (tc, LANES))
        carry = cs[tc - 1:tc, :]
        xt = jnp.concatenate([xt_ref[0, :, sl], jnp.zeros((N_HEADS, tc), F32)], axis=0)
        cst = jnp.zeros((2 * N_HEADS, tc), F32)
        for p in _split_bf16(xt, 3):
            cst = cst + _dot(p, upper)
        cst = cst[:N_HEADS] + carry_t
        ct_ref[0, :, sl] = cst * LOG2E
        carry_t = cst[:, tc - 1:tc]


def _cum(logf128, logft, with_rep):
    nb, tk, _ = logf128.shape
    tc = ATT_SUB
    out_specs = [pl.BlockSpec((1, tk, LANES), lambda b: (b, 0, 0)),
                 pl.BlockSpec((1, N_HEADS, tk), lambda b: (b, 0, 0))]
    out_shape = [jax.ShapeDtypeStruct((nb, tk, LANES), F32), jax.ShapeDtypeStruct((nb, N_HEADS, tk), F32)]
    if with_rep:
        out_specs.append(pl.BlockSpec((1, N_HEADS, tk, LANES), lambda b: (b, 0, 0, 0)))
        out_shape.append(jax.ShapeDtypeStruct((nb, N_HEADS, tk, LANES), F32))
    return pl.pallas_call(
        functools.partial(_cum_kernel, n_chunks=tk // tc, tc=tc),
        grid=(nb,),
        in_specs=[pl.BlockSpec((1, tk, LANES), lambda b: (b, 0, 0)),
                  pl.BlockSpec((1, N_HEADS, tk), lambda b: (b, 0, 0))],
        out_specs=out_specs,
        out_shape=out_shape,
        compiler_params=_params(1),
        name="cum",
    )(logf128, logft)


def _flash_kernel(*refs, mode, tq, tk, q_off, n_valid, tail_widths):
    if mode == "fox":
        q_ref, k_ref, v_ref, cq_ref, ckt_ref, o_ref, m_s, l_s, acc_s = refs
    else:
        q_ref, k_ref, v_ref, o_ref, m_s, l_s, acc_s = refs
    i = pl.program_id(2)
    qpos0 = q_off + i * tq
    n_full = qpos0 // tk
    rem = qpos0 + tq - n_full * tk
    lane = lax.broadcasted_iota(jnp.int32, (tq, LANES), 1)

    qs, kcols, cq_tiles = [], [], []
    for h in range(2):
        if mode == "fox":
            q = q_ref[0]
            qs.append(jnp.where((lane < HEAD_DIM) if h == 0 else (lane >= HEAD_DIM), q, jnp.zeros_like(q)))
            kcols.append(slice(0, LANES))
            cq_tiles.append(jnp.broadcast_to(cq_ref[0, 0][:, h:h + 1], (tq, LANES)))
        else:
            qs.append(q_ref[0, :, h * LANES:(h + 1) * LANES])
            kcols.append(slice(h * LANES, (h + 1) * LANES))
        m_s[h] = jnp.full((tq, LANES), NEG_INF, F32)
        l_s[h] = jnp.zeros((tq, LANES), F32)
        acc_s[h] = jnp.zeros((tq, LANES), F32)

    def step(j, width, masked):
        nrep = width // LANES
        r0 = pl.multiple_of(j * tk, tk)
        v = v_ref[0, pl.ds(r0, width), :]
        if masked:
            kpos = n_full * tk + lax.broadcasted_iota(jnp.int32, (tq, width), 1)
            qpos = qpos0 + lax.broadcasted_iota(jnp.int32, (tq, width), 0)
            if mode == "fox":
                mask = kpos <= qpos
            else:
                mask = (lax.shift_right_logical(kpos, 6) <= lax.shift_right_logical(qpos, 6)) & (kpos < n_valid)
        for h in range(2):
            s = _dot_nt(qs[h], k_ref[0, pl.ds(r0, width), kcols[h]])
            if mode == "fox":
                s = s + (_tile_lanes(cq_tiles[h], nrep) - ckt_ref[0, 0, j][h:h + 1, :width])
            if masked:
                s = jnp.where(mask, s, NEG_INF)
            m_prev = m_s[h]
            m_next = jnp.maximum(m_prev, jnp.max(s, axis=1, keepdims=True))
            p = jnp.exp2(s - _tile_lanes(m_next, nrep))
            alpha = jnp.exp2(m_prev - m_next)
            l_s[h] = alpha * l_s[h] + jnp.sum(p, axis=1, keepdims=True)
            acc_s[h] = alpha * acc_s[h] + _dot(p.astype(BF16), v)
            m_s[h] = m_next

    def body(j, carry):
        step(j, tk, False)
        return carry

    lax.fori_loop(0, n_full, body, 0)
    if len(tail_widths) == 1:
        step(n_full, tail_widths[0], True)
    else:
        lo_w, hi_w = tail_widths
        pl.when(rem <= lo_w)(lambda: step(n_full, lo_w, True))
        pl.when(rem > lo_w)(lambda: step(n_full, hi_w, True))

    o0 = acc_s[0] / l_s[0]
    o1 = acc_s[1] / l_s[1]
    o_ref[0] = jnp.where(lane < HEAD_DIM, o0, o1).astype(BF16)


def _tail_widths(tk):
    return (ATT_SUB, tk) if tk > ATT_SUB else (tk,)


def _flash(mode, q, k, v, cq=None, ckt=None, *, tq, q_off, n_valid):
    nb, rows, _ = q.shape
    tkk = k.shape[1]
    tk = ATT_BLOCK
    kw = 2 * LANES if mode == "mla" else LANES
    in_specs = [pl.BlockSpec((1, tq, kw), lambda b, p, i: (b, i, p)),
                pl.BlockSpec((1, tkk, kw), lambda b, p, i: (b, 0, p)),
                pl.BlockSpec((1, tkk, LANES), lambda b, p, i: (b, 0, p))]
    args = [q, k, v]
    if mode == "fox":
        in_specs += [pl.BlockSpec((1, 1, tq, 2), lambda b, p, i: (b, p, i, 0)),
                     pl.BlockSpec((1, 1, tkk // tk, 2, tk), lambda b, p, i: (b, p, 0, 0, 0))]
        args += [cq, ckt]
    return pl.pallas_call(
        functools.partial(_flash_kernel, mode=mode, tq=tq, tk=tk, q_off=q_off, n_valid=n_valid,
                          tail_widths=_tail_widths(tk)),
        grid=(nb, N_PAIRS, rows // tq),
        in_specs=in_specs,
        out_specs=pl.BlockSpec((1, tq, LANES), lambda b, p, i: (b, i, p)),
        out_shape=jax.ShapeDtypeStruct((nb, rows, 512), BF16),
        scratch_shapes=[pltpu.VMEM((2, tq, LANES), F32)] * 3,
        compiler_params=_params(3),
        name="attn_" + mode,
    )(*args)


def _sb_kernel(q_ref, k_ref, v_ref, o_ref, acc_s, r_s, u_s, *, tq, tk, q_off, tail_widths):
    i = pl.program_id(2)
    qpos0 = q_off + i * tq
    n_full = qpos0 // tk
    rem = qpos0 + tq - n_full * tk
    nrep = ATT_SUB // LANES
    lane = lax.broadcasted_iota(jnp.int32, (tq, LANES), 1)
    ur = lax.broadcasted_iota(jnp.int32, (ATT_SUB, ATT_SUB), 0)
    uc = lax.broadcasted_iota(jnp.int32, (ATT_SUB, ATT_SUB), 1)
    u_s[...] = jnp.where(ur > uc, 1.0, 0.0).astype(BF16)
    q = q_ref[0]
    qs = [jnp.where((lane < HEAD_DIM) if h == 0 else (lane >= HEAD_DIM), q, jnp.zeros_like(q)) for h in range(2)]
    acc_s[...] = jnp.zeros((2, tq, LANES), F32)
    r_s[...] = jnp.zeros((2, tq, LANES), F32)

    def step(j, width, masked):
        r0 = pl.multiple_of(j * tk, tk)
        k = k_ref[0, pl.ds(r0, width), :]
        v = v_ref[0, pl.ds(r0, width), :]
        if masked:
            kpos = n_full * tk + lax.broadcasted_iota(jnp.int32, (tq, width), 1)
            qpos = qpos0 + lax.broadcasted_iota(jnp.int32, (tq, width), 0)
            mask = kpos < qpos
        u = u_s[...]
        for h in range(2):
            z = _dot_nt(qs[h], k)
            lsz, nlk = _log2_sigmoids(z)
            if masked:
                nlk = jnp.where(mask, nlk, 0.0)
            r = r_s[h]
            ws = []
            for sb in reversed(range(width // ATT_SUB)):
                cols = slice(sb * ATT_SUB, (sb + 1) * ATT_SUB)
                nlk_sb = nlk[:, cols]
                hi, lo = _split_bf16(nlk_sb, 2)
                after = _dot(hi, u) + _dot(lo, u) + _tile_lanes(r, nrep)
                w = jnp.exp2(lsz[:, cols] - after)
                if masked:
                    w = jnp.where(mask[:, cols], w, 0.0)
                ws.append(w.astype(BF16))
                r = r + jnp.sum(nlk_sb, axis=1, keepdims=True)
            w_all = ws[0] if len(ws) == 1 else jnp.concatenate(ws[::-1], axis=1)
            acc_s[h] = acc_s[h] + _dot(w_all, v)
            r_s[h] = r

    if len(tail_widths) == 1:
        step(n_full, tail_widths[0], True)
    else:
        lo_w, hi_w = tail_widths
        pl.when(rem <= lo_w)(lambda: step(n_full, lo_w, True))
        pl.when(rem > lo_w)(lambda: step(n_full, hi_w, True))

    def body(jj, carry):
        step(n_full - 1 - jj, tk, False)
        return carry

    lax.fori_loop(0, n_full, body, 0)
    o_ref[0] = jnp.where(lane < HEAD_DIM, acc_s[0], acc_s[1]).astype(BF16)


def _sb(q, k, v, *, tq, q_off):
    nb, rows, _ = q.shape
    tkk = k.shape[1]
    tk = ATT_BLOCK
    return pl.pallas_call(
        functools.partial(_sb_kernel, tq=tq, tk=tk, q_off=q_off, tail_widths=_tail_widths(tk)),
        grid=(nb, N_PAIRS, rows // tq),
        in_specs=[pl.BlockSpec((1, tq, LANES), lambda b, p, i: (b, i, p)),
                  pl.BlockSpec((1, tkk, LANES), lambda b, p, i: (b, 0, p)),
                  pl.BlockSpec((1, tkk, LANES), lambda b, p, i: (b, 0, p))],
        out_specs=pl.BlockSpec((1, tq, LANES), lambda b, p, i: (b, i, p)),
        out_shape=jax.ShapeDtypeStruct((nb, rows, 512), BF16),
        scratch_shapes=[pltpu.VMEM((2, tq, LANES), F32), pltpu.VMEM((2, tq, LANES), F32),
                        pltpu.VMEM((ATT_SUB, ATT_SUB), BF16)],
        compiler_params=_params(3),
        name="attn_sb",
    )(q, k, v)


def _pair_queries(q_ref, mode, tq):
    lane = lax.broadcasted_iota(jnp.int32, (tq, LANES), 1)
    if mode == "mla":
        return [q_ref[0, :, h * LANES:(h + 1) * LANES] for h in range(2)], [slice(h * LANES, (h + 1) * LANES) for h in range(2)]
    q = q_ref[0]
    qs = [jnp.where((lane < HEAD_DIM) if h == 0 else (lane >= HEAD_DIM), q, jnp.zeros_like(q)) for h in range(2)]
    return qs, [slice(0, LANES)] * 2


def _store_heads_t(o_ref, o0, o1):
    o_ref[0] = jnp.concatenate([o0, o1], axis=0).T.astype(BF16)


def _flash_t_kernel(*refs, mode, tq, tk, tail_widths):
    if mode == "fox":
        q_ref, k_ref, vt_ref, cq_ref, ck_ref, o_ref, m_s, l_s, acc_s = refs
    else:
        q_ref, k_ref, vt_ref, o_ref, m_s, l_s, acc_s = refs
    i = pl.program_id(2)
    qpos0 = i * tq
    n_full = qpos0 // tk
    rem = qpos0 + tq - n_full * tk
    qs, kcols = _pair_queries(q_ref, mode, tq)
    m_s[...] = jnp.full((2, 1, tq), NEG_INF, F32)
    l_s[...] = jnp.zeros((2, 1, tq), F32)
    acc_s[...] = jnp.zeros((2, HEAD_DIM, tq), F32)

    def step(j, width, masked):
        r0 = pl.multiple_of(j * tk, tk)
        if masked:
            kpos = n_full * tk + lax.broadcasted_iota(jnp.int32, (width, tq), 0)
            qpos = qpos0 + lax.broadcasted_iota(jnp.int32, (width, tq), 1)
            if mode == "fox":
                mask = kpos <= qpos
            else:
                mask = lax.shift_right_logical(kpos, 6) <= lax.shift_right_logical(qpos, 6)
        for h in range(2):
            s = _dot_nt(k_ref[0, pl.ds(r0, width), kcols[h]], qs[h])
            if mode == "fox":
                s = s + (cq_ref[0, 0, h:h + 1, :] - _tile_lanes(ck_ref[0, h, pl.ds(r0, width), :], tq // LANES))
            if masked:
                s = jnp.where(mask, s, NEG_INF)
            m_prev = m_s[h]
            m_next = jnp.maximum(m_prev, jnp.max(s, axis=0, keepdims=True))
            p = jnp.exp2(s - m_next)
            alpha = jnp.exp2(m_prev - m_next)
            l_s[h] = alpha * l_s[h] + jnp.sum(p, axis=0, keepdims=True)
            vt = vt_ref[0, 0, j, h * HEAD_DIM:(h + 1) * HEAD_DIM, :width]
            acc_s[h] = alpha * acc_s[h] + _dot(vt, p.astype(BF16))
            m_s[h] = m_next

    def body(j, carry):
        step(j, tk, False)
        return carry

    lax.fori_loop(0, n_full, body, 0)
    lo_w, hi_w = tail_widths
    pl.when(rem <= lo_w)(lambda: step(n_full, lo_w, True))
    pl.when(rem > lo_w)(lambda: step(n_full, hi_w, True))
    _store_heads_t(o_ref, acc_s[0] / l_s[0], acc_s[1] / l_s[1])


def _flash_t(mode, q, k, vt, cq=None, ck=None, *, tq):
    nb, rows, _ = q.shape
    tk = ATT_BLOCK
    nkb = rows // tk
    kw = 2 * LANES if mode == "mla" else LANES
    in_specs = [pl.BlockSpec((1, tq, kw), lambda b, p, i: (b, i, p)),
                pl.BlockSpec((1, rows, kw), lambda b, p, i: (b, 0, p)),
                pl.BlockSpec((1, 1, nkb, LANES, tk), lambda b, p, i: (b, p, 0, 0, 0))]
    args = [q, k, vt]
    if mode == "fox":
        in_specs += [pl.BlockSpec((1, 1, 2, tq), lambda b, p, i: (b, p, 0, i)),
                     pl.BlockSpec((1, 2, rows, LANES), lambda b, p, i: (b, p, 0, 0))]
        args += [cq, ck]
    return pl.pallas_call(
        functools.partial(_flash_t_kernel, mode=mode, tq=tq, tk=tk, tail_widths=_tail_widths(tk)),
        grid=(nb, N_PAIRS, rows // tq),
        in_specs=in_specs,
        out_specs=pl.BlockSpec((1, tq, LANES), lambda b, p, i: (b, i, p)),
        out_shape=jax.ShapeDtypeStruct((nb, rows, 512), BF16),
        scratch_shapes=[pltpu.VMEM((2, 1, tq), F32), pltpu.VMEM((2, 1, tq), F32), pltpu.VMEM((2, HEAD_DIM, tq), F32)],
        compiler_params=_params(3),
        name="attn_" + mode,
    )(*args)


def _sb_t_kernel(q_ref, k_ref, vt_ref, o_ref, acc_s, r_s, ut_s, *, tq, tk, tail_widths):
    i = pl.program_id(2)
    qpos0 = i * tq
    n_full = qpos0 // tk
    rem = qpos0 + tq - n_full * tk
    ur = lax.broadcasted_iota(jnp.int32, (ATT_SUB, ATT_SUB), 0)
    uc = lax.broadcasted_iota(jnp.int32, (ATT_SUB, ATT_SUB), 1)
    ut_s[...] = jnp.where(uc > ur, 1.0, 0.0).astype(BF16)
    qs, _ = _pair_queries(q_ref, "sb", tq)
    acc_s[...] = jnp.zeros((2, HEAD_DIM, tq), F32)
    r_s[...] = jnp.zeros((2, 1, tq), F32)

    def step(j, width, masked):
        r0 = pl.multiple_of(j * tk, tk)
        k = k_ref[0, pl.ds(r0, width), :]
        if masked:
            kpos = n_full * tk + lax.broadcasted_iota(jnp.int32, (width, tq), 0)
            qpos = qpos0 + lax.broadcasted_iota(jnp.int32, (width, tq), 1)
            mask = kpos < qpos
        ut = ut_s[...]
        for h in range(2):
            lsz, nlk = _log2_sigmoids(_dot_nt(k, qs[h]))
            if masked:
                nlk = jnp.where(mask, nlk, 0.0)
            r = r_s[h]
            ws = [None] * (width // ATT_SUB)
            for sb in reversed(range(width // ATT_SUB)):
                rows = slice(sb * ATT_SUB, (sb + 1) * ATT_SUB)
                nlk_sb = nlk[rows]
                hi, lo = _split_bf16(nlk_sb, 2)
                w = jnp.exp2(lsz[rows] - (_dot(ut, hi) + _dot(ut, lo)) - r)
                if masked:
                    w = jnp.where(mask[rows], w, 0.0)
                ws[sb] = w.astype(BF16)
                r = r + jnp.sum(nlk_sb, axis=0, keepdims=True)
            w_all = ws[0] if len(ws) == 1 else jnp.concatenate(ws, axis=0)
            vt = vt_ref[0, 0, j, h * HEAD_DIM:(h + 1) * HEAD_DIM, :width]
            acc_s[h] = acc_s[h] + _dot(vt, w_all)
            r_s[h] = r

    lo_w, hi_w = tail_widths
    pl.when(rem <= lo_w)(lambda: step(n_full, lo_w, True))
    pl.when(rem > lo_w)(lambda: step(n_full, hi_w, True))

    def body(jj, carry):
        step(n_full - 1 - jj, tk, False)
        return carry

    lax.fori_loop(0, n_full, body, 0)
    _store_heads_t(o_ref, acc_s[0], acc_s[1])


def _sb_t(q, k, vt, *, tq):
    nb, rows, _ = q.shape
    tk = ATT_BLOCK
    return pl.pallas_call(
        functools.partial(_sb_t_kernel, tq=tq, tk=tk, tail_widths=_tail_widths(tk)),
        grid=(nb, N_PAIRS, rows // tq),
        in_specs=[pl.BlockSpec((1, tq, LANES), lambda b, p, i: (b, i, p)),
                  pl.BlockSpec((1, rows, LANES), lambda b, p, i: (b, 0, p)),
                  pl.BlockSpec((1, 1, rows // tk, LANES, tk), lambda b, p, i: (b, p, 0, 0, 0))],
        out_specs=pl.BlockSpec((1, tq, LANES), lambda b, p, i: (b, i, p)),
        out_shape=jax.ShapeDtypeStruct((nb, rows, 512), BF16),
        scratch_shapes=[pltpu.VMEM((2, HEAD_DIM, tq), F32), pltpu.VMEM((2, 1, tq), F32),
                        pltpu.VMEM((ATT_SUB, ATT_SUB), BF16)],
        compiler_params=_params(3),
        name="attn_sb",
    )(q, k, vt)


def _bias_kernel(tab_ref, o_ref, *, nq, nk, q0, k0):
    h = pl.program_id(0)
    qpos = q0 + lax.broadcasted_iota(jnp.int32, (nq, nk), 0)
    kpos = k0 + lax.broadcasted_iota(jnp.int32, (nq, nk), 1)
    idx = jnp.clip(qpos - kpos, -REL_CLIP, REL_CLIP) + REL_CLIP
    qc = lax.shift_right_arithmetic(qpos, 6)
    kc = lax.shift_right_arithmetic(kpos, 6)
    band = (kc <= qc) & (kc >= qc - BAND_CHUNKS)

    def body(r, acc):
        return jnp.where(idx == r, tab_ref[r * N_HEADS + h], acc)

    bias = lax.fori_loop(0, 2 * REL_CLIP + 1, body, jnp.zeros((nq, nk), F32))
    o_ref[0] = jnp.where(band, bias * LOG2E, NEG_INF)


def _band_bias(rel_table, nq, nk, q0, k0):
    return pl.pallas_call(
        functools.partial(_bias_kernel, nq=nq, nk=nk, q0=q0, k0=k0),
        grid=(N_HEADS,),
        in_specs=[pl.BlockSpec(memory_space=pltpu.SMEM)],
        out_specs=pl.BlockSpec((1, nq, nk), lambda h: (h, 0, 0)),
        out_shape=jax.ShapeDtypeStruct((N_HEADS, nq, nk), F32),
        compiler_params=_params(1),
        name="band_bias",
    )(rel_table.reshape(-1))


def _band_kernel(q_ref, k_ref, v_ref, bias_ref, o_ref, *, tq, win, pad_rows):
    i = pl.program_id(2)
    start = pl.multiple_of(i * tq, tq)
    k = k_ref[0, pl.ds(start, win), :]
    v = v_ref[0, pl.ds(start, win), :]
    lane = lax.broadcasted_iota(jnp.int32, (tq, LANES), 1)
    valid = lax.broadcasted_iota(jnp.int32, (tq, win), 1) >= pad_rows - i * tq
    q = q_ref[0]
    outs = []
    for h in range(2):
        qh = jnp.where((lane < HEAD_DIM) if h == 0 else (lane >= HEAD_DIM), q, jnp.zeros_like(q))
        s = jnp.where(valid, _dot_nt(qh, k) + bias_ref[h], NEG_INF)
        p = jnp.exp2(s - jnp.max(s, axis=1, keepdims=True))
        outs.append(_dot(p.astype(BF16), v) / jnp.sum(p, axis=1, keepdims=True))
    o_ref[0] = jnp.where(lane < HEAD_DIM, outs[0], outs[1]).astype(BF16)


def _band(q, k_pad, v_pad, bias, *, tq, pad_rows):
    nb, rows, _ = q.shape
    tkk = k_pad.shape[1]
    win = bias.shape[2]
    return pl.pallas_call(
        functools.partial(_band_kernel, tq=tq, win=win, pad_rows=pad_rows),
        grid=(nb, N_PAIRS, rows // tq),
        in_specs=[pl.BlockSpec((1, tq, LANES), lambda b, p, i: (b, i, p)),
                  pl.BlockSpec((1, tkk, LANES), lambda b, p, i: (b, 0, p)),
                  pl.BlockSpec((1, tkk, LANES), lambda b, p, i: (b, 0, p)),
                  pl.BlockSpec((2, tq, win), lambda b, p, i: (p, 0, 0))],
        out_specs=pl.BlockSpec((1, tq, LANES), lambda b, p, i: (b, i, p)),
        out_shape=jax.ShapeDtypeStruct((nb, rows, 512), BF16),
        compiler_params=_params(3),
        name="attn_band",
    )(q, k_pad, v_pad, bias)


def _in_cd_kernel(x_ref, sh_ref, sc_ref, gpre_ref, w_ref,
                  qc_ref, kc16_ref, vc16_ref, qd_ref, kd16_ref, vd16_ref, kc32_ref, vc32_ref, kd32_ref, vd32_ref,
                  vdt_ref=None):
    x = x_ref[0]
    h = _rms(x, gpre_ref[...]) * (1.0 + sc_ref[0]) + sh_ref[0]
    y = _dot(h.astype(BF16), w_ref[...])
    qc_ref[0] = (y[:, 0:512] * QK_SCALE_64).astype(BF16)
    qd_ref[0] = (y[:, 1536:2048] * QK_SCALE_64).astype(BF16)
    for lo, r32, r16 in ((512, kc32_ref, kc16_ref), (1024, vc32_ref, vc16_ref),
                         (2048, kd32_ref, kd16_ref), (2560, vd32_ref, vd16_ref)):
        part = y[:, lo:lo + 512]
        r32[0] = part
        r16[0] = part.astype(BF16)
    if vdt_ref is not None:
        _store_vt(vdt_ref, y[:, 2560:3072])


def _in_cd(x, sh, sc, gpre, w, tm, with_vt):
    nb, rows, _ = x.shape
    row3 = lambda wd: pl.BlockSpec((1, tm, wd), lambda b, t: (b, t, 0))
    mod = pl.BlockSpec((1, 1, D_MODEL), lambda b, t: (b, 0, 0))
    const = lambda a: pl.BlockSpec(a.shape, lambda b, t: (0,) * a.ndim)
    out_specs = [row3(512)] * 10
    out_shape = [jax.ShapeDtypeStruct((nb, rows, 512), BF16)] * 6 + [jax.ShapeDtypeStruct((nb, rows, 512), F32)] * 4
    if with_vt:
        vt_shape, vt_spec = _vt_out(nb, rows, tm)
        out_shape.append(vt_shape)
        out_specs.append(vt_spec)
    return pl.pallas_call(
        _in_cd_kernel,
        grid=(nb, rows // tm),
        in_specs=[row3(D_MODEL), mod, mod, const(gpre), const(w)],
        out_specs=out_specs,
        out_shape=out_shape,
        compiler_params=_params(2),
        name="in_cd",
    )(x, sh, sc, gpre, w)


def _out_kernel(o1_ref, o2_ref, w1_ref, w2_ref, x_ref, gate_ref, gpost_ref, xo_ref):
    o = _dot(o1_ref[0], w1_ref[...]) + _dot(o2_ref[0], w2_ref[...])
    xo_ref[0] = x_ref[0] + gate_ref[0] * _rms(o, gpost_ref[...])


def _out_proj(o1, o2, w1, w2, x, gate, gpost, tm):
    nb, rows, _ = x.shape
    row3 = lambda wd: pl.BlockSpec((1, tm, wd), lambda b, t: (b, t, 0))
    mod = pl.BlockSpec((1, 1, D_MODEL), lambda b, t: (b, 0, 0))
    const = lambda a: pl.BlockSpec(a.shape, lambda b, t: (0,) * a.ndim)
    return pl.pallas_call(
        _out_kernel,
        grid=(nb, rows // tm),
        in_specs=[row3(512), row3(512), const(w1), const(w2), row3(D_MODEL), mod, const(gpost)],
        out_specs=row3(D_MODEL),
        out_shape=jax.ShapeDtypeStruct((nb, rows, D_MODEL), F32),
        compiler_params=_params(2),
        name="out_proj",
    )(o1, o2, w1, w2, x, gate, gpost)


def _ffn_kernel(x_ref, sh_ref, sc_ref, gate_ref, gpre_ref, gpost_ref, prev_ref, wg_ref, wu_ref, cw_ref, cb_ref, wd_ref,
                xo_ref, conv_ref, gbuf_ref, *, tm):
    t = pl.program_id(1)
    x = x_ref[0]
    h = (_rms(x, gpre_ref[...]) * (1.0 + sc_ref[0]) + sh_ref[0]).astype(BF16)

    @pl.when(t == 0)
    def _():
        gbuf_ref[0:8, :] = jnp.zeros((8, D_FF), F32)
        gbuf_ref[8 - (CONV_W - 1):8, :] = prev_ref[0]

    acc = jnp.zeros((tm, D_MODEL), F32)
    for c in range(N_FF_CHUNKS):
        cols = slice(c * FF_CHUNK, (c + 1) * FF_CHUNK)
        g = _dot(h, wg_ref[c])
        u = _dot(h, wu_ref[c])
        gbuf_ref[8:8 + tm, cols] = g
        g1 = gbuf_ref[7:7 + tm, cols]
        g2 = gbuf_ref[6:6 + tm, cols]
        cw = cw_ref[c]
        gc = cb_ref[c] + (cw[0:1] * g2 + cw[1:2] * g1 + cw[2:3] * g)
        a = (gc * jax.nn.sigmoid(gc)) * u
        acc = acc + _dot(a.astype(BF16), wd_ref[c])
    conv_ref[0] = gbuf_ref[tm + 6:tm + 8, :]
    gbuf_ref[0:8, :] = gbuf_ref[tm:tm + 8, :]
    xo_ref[0] = x + gate_ref[0] * _rms(acc, gpost_ref[...])


def _ffn(x, sh, sc, gate, gpre, gpost, prev, P, tm):
    nb, rows, _ = x.shape
    row3 = lambda wd: pl.BlockSpec((1, tm, wd), lambda b, t: (b, t, 0))
    mod = pl.BlockSpec((1, 1, D_MODEL), lambda b, t: (b, 0, 0))
    const = lambda a: pl.BlockSpec(a.shape, lambda b, t: (0,) * a.ndim)
    conv = pl.BlockSpec((1, CONV_W - 1, D_FF), lambda b, t: (b, 0, 0))
    consts = [P["wg"], P["wu"], P["cw"], P["cb"], P["wd"]]
    resident = lambda a: pl.BlockSpec(a.shape, lambda b, t: (0,) * a.ndim, pipeline_mode=pl.Buffered(1))
    return pl.pallas_call(
        functools.partial(_ffn_kernel, tm=tm),
        grid=(nb, rows // tm),
        in_specs=[row3(D_MODEL), mod, mod, mod, const(gpre), const(gpost), conv] + [resident(a) for a in consts],
        out_specs=[row3(D_MODEL), conv],
        out_shape=[jax.ShapeDtypeStruct((nb, rows, D_MODEL), F32),
                   jax.ShapeDtypeStruct((nb, CONV_W - 1, D_FF), F32)],
        scratch_shapes=[pltpu.VMEM((tm + 8, D_FF), F32)],
        compiler_params=_params(2),
        name="ffn",
    )(x, sh, sc, gate, gpre, gpost, prev, *consts)


def _rot_half_cols(w):
    half = ROPE_DIM // 2
    return jnp.concatenate([-w[..., half:], w[..., :half]], axis=-1)


def _prep_ab(w_in_ab, b_f, q_a_g, kv_a_g, w_uq, w_ukv, w_out_ab):
    z = lambda n: jnp.zeros((D_MODEL, n), F32)
    c_q, c_kv, k_r = w_in_ab[:, 0:256], w_in_ab[:, 256:384], w_in_ab[:, 384:416]
    q_b, k_b, v_b, f_b = w_in_ab[:, 416:928], w_in_ab[:, 928:1440], w_in_ab[:, 1440:1952], w_in_ab[:, 1952:1960]
    w_ab = jnp.concatenate([c_q, c_kv, k_r, z(LANES - ROPE_DIM), _rot_half_cols(k_r), z(LANES - ROPE_DIM),
                            q_b, k_b, v_b, f_b, z(LANES - N_HEADS)], axis=1).astype(BF16)
    w_ft = jnp.concatenate([f_b.T, jnp.zeros((N_HEADS, D_MODEL), F32)], axis=0).astype(BF16)
    uq = w_uq.reshape(Q_RANK, N_HEADS, NOPE_DIM + ROPE_DIM)
    nope, rope_w = uq[..., :NOPE_DIM], uq[..., NOPE_DIM:]
    zq = lambda n: jnp.zeros((Q_RANK, N_HEADS, n), F32)
    wq_main = jnp.concatenate([nope, rope_w, zq(LANES - NOPE_DIM - ROPE_DIM)], axis=-1)
    wq_rot = jnp.concatenate([zq(NOPE_DIM), _rot_half_cols(rope_w), zq(LANES - NOPE_DIM - ROPE_DIM)], axis=-1)
    wq = jnp.concatenate([wq_main.reshape(Q_RANK, -1), wq_rot.reshape(Q_RANK, -1)], axis=1).astype(BF16)
    ukv = w_ukv.reshape(KV_RANK, N_HEADS, 2 * HEAD_DIM)
    wk = jnp.concatenate([ukv[..., :NOPE_DIM], jnp.zeros((KV_RANK, N_HEADS, LANES - NOPE_DIM), F32)], axis=-1)
    place = np.zeros((LANES, N_HEADS, LANES), np.float32)
    for r in range(ROPE_DIM):
        place[r, :, NOPE_DIM + r] = 1.0
    wkv = jnp.concatenate([
        jnp.concatenate([wk.reshape(KV_RANK, -1), ukv[..., NOPE_DIM:].reshape(KV_RANK, -1)], axis=1),
        jnp.concatenate([jnp.asarray(place.reshape(LANES, -1)), jnp.zeros((LANES, 512), F32)], axis=1),
    ], axis=0).astype(BF16)
    return {
        "w_ab": w_ab, "w_ft": w_ft, "wq": wq, "wkv": wkv,
        "bf128": jnp.concatenate([b_f, jnp.zeros((LANES - N_HEADS,), F32)]).reshape(1, LANES),
        "bft": b_f.reshape(N_HEADS, 1),
        "gq": q_a_g.reshape(1, Q_RANK), "gkv": kv_a_g.reshape(1, KV_RANK),
        "wo1": w_out_ab[:512].astype(BF16), "wo2": w_out_ab[512:].astype(BF16),
    }


def _prep_ffn(w_gate, w_up, conv_w, conv_b, w_down):
    chunk_cols = lambda w: w.reshape(D_MODEL, N_FF_CHUNKS, FF_CHUNK).transpose(1, 0, 2).astype(BF16)
    return {
        "wg": chunk_cols(w_gate), "wu": chunk_cols(w_up),
        "cw": conv_w.reshape(CONV_W, N_FF_CHUNKS, FF_CHUNK).transpose(1, 0, 2),
        "cb": conv_b.reshape(N_FF_CHUNKS, 1, FF_CHUNK),
        "wd": w_down.reshape(N_FF_CHUNKS, FF_CHUNK, D_MODEL).astype(BF16),
    }


def _rope_tables(pos):
    half = ROPE_DIM // 2
    inv = ROPE_BASE ** (-jnp.arange(half, dtype=F32) / half)
    ang = pos.astype(F32)[:, None] * inv[None, :]
    cos2 = jnp.concatenate([jnp.cos(ang)] * 2, axis=1)
    sin2 = jnp.concatenate([jnp.sin(ang)] * 2, axis=1)
    n = pos.shape[0]
    scale = QK_SCALE_A
    zeros = lambda w: jnp.zeros((n, w), F32)
    tqc = scale * jnp.concatenate([jnp.ones((n, NOPE_DIM), F32), cos2, zeros(LANES - NOPE_DIM - ROPE_DIM)], axis=1)
    tqs = scale * jnp.concatenate([zeros(NOPE_DIM), sin2, zeros(LANES - NOPE_DIM - ROPE_DIM)], axis=1)
    tkc = jnp.concatenate([cos2, zeros(LANES - ROPE_DIM)], axis=1)
    tks = jnp.concatenate([sin2, zeros(LANES - ROPE_DIM)], axis=1)
    return tqc, tqs, tkc, tks


def _pad_rows(a, front, total):
    return jnp.pad(a, ((0, 0), (front, total - front - a.shape[1])) + ((0, 0),) * (a.ndim - 2))


def _trunk(x, mods, past, PA, PF, w_cd, wo_cd, rel, gains, tm, tq):
    nb, rows, _ = x.shape
    q_off = 0 if past is None else PAST_LEN
    pos = q_off + jnp.arange(rows)
    tabs = _rope_tables(pos)
    mix_pre_g, mix_post_g, ffn_pre_g, ffn_post_g = gains
    out = {}

    def mod6(l):
        m = mods[l].reshape(nb, 6, 1, D_MODEL)
        return [m[:, j] for j in range(6)]

    sh_m, sc_m, g_m, sh_f, sc_f, g_f = mod6(0)
    prompt = past is None
    (qcomb, ckv, krope, ckr, qb, kb16, vb16, kb32, vb32, logf, logf128, logft, *vbt) = _in_ab(
        x, sh_m, sc_m, mix_pre_g[0:1], PA, tabs, tm, prompt)
    out["a_ckv"], out["a_krope"], out["b_k"], out["b_v"], out["b_logf"] = ckv, krope, kb32, vb32, logf
    if prompt:
        tkk = rows
        ckr_all, kb_all, vb_all, lf_all, lft_all = ckr, kb16, vb16, logf128, logft
    else:
        tkk = -(-(PAST_LEN + rows) // ATT_BLOCK) * ATT_BLOCK
        cat = lambda p, n: _pad_rows(jnp.concatenate([p, n], axis=1), 0, tkk)
        ckr_past = jnp.concatenate([past["a_ckv"].astype(BF16),
                                    jnp.pad(past["a_krope"].astype(BF16), ((0, 0), (0, 0), (0, LANES - ROPE_DIM)))],
                                   axis=-1)
        ckr_all = cat(ckr_past, ckr)
        kb_all = cat(past["b_k"].reshape(nb, PAST_LEN, 512).astype(BF16), kb16)
        vb_all = cat(past["b_v"].reshape(nb, PAST_LEN, 512).astype(BF16), vb16)
        lf_all = cat(jnp.pad(past["b_logf"], ((0, 0), (0, 0), (0, LANES - N_HEADS))), logf128)
        lft_all = jnp.pad(jnp.concatenate([jnp.swapaxes(past["b_logf"], 1, 2), logft], axis=2),
                          ((0, 0), (0, 0), (0, tkk - PAST_LEN - rows)))
    if prompt:
        kcomb, vat = _kvup(ckr_all, PA["wkv"], ATT_BLOCK, True)
        _, cumt, ck_rep = _cum(lf_all, lft_all, True)
        o_a = _flash_t("mla", qcomb, kcomb, vat, tq=ATT_BLOCK)
        o_b = _flash_t("fox", qb, kb_all, vbt[0], cumt.reshape(nb, N_PAIRS, 2, rows), ck_rep, tq=ATT_BLOCK)
    else:
        kcomb, va = _kvup(ckr_all, PA["wkv"], ATT_SUB, False)
        cum, cumt = _cum(lf_all, lft_all, False)
        cq = cum[:, q_off:q_off + rows, :N_HEADS].reshape(nb, rows, N_PAIRS, 2).transpose(0, 2, 1, 3)
        nkb = tkk // ATT_BLOCK
        ckt = cumt.reshape(nb, N_PAIRS, 2, nkb, ATT_BLOCK).transpose(0, 1, 3, 2, 4)
        o_a = _flash("mla", qcomb, kcomb, va, tq=tq, q_off=q_off, n_valid=q_off + rows)
        o_b = _flash("fox", qb, kb_all, vb_all, cq, ckt, tq=tq, q_off=q_off, n_valid=q_off + rows)
    x = _out_proj(o_a, o_b, PA["wo1"], PA["wo2"], x, g_m, mix_post_g[0:1], tm)
    prev = jnp.zeros((nb, CONV_W - 1, D_FF), F32) if past is None else past["ffn_conv"][0]
    x, conv0 = _ffn(x, sh_f, sc_f, g_f, ffn_pre_g[0:1], ffn_post_g[0:1], prev, PF[0], tm)

    sh_m, sc_m, g_m, sh_f, sc_f, g_f = mod6(1)
    qc, kc16, vc16, qd, kd16, vd16, kc32, vc32, kd32, vd32, *vdt = _in_cd(x, sh_m, sc_m, mix_pre_g[1:2], w_cd, tm,
                                                                          prompt)
    out["d_k"], out["d_v"] = kd32, vd32
    if prompt:
        keep = min(BAND_CHUNKS * CHUNK, rows)
        out["c_k"], out["c_v"] = kc32[:, rows - keep:], vc32[:, rows - keep:]
        front = BAND_CHUNKS * CHUNK
        kc_all = _pad_rows(kc16, front, front + rows)
        vc_all = _pad_rows(vc16, front, front + rows)
        bias = _band_bias(rel, tq, front + tq, 0, -front)
        o_c = _band(qc, kc_all, vc_all, bias, tq=tq, pad_rows=front)
        o_d = _sb_t(qd, kd16, vdt[0], tq=ATT_BLOCK)
    else:
        out["c_k"], out["c_v"] = kc32, vc32
        n_c = past["c_k"].shape[1]
        front = BAND_SAMPLE_WIN - n_c - rows
        kc_all = _pad_rows(jnp.concatenate([past["c_k"].reshape(nb, n_c, 512).astype(BF16), kc16], axis=1), front,
                           BAND_SAMPLE_WIN)
        vc_all = _pad_rows(jnp.concatenate([past["c_v"].reshape(nb, n_c, 512).astype(BF16), vc16], axis=1), front,
                           BAND_SAMPLE_WIN)
        bias = _band_bias(rel, rows, BAND_SAMPLE_WIN, q_off, PAST_LEN - n_c - front)
        o_c = _band(qc, kc_all, vc_all, bias, tq=rows, pad_rows=0)
        kd_all = _pad_rows(jnp.concatenate([past["d_k"].reshape(nb, PAST_LEN, 512).astype(BF16), kd16], axis=1), 0, tkk)
        vd_all = _pad_rows(jnp.concatenate([past["d_v"].reshape(nb, PAST_LEN, 512).astype(BF16), vd16], axis=1), 0, tkk)
        o_d = _sb(qd, kd_all, vd_all, tq=tq, q_off=q_off)
    x = _out_proj(o_c, o_d, wo_cd[0], wo_cd[1], x, g_m, mix_post_g[1:2], tm)
    prev = jnp.zeros((nb, CONV_W - 1, D_FF), F32) if past is None else past["ffn_conv"][1]
    x, conv1 = _ffn(x, sh_f, sc_f, g_f, ffn_pre_g[1:2], ffn_post_g[1:2], prev, PF[1], tm)
    out["ffn_conv"] = jnp.stack([conv0, conv1])
    return x, out


def kernel(x_prompt, x_sample, c_prompt, c_sample, cache_a_ckv, cache_a_krope, cache_b_k, cache_b_v, cache_b_logf, cache_c_k, cache_c_v, cache_d_k, cache_d_v, state_ffn_conv, ada_w, ada_b, mix_pre_g, mix_post_g, ffn_pre_g, ffn_post_g, w_in_ab, b_f, q_a_g, kv_a_g, w_uq, w_ukv, w_out_ab, w_in_cd, rel_bias_c, w_out_cd, ffn_w_gate, ffn_w_up, ffn_conv_w, ffn_conv_b, ffn_w_down):
    nbp, nbs = x_prompt.shape[0], x_sample.shape[0]
    c_all = jnp.concatenate([c_prompt, c_sample, jnp.zeros((32 - nbp - nbs, D_MODEL), F32)], axis=0)
    mods = _ada(c_all, ada_w, ada_b)
    PA = _prep_ab(w_in_ab[0], b_f[0], q_a_g[0], kv_a_g[0], w_uq[0], w_ukv[0], w_out_ab[0])
    PF = [_prep_ffn(ffn_w_gate[l], ffn_w_up[l], ffn_conv_w[l], ffn_conv_b[l], ffn_w_down[l]) for l in range(DEPTH)]
    w_cd = w_in_cd[0].astype(BF16)
    wo_cd = (w_out_cd[0][:512].astype(BF16), w_out_cd[0][512:].astype(BF16))
    gains = (mix_pre_g, mix_post_g, ffn_pre_g, ffn_post_g)
    past = {"a_ckv": cache_a_ckv[0], "a_krope": cache_a_krope[0], "b_k": cache_b_k[0], "b_v": cache_b_v[0],
            "b_logf": cache_b_logf[0], "c_k": cache_c_k[0], "c_v": cache_c_v[0], "d_k": cache_d_k[0],
            "d_v": cache_d_v[0], "ffn_conv": state_ffn_conv}
    y_p, sp = _trunk(x_prompt, mods[:, :nbp], None, PA, PF, w_cd, wo_cd, rel_bias_c[0], gains, 512, ATT_SUB)
    y_s, ss = _trunk(x_sample, mods[:, nbp:nbp + nbs], past, PA, PF, w_cd, wo_cd, rel_bias_c[0], gains,
                     x_sample.shape[1], x_sample.shape[1])

    def heads(a):
        return a.reshape(a.shape[0], a.shape[1], N_HEADS, HEAD_DIM)[None]

    return (y_p, y_s,
            sp["a_ckv"][None], ss["a_ckv"][None], sp["a_krope"][None], ss["a_krope"][None],
            heads(sp["b_k"]), heads(ss["b_k"]), heads(sp["b_v"]), heads(ss["b_v"]),
            sp["b_logf"][None], ss["b_logf"][None],
            heads(sp["c_k"]), heads(ss["c_k"]), heads(sp["c_v"]), heads(ss["c_v"]),
            heads(sp["d_k"]), heads(ss["d_k"]), heads(sp["d_v"]), heads(ss["d_v"]),
            sp["ffn_conv"], ss["ffn_conv"])
```

```python
import functools

import jax
import jax.numpy as jnp
import numpy as np
from jax import lax
from jax.experimental import pallas as pl
from jax.experimental.pallas import tpu as pltpu

D_MODEL = 1024
DEPTH = 2
PAST_LEN = 1024
CHUNK = 64
HEAD_DIM = 64
N_HEADS = 8
N_PAIRS = N_HEADS // 2
Q_RANK = 256
KV_RANK = 128
NOPE_DIM = 64
ROPE_DIM = 32
ROPE_BASE = 10000.0
BAND_CHUNKS = 8
REL_CLIP = 128
D_FF = 2816
CONV_W = 3
EPS = 1e-6
NEG_INF = -1e30
LOG2E = 1.4426950408889634
QK_SCALE_64 = HEAD_DIM ** -0.5 * LOG2E
QK_SCALE_A = (NOPE_DIM + ROPE_DIM) ** -0.5 * LOG2E
SB_UNDERFLOW_BITS = 160.0
SB_PAIRS = 2
ONES_ROWS = 16

LANES = 128
FF_CHUNK = 2816
N_FF_CHUNKS = D_FF // FF_CHUNK
ATT_BLOCK = 512
ATT_SUB = 256
BAND_SAMPLE_WIN = 640
VMEM_LIMIT_BYTES = 56 * 1024 * 1024

F32 = jnp.float32
BF16 = jnp.bfloat16

AB_CQ, AB_CKV, AB_KR, AB_KRROT, AB_QB, AB_KB, AB_VB, AB_FB, AB_COLS = 0, 256, 384, 512, 640, 1152, 1664, 2176, 2304


def _params(n_axes):
    return pltpu.CompilerParams(dimension_semantics=("arbitrary",) * n_axes, vmem_limit_bytes=VMEM_LIMIT_BYTES)


def _rms(x, g):
    return x * lax.rsqrt(jnp.mean(x * x, axis=-1, keepdims=True) + EPS) * g


def _log_sigmoid(x):
    return jnp.minimum(x, 0.0) - jnp.log1p(jnp.exp(-jnp.abs(x)))


def _log2_sigmoids(z2):
    l2 = jnp.log2(1.0 + jnp.exp2(-jnp.abs(z2)))
    return jnp.minimum(z2, 0.0) - l2, jnp.maximum(z2, 0.0) + l2


def _dot(a, b):
    return jnp.dot(a, b, preferred_element_type=F32)


def _dot_nt(a, b):
    return lax.dot_general(a, b, (((1,), (1,)), ((), ())), preferred_element_type=F32)


def _split_bf16(x, n):
    parts = []
    for _ in range(n):
        p = x.astype(BF16)
        parts.append(p)
        x = x - p.astype(F32)
    return parts


def _tile_lanes(x, n):
    return x if n == 1 else jnp.concatenate([x] * n, axis=1)


def _ada_kernel(c_ref, w_ref, b_ref, o_ref):
    c = c_ref[...]
    cond = c * jax.nn.sigmoid(c)
    o_ref[0] = _dot(cond.astype(BF16), w_ref[0].astype(BF16)) + b_ref[0]


def _ada(c_all, ada_w, ada_b):
    rows = c_all.shape[0]
    tn = 1536
    return pl.pallas_call(
        _ada_kernel,
        grid=(DEPTH, 6 * D_MODEL // tn),
        in_specs=[
            pl.BlockSpec((rows, D_MODEL), lambda l, j: (0, 0)),
            pl.BlockSpec((1, D_MODEL, tn), lambda l, j: (l, 0, j)),
            pl.BlockSpec((1, 1, tn), lambda l, j: (l, 0, j)),
        ],
        out_specs=pl.BlockSpec((1, rows, tn), lambda l, j: (l, 0, j)),
        out_shape=jax.ShapeDtypeStruct((DEPTH, rows, 6 * D_MODEL), F32),
        compiler_params=_params(2),
        name="ada",
    )(c_all, ada_w, ada_b.reshape(DEPTH, 1, 6 * D_MODEL))


def _store_vt(vt_ref, v):
    vt = v.T
    blk = vt_ref.shape[-1]
    for p in range(N_PAIRS):
        for c in range(vt_ref.shape[2]):
            vt_ref[0, p, c] = vt[p * LANES:(p + 1) * LANES, c * blk:(c + 1) * blk].astype(BF16)


def _in_ab_kernel(x_ref, sh_ref, sc_ref, gpre_ref, w_ref, wft_ref, bf_ref, bft_ref, gq_ref, gkv_ref, wq_ref,
                  tqc_ref, tqs_ref, tkc_ref, tks_ref,
                  qcomb_ref, ckv_ref, krope_ref, ckr_ref, qb_ref, kb16_ref, vb16_ref, kb32_ref, vb32_ref,
                  logf_ref, logf128_ref, logft_ref, vbt_ref=None):
    x = x_ref[0]
    h = _rms(x, gpre_ref[...]) * (1.0 + sc_ref[0]) + sh_ref[0]
    hb = h.astype(BF16)
    y = _dot(hb, w_ref[...])
    cq = _rms(y[:, AB_CQ:AB_CQ + Q_RANK], gq_ref[...]).astype(BF16)
    qa = _dot(cq, wq_ref[...])
    tqc = tqc_ref[...]
    tqs = tqs_ref[...]
    half = N_HEADS * LANES
    for hh in range(N_HEADS):
        lo = hh * LANES
        qcomb_ref[0, :, lo:lo + LANES] = (qa[:, lo:lo + LANES] * tqc + qa[:, half + lo:half + lo + LANES] * tqs).astype(BF16)
    ckv = _rms(y[:, AB_CKV:AB_CKV + KV_RANK], gkv_ref[...])
    ckv_ref[0] = ckv
    kr = y[:, AB_KR:AB_KR + LANES] * tkc_ref[...] + y[:, AB_KRROT:AB_KRROT + LANES] * tks_ref[...]
    krope_ref[0] = kr[:, :ROPE_DIM]
    ckr_ref[0, :, 0:LANES] = ckv.astype(BF16)
    ckr_ref[0, :, LANES:2 * LANES] = kr.astype(BF16)
    qb_ref[0] = (y[:, AB_QB:AB_QB + 512] * QK_SCALE_64).astype(BF16)
    kb = y[:, AB_KB:AB_KB + 512]
    vb = y[:, AB_VB:AB_VB + 512]
    kb32_ref[0] = kb
    vb32_ref[0] = vb
    kb16_ref[0] = kb.astype(BF16)
    vb16_ref[0] = vb.astype(BF16)
    if vbt_ref is not None:
        _store_vt(vbt_ref, vb)
    logf = _log_sigmoid(y[:, AB_FB:AB_FB + LANES] + bf_ref[...])
    logf128_ref[0] = logf
    logf_ref[0] = logf[:, :N_HEADS]
    ft = _dot_nt(wft_ref[...], hb)
    logft_ref[0] = _log_sigmoid(ft[:N_HEADS] + bft_ref[...])


def _vt_out(nb, rows, tm, blk=None):
    blk = tm if blk is None else blk
    return (jax.ShapeDtypeStruct((nb, N_PAIRS, rows // blk, LANES, blk), BF16),
            pl.BlockSpec((1, N_PAIRS, tm // blk, LANES, blk), lambda b, t: (b, 0, t, 0, 0)))


def _in_ab(x, sh, sc, gpre, P, tabs, tm, with_vt):
    nb, rows, _ = x.shape
    nt = rows // tm
    row3 = lambda w: pl.BlockSpec((1, tm, w), lambda b, t: (b, t, 0))
    mod = pl.BlockSpec((1, 1, D_MODEL), lambda b, t: (b, 0, 0))
    const = lambda a: pl.BlockSpec(a.shape, lambda b, t: (0,) * a.ndim)
    tab = pl.BlockSpec((tm, LANES), lambda b, t: (t, 0))
    out_shapes = [
        jax.ShapeDtypeStruct((nb, rows, N_HEADS * LANES), BF16),
        jax.ShapeDtypeStruct((nb, rows, KV_RANK), F32),
        jax.ShapeDtypeStruct((nb, rows, ROPE_DIM), F32),
        jax.ShapeDtypeStruct((nb, rows, 2 * LANES), BF16),
        jax.ShapeDtypeStruct((nb, rows, 512), BF16),
        jax.ShapeDtypeStruct((nb, rows, 512), BF16),
        jax.ShapeDtypeStruct((nb, rows, 512), BF16),
        jax.ShapeDtypeStruct((nb, rows, 512), F32),
        jax.ShapeDtypeStruct((nb, rows, 512), F32),
        jax.ShapeDtypeStruct((nb, rows, N_HEADS), F32),
        jax.ShapeDtypeStruct((nb, rows, LANES), F32),
        jax.ShapeDtypeStruct((nb, N_HEADS, rows), F32),
    ]
    out_specs = [row3(N_HEADS * LANES), row3(KV_RANK), row3(ROPE_DIM), row3(2 * LANES), row3(512), row3(512),
                 row3(512), row3(512), row3(512), row3(N_HEADS), row3(LANES),
                 pl.BlockSpec((1, N_HEADS, tm), lambda b, t: (b, 0, t))]
    if with_vt:
        vt_shape, vt_spec = _vt_out(nb, rows, tm)
        out_shapes.append(vt_shape)
        out_specs.append(vt_spec)
    consts = [P["w_ab"], P["w_ft"], P["bf128"], P["bft"], P["gq"], P["gkv"], P["wq"]]
    return pl.pallas_call(
        _in_ab_kernel,
        grid=(nb, nt),
        in_specs=[row3(D_MODEL), mod, mod, const(gpre)] + [const(a) for a in consts] + [tab] * 4,
        out_specs=out_specs,
        out_shape=out_shapes,
        compiler_params=_params(2),
        name="in_ab",
    )(x, sh, sc, gpre, *consts, *tabs)


def _kvup_kernel(ckr_ref, w_ref, k_ref, v_ref):
    y = _dot(ckr_ref[0], w_ref[...])
    k_ref[0] = y[:, :N_HEADS * LANES].astype(BF16)
    v_ref[0] = y[:, N_HEADS * LANES:].astype(BF16)


def _kvup_t_kernel(ckr_ref, w_ref, k_ref, vt_ref):
    y = _dot(ckr_ref[0], w_ref[...])
    k_ref[0] = y[:, :N_HEADS * LANES].astype(BF16)
    _store_vt(vt_ref, y[:, N_HEADS * LANES:])


def _kvup(ckr, w, tm, with_vt):
    nb, rows, _ = ckr.shape
    if with_vt:
        v_shape, v_spec = _vt_out(nb, rows, tm)
    else:
        v_shape = jax.ShapeDtypeStruct((nb, rows, 512), BF16)
        v_spec = pl.BlockSpec((1, tm, 512), lambda b, t: (b, t, 0))
    return pl.pallas_call(
        _kvup_t_kernel if with_vt else _kvup_kernel,
        grid=(nb, rows // tm),
        in_specs=[pl.BlockSpec((1, tm, 2 * LANES), lambda b, t: (b, t, 0)),
                  pl.BlockSpec(w.shape, lambda b, t: (0, 0))],
        out_specs=[pl.BlockSpec((1, tm, N_HEADS * LANES), lambda b, t: (b, t, 0)), v_spec],
        out_shape=[jax.ShapeDtypeStruct((nb, rows, N_HEADS * LANES), BF16), v_shape],
        compiler_params=_params(2),
        name="kvup",
    )(ckr, w)


def _cum_kernel(x_ref, xt_ref, c_ref, ct_ref, rep_ref=None, *, n_chunks, tc):
    r = lax.broadcasted_iota(jnp.int32, (tc, tc), 0)
    c = lax.broadcasted_iota(jnp.int32, (tc, tc), 1)
    lower = jnp.where(c <= r, 1.0, 0.0).astype(BF16)
    upper = jnp.where(r <= c, 1.0, 0.0).astype(BF16)
    carry = jnp.zeros((1, LANES), F32)
    carry_t = jnp.zeros((N_HEADS, 1), F32)
    for ci in range(n_chunks):
        sl = slice(ci * tc, (ci + 1) * tc)
        cs = carry
        for p in _split_bf16(x_ref[0, sl, :], 3):
            cs = cs + _dot(lower, p)
        c_ref[0, sl, :] = cs * LOG2E
        if rep_ref is not None:
            for h in range(N_HEADS):
                rep_ref[0, h, sl, :] = jnp.broadcast_to(cs[:, h:h + 1] * LOG2E, (tc, LANES))
        carry = cs[tc - 1:tc, :]
        xt = jnp.concatenate([xt_ref[0, :, sl], jnp.zeros((N_HEADS, tc), F32)], axis=0)
        cst = jnp.zeros((2 * N_HEADS, tc), F32)
        for p in _split_bf16(xt, 3):
            cst = cst + _dot(p, upper)
        cst = cst[:N_HEADS] + carry_t
        ct_ref[0, :, sl] = cst * LOG2E
        carry_t = cst[:, tc - 1:tc]


def _cum(logf128, logft, with_rep):
    nb, tk, _ = logf128.shape
    tc = ATT_SUB
    out_specs = [pl.BlockSpec((1, tk, LANES), lambda b: (b, 0, 0)),
                 pl.BlockSpec((1, N_HEADS, tk), lambda b: (b, 0, 0))]
    out_shape = [jax.ShapeDtypeStruct((nb, tk, LANES), F32), jax.ShapeDtypeStruct((nb, N_HEADS, tk), F32)]
    if with_rep:
        out_specs.append(pl.BlockSpec((1, N_HEADS, tk, LANES), lambda b: (b, 0, 0, 0)))
        out_shape.append(jax.ShapeDtypeStruct((nb, N_HEADS, tk, LANES), F32))
    return pl.pallas_call(
        functools.partial(_cum_kernel, n_chunks=tk // tc, tc=tc),
        grid=(nb,),
        in_specs=[pl.BlockSpec((1, tk, LANES), lambda b: (b, 0, 0)),
                  pl.BlockSpec((1, N_HEADS, tk), lambda b: (b, 0, 0))],
        out_specs=out_specs,
        out_shape=out_shape,
        compiler_params=_params(1),
        name="cum",
    )(logf128, logft)


def _flash_kernel(*refs, mode, tq, tk, q_off, n_valid, tail_widths):
    if mode == "fox":
        q_ref, k_ref, v_ref, cq_ref, ckt_ref, o_ref, m_s, l_s, acc_s = refs
    else:
        q_ref, k_ref, v_ref, o_ref, m_s, l_s, acc_s = refs
    i = pl.program_id(2)
    qpos0 = q_off + i * tq
    n_full = qpos0 // tk
    rem = qpos0 + tq - n_full * tk
    lane = lax.broadcasted_iota(jnp.int32, (tq, LANES), 1)

    qs, kcols, cq_tiles = [], [], []
    for hp in range(N_HEADS):
        p, h = divmod(hp, 2)
        pair = slice(p * LANES, (p + 1) * LANES)
        if mode == "fox":
            q = q_ref[0, :, pair]
            qs.append(jnp.where((lane < HEAD_DIM) if h == 0 else (lane >= HEAD_DIM), q, jnp.zeros_like(q)))
            kcols.append(pair)
            cq_tiles.append(jnp.broadcast_to(cq_ref[0, p][:, h:h + 1], (tq, LANES)))
        else:
            qs.append(q_ref[0, :, hp * LANES:(hp + 1) * LANES])
            kcols.append(slice(hp * LANES, (hp + 1) * LANES))
        m_s[hp] = jnp.full((tq, LANES), NEG_INF, F32)
        l_s[hp] = jnp.zeros((tq, LANES), F32)
        acc_s[hp] = jnp.zeros((tq, LANES), F32)

    def step(j, width, masked):
        nrep = width // LANES
        r0 = pl.multiple_of(j * tk, tk)
        if masked:
            kpos = n_full * tk + lax.broadcasted_iota(jnp.int32, (tq, width), 1)
            qpos = qpos0 + lax.broadcasted_iota(jnp.int32, (tq, width), 0)
            if mode == "fox":
                mask = kpos <= qpos
            else:
                mask = (lax.shift_right_logical(kpos, 6) <= lax.shift_right_logical(qpos, 6)) & (kpos < n_valid)
        for hp in range(N_HEADS):
            p, h = divmod(hp, 2)
            s = _dot_nt(qs[hp], k_ref[0, pl.ds(r0, width), kcols[hp]])
            if mode == "fox":
                s = s + (_tile_lanes(cq_tiles[hp], nrep) - ckt_ref[0, p, j][h:h + 1, :width])
            if masked:
                s = jnp.where(mask, s, NEG_INF)
            m_prev = m_s[hp]
            m_next = jnp.maximum(m_prev, jnp.max(s, axis=1, keepdims=True))
            pr = jnp.exp2(s - _tile_lanes(m_next, nrep))
            alpha = jnp.exp2(m_prev - m_next)
            l_s[hp] = alpha * l_s[hp] + jnp.sum(pr, axis=1, keepdims=True)
            v = v_ref[0, pl.ds(r0, width), p * LANES:(p + 1) * LANES]
            acc_s[hp] = alpha * acc_s[hp] + _dot(pr.astype(BF16), v)
            m_s[hp] = m_next

    def body(j, carry):
        step(j, tk, False)
        return carry

    lax.fori_loop(0, n_full, body, 0)
    if len(tail_widths) == 1:
        step(n_full, tail_widths[0], True)
    else:
        lo_w, hi_w = tail_widths
        pl.when(rem <= lo_w)(lambda: step(n_full, lo_w, True))
        pl.when(rem > lo_w)(lambda: step(n_full, hi_w, True))

    for p in range(N_PAIRS):
        o0 = acc_s[2 * p] / l_s[2 * p]
        o1 = acc_s[2 * p + 1] / l_s[2 * p + 1]
        o_ref[0, :, p * LANES:(p + 1) * LANES] = jnp.where(lane < HEAD_DIM, o0, o1).astype(BF16)


def _tail_widths(tk):
    return (ATT_SUB, tk) if tk > ATT_SUB else (tk,)


def _flash(mode, q, k, v, cq=None, ckt=None, *, tq, q_off, n_valid):
    nb, rows, _ = q.shape
    tkk = k.shape[1]
    tk = ATT_BLOCK
    kw = q.shape[2]
    in_specs = [pl.BlockSpec((1, tq, kw), lambda b, p, i: (b, i, 0)),
                pl.BlockSpec((1, tkk, kw), lambda b, p, i: (b, 0, 0)),
                pl.BlockSpec((1, tkk, 512), lambda b, p, i: (b, 0, 0))]
    args = [q, k, v]
    if mode == "fox":
        in_specs += [pl.BlockSpec((1, N_PAIRS, tq, 2), lambda b, p, i: (b, 0, i, 0)),
                     pl.BlockSpec((1, N_PAIRS, tkk // tk, 2, tk), lambda b, p, i: (b, 0, 0, 0, 0))]
        args += [cq, ckt]
    return pl.pallas_call(
        functools.partial(_flash_kernel, mode=mode, tq=tq, tk=tk, q_off=q_off, n_valid=n_valid,
                          tail_widths=_tail_widths(tk)),
        grid=(nb, 1, rows // tq),
        in_specs=in_specs,
        out_specs=pl.BlockSpec((1, tq, 512), lambda b, p, i: (b, i, 0)),
        out_shape=jax.ShapeDtypeStruct((nb, rows, 512), BF16),
        scratch_shapes=[pltpu.VMEM((N_HEADS, tq, LANES), F32)] * 3,
        compiler_params=_params(3),
        name="attn_" + mode,
    )(*args)


def _sb_kernel(q_ref, k_ref, v_ref, o_ref, acc_s, r_s, u_s, *, tq, tk, q_off, tail_widths):
    i = pl.program_id(2)
    qpos0 = q_off + i * tq
    n_full = qpos0 // tk
    rem = qpos0 + tq - n_full * tk
    nrep = ATT_SUB // LANES
    lane = lax.broadcasted_iota(jnp.int32, (tq, LANES), 1)
    ur = lax.broadcasted_iota(jnp.int32, (ATT_SUB, ATT_SUB), 0)
    uc = lax.broadcasted_iota(jnp.int32, (ATT_SUB, ATT_SUB), 1)
    u_s[...] = jnp.where(ur > uc, 1.0, 0.0).astype(BF16)
    qs = []
    for hp in range(N_HEADS):
        p, h = divmod(hp, 2)
        q = q_ref[0, :, p * LANES:(p + 1) * LANES]
        qs.append(jnp.where((lane < HEAD_DIM) if h == 0 else (lane >= HEAD_DIM), q, jnp.zeros_like(q)))
    acc_s[...] = jnp.zeros((N_HEADS, tq, LANES), F32)
    r_s[...] = jnp.zeros((N_HEADS, tq, LANES), F32)

    def step(j, width, masked):
        r0 = pl.multiple_of(j * tk, tk)
        if masked:
            kpos = n_full * tk + lax.broadcasted_iota(jnp.int32, (tq, width), 1)
            qpos = qpos0 + lax.broadcasted_iota(jnp.int32, (tq, width), 0)
            mask = kpos < qpos
        u = u_s[...]
        for h in range(N_HEADS):
            pair = slice(h // 2 * LANES, (h // 2 + 1) * LANES)
            k = k_ref[0, pl.ds(r0, width), pair]
            v = v_ref[0, pl.ds(r0, width), pair]
            z = _dot_nt(qs[h], k)
            lsz, nlk = _log2_sigmoids(z)
            if masked:
                nlk = jnp.where(mask, nlk, 0.0)
            r = r_s[h]
            ws = []
            for sb in reversed(range(width // ATT_SUB)):
                cols = slice(sb * ATT_SUB, (sb + 1) * ATT_SUB)
                nlk_sb = nlk[:, cols]
                hi, lo = _split_bf16(nlk_sb, 2)
                after = _dot(hi, u) + _dot(lo, u) + _tile_lanes(r, nrep)
                w = jnp.exp2(lsz[:, cols] - after)
                if masked:
                    w = jnp.where(mask[:, cols], w, 0.0)
                ws.append(w.astype(BF16))
                r = r + jnp.sum(nlk_sb, axis=1, keepdims=True)
            w_all = ws[0] if len(ws) == 1 else jnp.concatenate(ws[::-1], axis=1)
            acc_s[h] = acc_s[h] + _dot(w_all, v)
            r_s[h] = r

    if len(tail_widths) == 1:
        step(n_full, tail_widths[0], True)
    else:
        lo_w, hi_w = tail_widths
        pl.when(rem <= lo_w)(lambda: step(n_full, lo_w, True))
        pl.when(rem > lo_w)(lambda: step(n_full, hi_w, True))

    def body(jj, carry):
        step(n_full - 1 - jj, tk, False)
        return carry

    lax.fori_loop(0, n_full, body, 0)
    for p in range(N_PAIRS):
        o_ref[0, :, p * LANES:(p + 1) * LANES] = jnp.where(lane < HEAD_DIM, acc_s[2 * p], acc_s[2 * p + 1]).astype(BF16)


def _sb(q, k, v, *, tq, q_off):
    nb, rows, _ = q.shape
    tkk = k.shape[1]
    tk = ATT_BLOCK
    return pl.pallas_call(
        functools.partial(_sb_kernel, tq=tq, tk=tk, q_off=q_off, tail_widths=_tail_widths(tk)),
        grid=(nb, 1, rows // tq),
        in_specs=[pl.BlockSpec((1, tq, 512), lambda b, p, i: (b, i, 0)),
                  pl.BlockSpec((1, tkk, 512), lambda b, p, i: (b, 0, 0)),
                  pl.BlockSpec((1, tkk, 512), lambda b, p, i: (b, 0, 0))],
        out_specs=pl.BlockSpec((1, tq, 512), lambda b, p, i: (b, i, 0)),
        out_shape=jax.ShapeDtypeStruct((nb, rows, 512), BF16),
        scratch_shapes=[pltpu.VMEM((N_HEADS, tq, LANES), F32), pltpu.VMEM((N_HEADS, tq, LANES), F32),
                        pltpu.VMEM((ATT_SUB, ATT_SUB), BF16)],
        compiler_params=_params(3),
        name="attn_sb",
    )(q, k, v)


def _pair_queries(q_ref, mode, tq):
    lane = lax.broadcasted_iota(jnp.int32, (tq, LANES), 1)
    if mode == "mla":
        return [q_ref[0, :, h * LANES:(h + 1) * LANES] for h in range(2)], [slice(h * LANES, (h + 1) * LANES) for h in range(2)]
    q = q_ref[0]
    qs = [jnp.where((lane < HEAD_DIM) if h == 0 else (lane >= HEAD_DIM), q, jnp.zeros_like(q)) for h in range(2)]
    return qs, [slice(0, LANES)] * 2


def _store_heads_t(o_ref, o0, o1):
    o_ref[0] = jnp.concatenate([o0, o1], axis=0).T.astype(BF16)


def _flash_t_kernel(*refs, mode, tq, tk):
    if mode == "fox":
        q_ref, k_ref, vt_ref, cq_ref, ck_ref, o_ref, m_s, acc_s, s_buf = refs
    else:
        q_ref, k_ref, vt_ref, o_ref, m_s, acc_s, s_buf = refs
    n_full = pl.program_id(2)
    qs, kcols = _pair_queries(q_ref, mode, tq)
    m_s[...] = jnp.full((2, 1, tq), NEG_INF, F32)
    acc_s[...] = jnp.zeros((2, HEAD_DIM + ONES_ROWS, tq), F32)
    ones = jnp.ones((ONES_ROWS, tk), BF16)

    def scores(j, slot):
        r0 = pl.multiple_of(j * tk, tk)
        for h in range(2):
            s = _dot_nt(k_ref[0, pl.ds(r0, tk), kcols[h]], qs[h])
            if mode == "fox":
                s = s - _tile_lanes(ck_ref[0, h, pl.ds(r0, tk), :], tq // LANES)
            s_buf[slot, h] = s

    def update(j, slot, h, n_keys, q_lo, q_hi, masked):
        s = s_buf[slot, h, 0:n_keys, q_lo:q_hi]
        if masked:
            kpos = lax.broadcasted_iota(jnp.int32, s.shape, 0)
            qpos = q_lo + lax.broadcasted_iota(jnp.int32, s.shape, 1)
            if mode == "fox":
                mask = kpos <= qpos
            else:
                mask = lax.shift_right_logical(kpos, 6) <= lax.shift_right_logical(qpos, 6)
            s = jnp.where(mask, s, NEG_INF)
        m_prev = m_s[h, :, q_lo:q_hi]
        if mode == "fox":
            cq = cq_ref[0, 0, h:h + 1, q_lo:q_hi]
            m_next = jnp.maximum(m_prev, jnp.max(s, axis=0, keepdims=True) + cq)
            p = jnp.exp2(s - (m_next - cq))
        else:
            m_next = jnp.maximum(m_prev, jnp.max(s, axis=0, keepdims=True))
            p = jnp.exp2(s - m_next)
        alpha = jnp.exp2(m_prev - m_next)
        vt = jnp.concatenate([vt_ref[0, 0, j, h * HEAD_DIM:(h + 1) * HEAD_DIM, 0:n_keys], ones[:, 0:n_keys]], axis=0)
        acc_s[h, :, q_lo:q_hi] = alpha * acc_s[h, :, q_lo:q_hi] + _dot(vt, p.astype(BF16))
        m_s[h, :, q_lo:q_hi] = m_next

    def consume(j, slot, masked):
        for h in range(2):
            if masked:
                update(j, slot, h, tk // 2, 0, tq // 2, True)
                update(j, slot, h, tk, tq // 2, tq, True)
            else:
                update(j, slot, h, tk, 0, tq, False)

    scores(0, 0)

    def body(jj, carry):
        j = 2 * jj
        scores(j + 1, 1)
        consume(j, 0, False)
        scores(j + 2, 0)
        consume(j + 1, 1, False)
        return carry

    lax.fori_loop(0, n_full // 2, body, 0)

    @pl.when(n_full % 2 == 1)
    def _():
        scores(n_full, 1)
        consume(n_full - 1, 0, False)
        consume(n_full, 1, True)

    @pl.when(n_full % 2 == 0)
    def _():
        consume(n_full, 0, True)

    _store_heads_t(o_ref, *[acc_s[h, :HEAD_DIM] / acc_s[h, HEAD_DIM:HEAD_DIM + 1] for h in range(2)])


def _flash_t(mode, q, k, vt, cq=None, ck=None, *, tq):
    nb, rows, _ = q.shape
    tk = ATT_BLOCK
    assert tq == tk
    nkb = rows // tk
    kw = 2 * LANES if mode == "mla" else LANES
    in_specs = [pl.BlockSpec((1, tq, kw), lambda b, p, i: (b, i, p)),
                pl.BlockSpec((1, rows, kw), lambda b, p, i: (b, 0, p)),
                pl.BlockSpec((1, 1, nkb, LANES, tk), lambda b, p, i: (b, p, 0, 0, 0))]
    args = [q, k, vt]
    if mode == "fox":
        in_specs += [pl.BlockSpec((1, 1, 2, tq), lambda b, p, i: (b, p, 0, i)),
                     pl.BlockSpec((1, 2, rows, LANES), lambda b, p, i: (b, p, 0, 0))]
        args += [cq, ck]
    return pl.pallas_call(
        functools.partial(_flash_t_kernel, mode=mode, tq=tq, tk=tk),
        grid=(nb, N_PAIRS, rows // tq),
        in_specs=in_specs,
        out_specs=pl.BlockSpec((1, tq, LANES), lambda b, p, i: (b, i, p)),
        out_shape=jax.ShapeDtypeStruct((nb, rows, 512), BF16),
        scratch_shapes=[pltpu.VMEM((2, 1, tq), F32), pltpu.VMEM((2, HEAD_DIM + ONES_ROWS, tq), F32),
                        pltpu.VMEM((2, 2, tk, tq), F32)],
        compiler_params=_params(3),
        name="attn_" + mode,
    )(*args)


def _sb_t_kernel(q_ref, k_ref, vt_ref, o_ref, acc_s, r_s, ut_s, *, tq, tk):
    i = pl.program_id(2)
    n_diag = tq // tk
    nk = (i + 1) * n_diag
    ur = lax.broadcasted_iota(jnp.int32, (tk, 2 * tk), 0)
    uc = lax.broadcasted_iota(jnp.int32, (tk, 2 * tk), 1) & (tk - 1)
    ut_s[...] = jnp.where(uc > ur, 1.0, 0.0).astype(BF16)
    n_heads = 2 * SB_PAIRS
    lane = lax.broadcasted_iota(jnp.int32, (tq, LANES), 1)
    qs = []
    for hp in range(n_heads):
        q = q_ref[0, :, hp // 2 * LANES:(hp // 2 + 1) * LANES]
        qs.append(jnp.where((lane < HEAD_DIM) if hp % 2 == 0 else (lane >= HEAD_DIM), q, jnp.zeros_like(q)))
    acc_s[...] = jnp.zeros((n_heads, HEAD_DIM, tq), F32)
    r_s[...] = jnp.zeros((n_heads, 1, tq), F32)

    def step(j, masked, q_lo=0):
        nq = tq - q_lo
        r0 = pl.multiple_of(j * tk, tk)
        if masked:
            kpos = j * tk + lax.broadcasted_iota(jnp.int32, (tk, nq), 0)
            qpos = i * tq + q_lo + lax.broadcasted_iota(jnp.int32, (tk, nq), 1)
            mask = kpos < qpos
        ut = ut_s[...]
        for h in range(n_heads):
            k = k_ref[0, pl.ds(r0, tk), h // 2 * LANES:(h // 2 + 1) * LANES]
            z = _dot_nt(k, qs[h][q_lo:])
            lsz = jnp.minimum(z, 0.0) - jnp.log2(1.0 + jnp.exp2(-jnp.abs(z)))
            nlk = z - lsz
            if masked:
                nlk = jnp.where(mask, nlk, 0.0)
            hi, lo = _split_bf16(nlk, 2)
            r = r_s[h, :, q_lo:]
            w = jnp.exp2(lsz - _dot(ut, jnp.concatenate([hi, lo], axis=0)) - r)
            if masked:
                w = jnp.where(mask, w, 0.0)
            vt = vt_ref[0, h // 2, j, h % 2 * HEAD_DIM:(h % 2 + 1) * HEAD_DIM, :]
            acc_s[h, :, q_lo:] = acc_s[h, :, q_lo:] + _dot(vt, w.astype(BF16))
            r_s[h, :, q_lo:] = r + jnp.sum(nlk, axis=0, keepdims=True)

    def diagonal():
        for d in range(n_diag):
            step(nk - 1 - d, True, q_lo=(n_diag - 1 - d) * tk)

    @pl.when(i == 0)
    def _():
        diagonal()

    @pl.when(i > 0)
    def _():
        diagonal()
        step(nk - 1 - n_diag, False)

    def r_min():
        return functools.reduce(jnp.minimum, [jnp.min(r_s[h]) for h in range(n_heads)])

    def cond(c):
        j, rmin = c
        return (j >= 0) & (rmin <= SB_UNDERFLOW_BITS)

    def body(c):
        j, _ = c
        step(j, False)
        return j - 1, r_min()

    lax.while_loop(cond, body, (nk - 1 - n_diag - jnp.minimum(i, 1), r_min()))
    for p in range(SB_PAIRS):
        pair = jnp.concatenate([acc_s[2 * p], acc_s[2 * p + 1]], axis=0)
        o_ref[0, :, p * LANES:(p + 1) * LANES] = pair.T.astype(BF16)


def _sb_t(q, k, vt, *, tq):
    nb, rows, _ = q.shape
    tk = vt.shape[-1]
    w = SB_PAIRS * LANES
    return pl.pallas_call(
        functools.partial(_sb_t_kernel, tq=tq, tk=tk),
        grid=(nb, N_PAIRS // SB_PAIRS, rows // tq),
        in_specs=[pl.BlockSpec((1, tq, w), lambda b, p, i: (b, i, p)),
                  pl.BlockSpec((1, rows, w), lambda b, p, i: (b, 0, p)),
                  pl.BlockSpec((1, SB_PAIRS, rows // tk, LANES, tk), lambda b, p, i: (b, p, 0, 0, 0))],
        out_specs=pl.BlockSpec((1, tq, w), lambda b, p, i: (b, i, p)),
        out_shape=jax.ShapeDtypeStruct((nb, rows, 512), BF16),
        scratch_shapes=[pltpu.VMEM((2 * SB_PAIRS, HEAD_DIM, tq), F32), pltpu.VMEM((2 * SB_PAIRS, 1, tq), F32),
                        pltpu.VMEM((tk, 2 * tk), BF16)],
        compiler_params=_params(3),
        name="attn_sb",
    )(q, k, vt)


def _bias_kernel(tab_ref, o_ref, *, nq, nk, q0, k0):
    h = pl.program_id(0)
    qpos = q0 + lax.broadcasted_iota(jnp.int32, (nq, nk), 0)
    kpos = k0 + lax.broadcasted_iota(jnp.int32, (nq, nk), 1)
    qc = lax.shift_right_arithmetic(qpos, 6)
    kc = lax.shift_right_arithmetic(kpos, 6)
    band = (kc <= qc) & (kc >= qc - BAND_CHUNKS)
    width = pl.next_power_of_2(nq + nk)
    x = lax.broadcasted_iota(jnp.int32, (8, width), 1)
    idx = jnp.clip(q0 - k0 + nq - x, -REL_CLIP, REL_CLIP) + REL_CLIP

    def body(r, acc):
        return jnp.where(idx == r, tab_ref[r * N_HEADS + h], acc)

    ext = lax.fori_loop(0, 2 * REL_CLIP + 1, body, jnp.zeros((8, width), F32))
    lines = jnp.broadcast_to(ext[0:1], (nq, width))
    bias = pltpu.roll(lines, width - nq, 1, stride=1, stride_axis=0)[:, :nk]
    o_ref[0] = jnp.where(band, bias * LOG2E, NEG_INF)


def _band_bias(rel_table, nq, nk, q0, k0):
    return pl.pallas_call(
        functools.partial(_bias_kernel, nq=nq, nk=nk, q0=q0, k0=k0),
        grid=(N_HEADS,),
        in_specs=[pl.BlockSpec(memory_space=pltpu.SMEM)],
        out_specs=pl.BlockSpec((1, nq, nk), lambda h: (h, 0, 0)),
        out_shape=jax.ShapeDtypeStruct((N_HEADS, nq, nk), F32),
        compiler_params=_params(1),
        name="band_bias",
    )(rel_table.reshape(-1))


def _band_kernel(q_ref, k_ref, v_ref, bias_ref, o_ref, *, tq, n_sub, win, pad_rows):
    lane = lax.broadcasted_iota(jnp.int32, (tq, LANES), 1)
    col = lax.broadcasted_iota(jnp.int32, (tq, win), 1)
    for c in range(n_sub):
        i = pl.program_id(2) * n_sub + c
        start = pl.multiple_of(i * tq, tq)
        k = k_ref[0, pl.ds(start, win), :]
        v = v_ref[0, pl.ds(start, win), :]
        v1 = jnp.concatenate([v, jnp.ones_like(v)], axis=1)
        valid = col >= pad_rows - i * tq
        q = q_ref[0, c * tq:(c + 1) * tq, :]
        outs = []
        for h in range(2):
            qh = jnp.where((lane < HEAD_DIM) if h == 0 else (lane >= HEAD_DIM), q, jnp.zeros_like(q))
            s = jnp.where(valid, _dot_nt(qh, k) + bias_ref[h], NEG_INF)
            p = jnp.exp2(s - jnp.max(s, axis=1, keepdims=True))
            pv = _dot(p.astype(BF16), v1)
            outs.append(pv[:, :LANES] / pv[:, LANES:])
        o_ref[0, c * tq:(c + 1) * tq, :] = jnp.where(lane < HEAD_DIM, outs[0], outs[1]).astype(BF16)


def _band(q, k_pad, v_pad, bias, *, tq, n_sub, pad_rows):
    nb, rows, _ = q.shape
    tkk = k_pad.shape[1]
    win = bias.shape[2]
    tqq = tq * n_sub
    return pl.pallas_call(
        functools.partial(_band_kernel, tq=tq, n_sub=n_sub, win=win, pad_rows=pad_rows),
        grid=(nb, N_PAIRS, rows // tqq),
        in_specs=[pl.BlockSpec((1, tqq, LANES), lambda b, p, i: (b, i, p)),
                  pl.BlockSpec((1, tkk, LANES), lambda b, p, i: (b, 0, p)),
                  pl.BlockSpec((1, tkk, LANES), lambda b, p, i: (b, 0, p)),
                  pl.BlockSpec((2, tq, win), lambda b, p, i: (p, 0, 0))],
        out_specs=pl.BlockSpec((1, tqq, LANES), lambda b, p, i: (b, i, p)),
        out_shape=jax.ShapeDtypeStruct((nb, rows, 512), BF16),
        compiler_params=_params(3),
        name="attn_band",
    )(q, k_pad, v_pad, bias)


def _in_cd_kernel(x_ref, sh_ref, sc_ref, gpre_ref, w_ref,
                  qc_ref, kc16_ref, vc16_ref, qd_ref, kd16_ref, vd16_ref, kc32_ref, vc32_ref, kd32_ref, vd32_ref,
                  vdt_ref=None):
    x = x_ref[0]
    h = _rms(x, gpre_ref[...]) * (1.0 + sc_ref[0]) + sh_ref[0]
    y = _dot(h.astype(BF16), w_ref[...])
    qc_ref[0] = (y[:, 0:512] * QK_SCALE_64).astype(BF16)
    qd_ref[0] = (y[:, 1536:2048] * QK_SCALE_64).astype(BF16)
    for lo, r32, r16 in ((512, kc32_ref, kc16_ref), (1024, vc32_ref, vc16_ref),
                         (2048, kd32_ref, kd16_ref), (2560, vd32_ref, vd16_ref)):
        part = y[:, lo:lo + 512]
        r32[0] = part
        r16[0] = part.astype(BF16)
    if vdt_ref is not None:
        _store_vt(vdt_ref, y[:, 2560:3072])


def _in_cd(x, sh, sc, gpre, w, tm, with_vt):
    nb, rows, _ = x.shape
    row3 = lambda wd: pl.BlockSpec((1, tm, wd), lambda b, t: (b, t, 0))
    mod = pl.BlockSpec((1, 1, D_MODEL), lambda b, t: (b, 0, 0))
    const = lambda a: pl.BlockSpec(a.shape, lambda b, t: (0,) * a.ndim)
    out_specs = [row3(512)] * 10
    out_shape = [jax.ShapeDtypeStruct((nb, rows, 512), BF16)] * 6 + [jax.ShapeDtypeStruct((nb, rows, 512), F32)] * 4
    if with_vt:
        vt_shape, vt_spec = _vt_out(nb, rows, tm, ATT_SUB)
        out_shape.append(vt_shape)
        out_specs.append(vt_spec)
    return pl.pallas_call(
        _in_cd_kernel,
        grid=(nb, rows // tm),
        in_specs=[row3(D_MODEL), mod, mod, const(gpre), const(w)],
        out_specs=out_specs,
        out_shape=out_shape,
        compiler_params=_params(2),
        name="in_cd",
    )(x, sh, sc, gpre, w)


def _ffn_kernel(o1_ref, o2_ref, wo1_ref, wo2_ref, gate_m_ref, gpost_m_ref,
                x_ref, sh_ref, sc_ref, gate_ref, gpre_ref, gpost_ref, prev_ref, wg_ref, wu_ref, cw_ref, cb_ref, wd_ref,
                xo_ref, conv_ref, gbuf_ref, *, tm):
    t = pl.program_id(1)
    o = _dot(o1_ref[0], wo1_ref[...]) + _dot(o2_ref[0], wo2_ref[...])
    x = x_ref[0] + gate_m_ref[0] * _rms(o, gpost_m_ref[...])
    h = (_rms(x, gpre_ref[...]) * (1.0 + sc_ref[0]) + sh_ref[0]).astype(BF16)

    @pl.when(t == 0)
    def _():
        gbuf_ref[0:8, :] = jnp.zeros((8, D_FF), F32)
        gbuf_ref[8 - (CONV_W - 1):8, :] = prev_ref[0]

    acc = jnp.zeros((tm, D_MODEL), F32)
    for c in range(N_FF_CHUNKS):
        cols = slice(c * FF_CHUNK, (c + 1) * FF_CHUNK)
        g = _dot(h, wg_ref[c])
        u = _dot(h, wu_ref[c])
        gbuf_ref[8:8 + tm, cols] = g
        g1 = gbuf_ref[7:7 + tm, cols]
        g2 = gbuf_ref[6:6 + tm, cols]
        cw = cw_ref[c]
        gc = cb_ref[c] + (cw[0:1] * g2 + cw[1:2] * g1 + cw[2:3] * g)
        a = (gc * jax.nn.sigmoid(gc)) * u
        acc = acc + _dot(a.astype(BF16), wd_ref[c])
    conv_ref[0] = gbuf_ref[tm + 6:tm + 8, :]
    gbuf_ref[0:8, :] = gbuf_ref[tm:tm + 8, :]
    xo_ref[0] = x + gate_ref[0] * _rms(acc, gpost_ref[...])


def _out_ffn(o1, o2, wo1, wo2, gate_m, gpost_m, x, sh, sc, gate, gpre, gpost, prev, P, tm):
    nb, rows, _ = x.shape
    row3 = lambda wd: pl.BlockSpec((1, tm, wd), lambda b, t: (b, t, 0))
    mod = pl.BlockSpec((1, 1, D_MODEL), lambda b, t: (b, 0, 0))
    const = lambda a: pl.BlockSpec(a.shape, lambda b, t: (0,) * a.ndim)
    conv = pl.BlockSpec((1, CONV_W - 1, D_FF), lambda b, t: (b, 0, 0))
    consts = [P["wg"], P["wu"], P["cw"], P["cb"], P["wd"]]
    resident = lambda a: pl.BlockSpec(a.shape, lambda b, t: (0,) * a.ndim, pipeline_mode=pl.Buffered(1))
    return pl.pallas_call(
        functools.partial(_ffn_kernel, tm=tm),
        grid=(nb, rows // tm),
        in_specs=[row3(512), row3(512), const(wo1), const(wo2), mod, const(gpost_m),
                  row3(D_MODEL), mod, mod, mod, const(gpre), const(gpost), conv] + [resident(a) for a in consts],
        out_specs=[row3(D_MODEL), conv],
        out_shape=[jax.ShapeDtypeStruct((nb, rows, D_MODEL), F32),
                   jax.ShapeDtypeStruct((nb, CONV_W - 1, D_FF), F32)],
        scratch_shapes=[pltpu.VMEM((tm + 8, D_FF), F32)],
        compiler_params=_params(2),
        name="out_ffn",
    )(o1, o2, wo1, wo2, gate_m, gpost_m, x, sh, sc, gate, gpre, gpost, prev, *consts)


def _rot_half_cols(w):
    half = ROPE_DIM // 2
    return jnp.concatenate([-w[..., half:], w[..., :half]], axis=-1)


def _prep_ab(w_in_ab, b_f, q_a_g, kv_a_g, w_uq, w_ukv, w_out_ab):
    z = lambda n: jnp.zeros((D_MODEL, n), F32)
    c_q, c_kv, k_r = w_in_ab[:, 0:256], w_in_ab[:, 256:384], w_in_ab[:, 384:416]
    q_b, k_b, v_b, f_b = w_in_ab[:, 416:928], w_in_ab[:, 928:1440], w_in_ab[:, 1440:1952], w_in_ab[:, 1952:1960]
    w_ab = jnp.concatenate([c_q, c_kv, k_r, z(LANES - ROPE_DIM), _rot_half_cols(k_r), z(LANES - ROPE_DIM),
                            q_b, k_b, v_b, f_b, z(LANES - N_HEADS)], axis=1).astype(BF16)
    w_ft = jnp.concatenate([f_b.T, jnp.zeros((N_HEADS, D_MODEL), F32)], axis=0).astype(BF16)
    uq = w_uq.reshape(Q_RANK, N_HEADS, NOPE_DIM + ROPE_DIM)
    nope, rope_w = uq[..., :NOPE_DIM], uq[..., NOPE_DIM:]
    zq = lambda n: jnp.zeros((Q_RANK, N_HEADS, n), F32)
    wq_main = jnp.concatenate([nope, rope_w, zq(LANES - NOPE_DIM - ROPE_DIM)], axis=-1)
    wq_rot = jnp.concatenate([zq(NOPE_DIM), _rot_half_cols(rope_w), zq(LANES - NOPE_DIM - ROPE_DIM)], axis=-1)
    wq = jnp.concatenate([wq_main.reshape(Q_RANK, -1), wq_rot.reshape(Q_RANK, -1)], axis=1).astype(BF16)
    ukv = w_ukv.reshape(KV_RANK, N_HEADS, 2 * HEAD_DIM)
    wk = jnp.concatenate([ukv[..., :NOPE_DIM], jnp.zeros((KV_RANK, N_HEADS, LANES - NOPE_DIM), F32)], axis=-1)
    place = np.zeros((LANES, N_HEADS, LANES), np.float32)
    for r in range(ROPE_DIM):
        place[r, :, NOPE_DIM + r] = 1.0
    wkv = jnp.concatenate([
        jnp.concatenate([wk.reshape(KV_RANK, -1), ukv[..., NOPE_DIM:].reshape(KV_RANK, -1)], axis=1),
        jnp.concatenate([jnp.asarray(place.reshape(LANES, -1)), jnp.zeros((LANES, 512), F32)], axis=1),
    ], axis=0).astype(BF16)
    return {
        "w_ab": w_ab, "w_ft": w_ft, "wq": wq, "wkv": wkv,
        "bf128": jnp.concatenate([b_f, jnp.zeros((LANES - N_HEADS,), F32)]).reshape(1, LANES),
        "bft": b_f.reshape(N_HEADS, 1),
        "gq": q_a_g.reshape(1, Q_RANK), "gkv": kv_a_g.reshape(1, KV_RANK),
        "wo1": w_out_ab[:512].astype(BF16), "wo2": w_out_ab[512:].astype(BF16),
    }


def _prep_ffn(w_gate, w_up, conv_w, conv_b, w_down):
    chunk_cols = lambda w: w.reshape(D_MODEL, N_FF_CHUNKS, FF_CHUNK).transpose(1, 0, 2).astype(BF16)
    return {
        "wg": chunk_cols(w_gate), "wu": chunk_cols(w_up),
        "cw": conv_w.reshape(CONV_W, N_FF_CHUNKS, FF_CHUNK).transpose(1, 0, 2),
        "cb": conv_b.reshape(N_FF_CHUNKS, 1, FF_CHUNK),
        "wd": w_down.reshape(N_FF_CHUNKS, FF_CHUNK, D_MODEL).astype(BF16),
    }


def _rope_tables(pos):
    half = ROPE_DIM // 2
    inv = ROPE_BASE ** (-jnp.arange(half, dtype=F32) / half)
    ang = pos.astype(F32)[:, None] * inv[None, :]
    cos2 = jnp.concatenate([jnp.cos(ang)] * 2, axis=1)
    sin2 = jnp.concatenate([jnp.sin(ang)] * 2, axis=1)
    n = pos.shape[0]
    scale = QK_SCALE_A
    zeros = lambda w: jnp.zeros((n, w), F32)
    tqc = scale * jnp.concatenate([jnp.ones((n, NOPE_DIM), F32), cos2, zeros(LANES - NOPE_DIM - ROPE_DIM)], axis=1)
    tqs = scale * jnp.concatenate([zeros(NOPE_DIM), sin2, zeros(LANES - NOPE_DIM - ROPE_DIM)], axis=1)
    tkc = jnp.concatenate([cos2, zeros(LANES - ROPE_DIM)], axis=1)
    tks = jnp.concatenate([sin2, zeros(LANES - ROPE_DIM)], axis=1)
    return tqc, tqs, tkc, tks


def _pad_rows(a, front, total):
    return jnp.pad(a, ((0, 0), (front, total - front - a.shape[1])) + ((0, 0),) * (a.ndim - 2))


def _trunk(x, mods, past, PA, PF, w_cd, wo_cd, rel, gains, tm, tq):
    nb, rows, _ = x.shape
    q_off = 0 if past is None else PAST_LEN
    pos = q_off + jnp.arange(rows)
    tabs = _rope_tables(pos)
    mix_pre_g, mix_post_g, ffn_pre_g, ffn_post_g = gains
    out = {}

    def mod6(l):
        m = mods[l].reshape(nb, 6, 1, D_MODEL)
        return [m[:, j] for j in range(6)]

    sh_m, sc_m, g_m, sh_f, sc_f, g_f = mod6(0)
    prompt = past is None
    (qcomb, ckv, krope, ckr, qb, kb16, vb16, kb32, vb32, logf, logf128, logft, *vbt) = _in_ab(
        x, sh_m, sc_m, mix_pre_g[0:1], PA, tabs, tm, prompt)
    out["a_ckv"], out["a_krope"], out["b_k"], out["b_v"], out["b_logf"] = ckv, krope, kb32, vb32, logf
    if prompt:
        tkk = rows
        ckr_all, kb_all, vb_all, lf_all, lft_all = ckr, kb16, vb16, logf128, logft
    else:
        tkk = -(-(PAST_LEN + rows) // ATT_BLOCK) * ATT_BLOCK
        cat = lambda p, n: _pad_rows(jnp.concatenate([p, n], axis=1), 0, tkk)
        ckr_past = jnp.concatenate([past["a_ckv"].astype(BF16),
                                    jnp.pad(past["a_krope"].astype(BF16), ((0, 0), (0, 0), (0, LANES - ROPE_DIM)))],
                                   axis=-1)
        ckr_all = cat(ckr_past, ckr)
        kb_all = cat(past["b_k"].reshape(nb, PAST_LEN, 512).astype(BF16), kb16)
        vb_all = cat(past["b_v"].reshape(nb, PAST_LEN, 512).astype(BF16), vb16)
        lf_all = cat(jnp.pad(past["b_logf"], ((0, 0), (0, 0), (0, LANES - N_HEADS))), logf128)
        lft_all = jnp.pad(jnp.concatenate([jnp.swapaxes(past["b_logf"], 1, 2), logft], axis=2),
                          ((0, 0), (0, 0), (0, tkk - PAST_LEN - rows)))
    if prompt:
        kcomb, vat = _kvup(ckr_all, PA["wkv"], ATT_BLOCK, True)
        _, cumt, ck_rep = _cum(lf_all, lft_all, True)
        o_a = _flash_t("mla", qcomb, kcomb, vat, tq=ATT_BLOCK)
        o_b = _flash_t("fox", qb, kb_all, vbt[0], cumt.reshape(nb, N_PAIRS, 2, rows), ck_rep, tq=ATT_BLOCK)
    else:
        kcomb, va = _kvup(ckr_all, PA["wkv"], tkk, False)
        cum, cumt = _cum(lf_all, lft_all, False)
        cq = cum[:, q_off:q_off + rows, :N_HEADS].reshape(nb, rows, N_PAIRS, 2).transpose(0, 2, 1, 3)
        nkb = tkk // ATT_BLOCK
        ckt = cumt.reshape(nb, N_PAIRS, 2, nkb, ATT_BLOCK).transpose(0, 1, 3, 2, 4)
        o_a = _flash("mla", qcomb, kcomb, va, tq=tq, q_off=q_off, n_valid=q_off + rows)
        o_b = _flash("fox", qb, kb_all, vb_all, cq, ckt, tq=tq, q_off=q_off, n_valid=q_off + rows)
    prev = jnp.zeros((nb, CONV_W - 1, D_FF), F32) if past is None else past["ffn_conv"][0]
    x, conv0 = _out_ffn(o_a, o_b, PA["wo1"], PA["wo2"], g_m, mix_post_g[0:1],
                        x, sh_f, sc_f, g_f, ffn_pre_g[0:1], ffn_post_g[0:1], prev, PF[0], tm)

    sh_m, sc_m, g_m, sh_f, sc_f, g_f = mod6(1)
    qc, kc16, vc16, qd, kd16, vd16, kc32, vc32, kd32, vd32, *vdt = _in_cd(x, sh_m, sc_m, mix_pre_g[1:2], w_cd, tm,
                                                                          prompt)
    out["d_k"], out["d_v"] = kd32, vd32
    if prompt:
        keep = min(BAND_CHUNKS * CHUNK, rows)
        out["c_k"], out["c_v"] = kc32[:, rows - keep:], vc32[:, rows - keep:]
        front = BAND_CHUNKS * CHUNK
        kc_all = _pad_rows(kc16, front, front + rows)
        vc_all = _pad_rows(vc16, front, front + rows)
        bias = _band_bias(rel, tq, front + tq, 0, -front)
        o_c = _band(qc, kc_all, vc_all, bias, tq=tq, n_sub=2, pad_rows=front)
        o_d = _sb_t(qd, kd16, vdt[0], tq=ATT_BLOCK)
    else:
        out["c_k"], out["c_v"] = kc32, vc32
        n_c = past["c_k"].shape[1]
        front = BAND_SAMPLE_WIN - n_c - rows
        kc_all = _pad_rows(jnp.concatenate([past["c_k"].reshape(nb, n_c, 512).astype(BF16), kc16], axis=1), front,
                           BAND_SAMPLE_WIN)
        vc_all = _pad_rows(jnp.concatenate([past["c_v"].reshape(nb, n_c, 512).astype(BF16), vc16], axis=1), front,
                           BAND_SAMPLE_WIN)
        bias = _band_bias(rel, rows, BAND_SAMPLE_WIN, q_off, PAST_LEN - n_c - front)
        o_c = _band(qc, kc_all, vc_all, bias, tq=rows, n_sub=1, pad_rows=0)
        kd_all = _pad_rows(jnp.concatenate([past["d_k"].reshape(nb, PAST_LEN, 512).astype(BF16), kd16], axis=1), 0, tkk)
        vd_all = _pad_rows(jnp.concatenate([past["d_v"].reshape(nb, PAST_LEN, 512).astype(BF16), vd16], axis=1), 0, tkk)
        o_d = _sb(qd, kd_all, vd_all, tq=tq, q_off=q_off)
    prev = jnp.zeros((nb, CONV_W - 1, D_FF), F32) if past is None else past["ffn_conv"][1]
    x, conv1 = _out_ffn(o_c, o_d, wo_cd[0], wo_cd[1], g_m, mix_post_g[1:2],
                        x, sh_f, sc_f, g_f, ffn_pre_g[1:2], ffn_post_g[1:2], prev, PF[1], tm)
    out["ffn_conv"] = jnp.stack([conv0, conv1])
    return x, out


def kernel(x_prompt, x_sample, c_prompt, c_sample, cache_a_ckv, cache_a_krope, cache_b_k, cache_b_v, cache_b_logf, cache_c_k, cache_c_v, cache_d_k, cache_d_v, state_ffn_conv, ada_w, ada_b, mix_pre_g, mix_post_g, ffn_pre_g, ffn_post_g, w_in_ab, b_f, q_a_g, kv_a_g, w_uq, w_ukv, w_out_ab, w_in_cd, rel_bias_c, w_out_cd, ffn_w_gate, ffn_w_up, ffn_conv_w, ffn_conv_b, ffn_w_down):
    nbp, nbs = x_prompt.shape[0], x_sample.shape[0]
    c_all = jnp.concatenate([c_prompt, c_sample, jnp.zeros((32 - nbp - nbs, D_MODEL), F32)], axis=0)
    mods = _ada(c_all, ada_w, ada_b)
    PA = _prep_ab(w_in_ab[0], b_f[0], q_a_g[0], kv_a_g[0], w_uq[0], w_ukv[0], w_out_ab[0])
    PF = [_prep_ffn(ffn_w_gate[l], ffn_w_up[l], ffn_conv_w[l], ffn_conv_b[l], ffn_w_down[l]) for l in range(DEPTH)]
    w_cd = w_in_cd[0].astype(BF16)
    wo_cd = (w_out_cd[0][:512].astype(BF16), w_out_cd[0][512:].astype(BF16))
    gains = (mix_pre_g, mix_post_g, ffn_pre_g, ffn_post_g)
    past = {"a_ckv": cache_a_ckv[0], "a_krope": cache_a_krope[0], "b_k": cache_b_k[0], "b_v": cache_b_v[0],
            "b_logf": cache_b_logf[0], "c_k": cache_c_k[0], "c_v": cache_c_v[0], "d_k": cache_d_k[0],
            "d_v": cache_d_v[0], "ffn_conv": state_ffn_conv}
    y_p, sp = _trunk(x_prompt, mods[:, :nbp], None, PA, PF, w_cd, wo_cd, rel_bias_c[0], gains, 512, ATT_SUB)
    y_s, ss = _trunk(x_sample, mods[:, nbp:nbp + nbs], past, PA, PF, w_cd, wo_cd, rel_bias_c[0], gains,
                     x_sample.shape[1], x_sample.shape[1])

    def heads(a):
        return a.reshape(a.shape[0], a.shape[1], N_HEADS, HEAD_DIM)[None]

    return (y_p, y_s,
            sp["a_ckv"][None], ss["a_ckv"][None], sp["a_krope"][None], ss["a_krope"][None],
            heads(sp["b_k"]), heads(ss["b_k"]), heads(sp["b_v"]), heads(ss["b_v"]),
            sp["b_logf"][None], ss["b_logf"][None],
            heads(sp["c_k"]), heads(ss["c_k"]), heads(sp["c_v"]), heads(ss["c_v"]),
            heads(sp["d_k"]), heads(ss["d_k"]), heads(sp["d_v"]), heads(ss["d_v"]),
            sp["ffn_conv"], ss["ffn_conv"])
```

```python
import functools

import jax
import jax.numpy as jnp
import numpy as np
from jax import lax
from jax.experimental import pallas as pl
from jax.experimental.pallas import tpu as pltpu

D_MODEL = 1024
DEPTH = 2
PAST_LEN = 1024
CHUNK = 64
HEAD_DIM = 64
N_HEADS = 8
N_PAIRS = N_HEADS // 2
Q_RANK = 256
KV_RANK = 128
NOPE_DIM = 64
ROPE_DIM = 32
ROPE_BASE = 10000.0
BAND_CHUNKS = 8
REL_CLIP = 128
D_FF = 2816
CONV_W = 3
EPS = 1e-6
NEG_INF = -1e30
LOG2E = 1.4426950408889634
QK_SCALE_64 = HEAD_DIM ** -0.5 * LOG2E
QK_SCALE_A = (NOPE_DIM + ROPE_DIM) ** -0.5 * LOG2E
SB_UNDERFLOW_BITS = 160.0
SB_PAIRS = 2
SB_CUM = 128
ONES_ROWS = 16

LANES = 128
FF_CHUNK = 2816
N_FF_CHUNKS = D_FF // FF_CHUNK
ATT_BLOCK = 512
ATT_SUB = 256
BAND_SAMPLE_WIN = 640
VMEM_LIMIT_BYTES = 56 * 1024 * 1024

F32 = jnp.float32
BF16 = jnp.bfloat16

AB_CQ, AB_CKV, AB_KR, AB_KRROT, AB_QB, AB_KB, AB_VB, AB_FB, AB_COLS = 0, 256, 384, 512, 640, 1152, 1664, 2176, 2304


def _params(n_axes):
    return pltpu.CompilerParams(dimension_semantics=("arbitrary",) * n_axes, vmem_limit_bytes=VMEM_LIMIT_BYTES)


def _rms(x, g):
    return x * lax.rsqrt(jnp.mean(x * x, axis=-1, keepdims=True) + EPS) * g


def _log_sigmoid(x):
    return jnp.minimum(x, 0.0) - jnp.log1p(jnp.exp(-jnp.abs(x)))


def _log2_sigmoids(z2):
    l2 = jnp.log2(1.0 + jnp.exp2(-jnp.abs(z2)))
    return jnp.minimum(z2, 0.0) - l2, jnp.maximum(z2, 0.0) + l2


def _dot(a, b):
    return jnp.dot(a, b, preferred_element_type=F32)


def _dot_nt(a, b):
    return lax.dot_general(a, b, (((1,), (1,)), ((), ())), preferred_element_type=F32)


def _split_bf16(x, n):
    parts = []
    for _ in range(n):
        p = x.astype(BF16)
        parts.append(p)
        x = x - p.astype(F32)
    return parts


def _tile_lanes(x, n):
    return x if n == 1 else jnp.concatenate([x] * n, axis=1)


def _ada_kernel(c_ref, w_ref, b_ref, o_ref):
    c = c_ref[...]
    cond = c * jax.nn.sigmoid(c)
    o_ref[0] = _dot(cond.astype(BF16), w_ref[0].astype(BF16)) + b_ref[0]


def _ada(c_all, ada_w, ada_b):
    rows = c_all.shape[0]
    tn = 1536
    return pl.pallas_call(
        _ada_kernel,
        grid=(DEPTH, 6 * D_MODEL // tn),
        in_specs=[
            pl.BlockSpec((rows, D_MODEL), lambda l, j: (0, 0)),
            pl.BlockSpec((1, D_MODEL, tn), lambda l, j: (l, 0, j)),
            pl.BlockSpec((1, 1, tn), lambda l, j: (l, 0, j)),
        ],
        out_specs=pl.BlockSpec((1, rows, tn), lambda l, j: (l, 0, j)),
        out_shape=jax.ShapeDtypeStruct((DEPTH, rows, 6 * D_MODEL), F32),
        compiler_params=_params(2),
        name="ada",
    )(c_all, ada_w, ada_b.reshape(DEPTH, 1, 6 * D_MODEL))


def _store_vt(vt_ref, v):
    vt = v.T
    blk = vt_ref.shape[-1]
    for p in range(N_PAIRS):
        for c in range(vt_ref.shape[2]):
            vt_ref[0, p, c] = vt[p * LANES:(p + 1) * LANES, c * blk:(c + 1) * blk].astype(BF16)


def _in_ab_kernel(x_ref, sh_ref, sc_ref, gpre_ref, w_ref, wft_ref, bf_ref, bft_ref, gq_ref, gkv_ref, wq_ref,
                  tqc_ref, tqs_ref, tkc_ref, tks_ref,
                  qcomb_ref, ckv_ref, krope_ref, ckr_ref, qb_ref, kb16_ref, vb16_ref, kb32_ref, vb32_ref,
                  logf_ref, logf128_ref, logft_ref, vbt_ref=None):
    x = x_ref[0]
    h = _rms(x, gpre_ref[...]) * (1.0 + sc_ref[0]) + sh_ref[0]
    hb = h.astype(BF16)
    y = _dot(hb, w_ref[...])
    cq = _rms(y[:, AB_CQ:AB_CQ + Q_RANK], gq_ref[...]).astype(BF16)
    qa = _dot(cq, wq_ref[...])
    tqc = tqc_ref[...]
    tqs = tqs_ref[...]
    half = N_HEADS * LANES
    for hh in range(N_HEADS):
        lo = hh * LANES
        qcomb_ref[0, :, lo:lo + LANES] = (qa[:, lo:lo + LANES] * tqc + qa[:, half + lo:half + lo + LANES] * tqs).astype(BF16)
    ckv = _rms(y[:, AB_CKV:AB_CKV + KV_RANK], gkv_ref[...])
    ckv_ref[0] = ckv
    kr = y[:, AB_KR:AB_KR + LANES] * tkc_ref[...] + y[:, AB_KRROT:AB_KRROT + LANES] * tks_ref[...]
    krope_ref[0] = kr[:, :ROPE_DIM]
    ckr_ref[0, :, 0:LANES] = ckv.astype(BF16)
    ckr_ref[0, :, LANES:2 * LANES] = kr.astype(BF16)
    qb_ref[0] = (y[:, AB_QB:AB_QB + 512] * QK_SCALE_64).astype(BF16)
    kb = y[:, AB_KB:AB_KB + 512]
    vb = y[:, AB_VB:AB_VB + 512]
    kb32_ref[0] = kb
    vb32_ref[0] = vb
    kb16_ref[0] = kb.astype(BF16)
    vb16_ref[0] = vb.astype(BF16)
    if vbt_ref is not None:
        _store_vt(vbt_ref, vb)
    logf = _log_sigmoid(y[:, AB_FB:AB_FB + LANES] + bf_ref[...])
    logf128_ref[0] = logf
    logf_ref[0] = logf[:, :N_HEADS]
    ft = _dot_nt(wft_ref[...], hb)
    logft_ref[0] = _log_sigmoid(ft[:N_HEADS] + bft_ref[...])


def _vt_out(nb, rows, tm, blk=None):
    blk = tm if blk is None else blk
    return (jax.ShapeDtypeStruct((nb, N_PAIRS, rows // blk, LANES, blk), BF16),
            pl.BlockSpec((1, N_PAIRS, tm // blk, LANES, blk), lambda b, t: (b, 0, t, 0, 0)))


def _in_ab(x, sh, sc, gpre, P, tabs, tm, with_vt):
    nb, rows, _ = x.shape
    nt = rows // tm
    row3 = lambda w: pl.BlockSpec((1, tm, w), lambda b, t: (b, t, 0))
    mod = pl.BlockSpec((1, 1, D_MODEL), lambda b, t: (b, 0, 0))
    const = lambda a: pl.BlockSpec(a.shape, lambda b, t: (0,) * a.ndim)
    tab = pl.BlockSpec((tm, LANES), lambda b, t: (t, 0))
    out_shapes = [
        jax.ShapeDtypeStruct((nb, rows, N_HEADS * LANES), BF16),
        jax.ShapeDtypeStruct((nb, rows, KV_RANK), F32),
        jax.ShapeDtypeStruct((nb, rows, ROPE_DIM), F32),
        jax.ShapeDtypeStruct((nb, rows, 2 * LANES), BF16),
        jax.ShapeDtypeStruct((nb, rows, 512), BF16),
        jax.ShapeDtypeStruct((nb, rows, 512), BF16),
        jax.ShapeDtypeStruct((nb, rows, 512), BF16),
        jax.ShapeDtypeStruct((nb, rows, 512), F32),
        jax.ShapeDtypeStruct((nb, rows, 512), F32),
        jax.ShapeDtypeStruct((nb, rows, N_HEADS), F32),
        jax.ShapeDtypeStruct((nb, rows, LANES), F32),
        jax.ShapeDtypeStruct((nb, N_HEADS, rows), F32),
    ]
    out_specs = [row3(N_HEADS * LANES), row3(KV_RANK), row3(ROPE_DIM), row3(2 * LANES), row3(512), row3(512),
                 row3(512), row3(512), row3(512), row3(N_HEADS), row3(LANES),
                 pl.BlockSpec((1, N_HEADS, tm), lambda b, t: (b, 0, t))]
    if with_vt:
        vt_shape, vt_spec = _vt_out(nb, rows, tm)
        out_shapes.append(vt_shape)
        out_specs.append(vt_spec)
    consts = [P["w_ab"], P["w_ft"], P["bf128"], P["bft"], P["gq"], P["gkv"], P["wq"]]
    return pl.pallas_call(
        _in_ab_kernel,
        grid=(nb, nt),
        in_specs=[row3(D_MODEL), mod, mod, const(gpre)] + [const(a) for a in consts] + [tab] * 4,
        out_specs=out_specs,
        out_shape=out_shapes,
        compiler_params=_params(2),
        name="in_ab",
    )(x, sh, sc, gpre, *consts, *tabs)


def _kvup_kernel(ckr_ref, w_ref, k_ref, v_ref):
    y = _dot(ckr_ref[0], w_ref[...])
    k_ref[0] = y[:, :N_HEADS * LANES].astype(BF16)
    v_ref[0] = y[:, N_HEADS * LANES:].astype(BF16)


def _kvup_t_kernel(ckr_ref, w_ref, k_ref, vt_ref):
    y = _dot(ckr_ref[0], w_ref[...])
    k_ref[0] = y[:, :N_HEADS * LANES].astype(BF16)
    _store_vt(vt_ref, y[:, N_HEADS * LANES:])


def _kvup(ckr, w, tm, with_vt):
    nb, rows, _ = ckr.shape
    if with_vt:
        v_shape, v_spec = _vt_out(nb, rows, tm)
    else:
        v_shape = jax.ShapeDtypeStruct((nb, rows, 512), BF16)
        v_spec = pl.BlockSpec((1, tm, 512), lambda b, t: (b, t, 0))
    return pl.pallas_call(
        _kvup_t_kernel if with_vt else _kvup_kernel,
        grid=(nb, rows // tm),
        in_specs=[pl.BlockSpec((1, tm, 2 * LANES), lambda b, t: (b, t, 0)),
                  pl.BlockSpec(w.shape, lambda b, t: (0, 0))],
        out_specs=[pl.BlockSpec((1, tm, N_HEADS * LANES), lambda b, t: (b, t, 0)), v_spec],
        out_shape=[jax.ShapeDtypeStruct((nb, rows, N_HEADS * LANES), BF16), v_shape],
        compiler_params=_params(2),
        name="kvup",
    )(ckr, w)


def _cum_kernel(x_ref, xt_ref, c_ref, ct_ref, rep_ref=None, *, n_chunks, tc):
    r = lax.broadcasted_iota(jnp.int32, (tc, tc), 0)
    c = lax.broadcasted_iota(jnp.int32, (tc, tc), 1)
    lower = jnp.where(c <= r, 1.0, 0.0).astype(BF16)
    upper = jnp.where(r <= c, 1.0, 0.0).astype(BF16)
    carry = jnp.zeros((1, LANES), F32)
    carry_t = jnp.zeros((N_HEADS, 1), F32)
    for ci in range(n_chunks):
        sl = slice(ci * tc, (ci + 1) * tc)
        cs = carry
        for p in _split_bf16(x_ref[0, sl, :], 3):
            cs = cs + _dot(lower, p)
        c_ref[0, sl, :] = cs * LOG2E
        if rep_ref is not None:
            for h in range(N_HEADS):
                rep_ref[0, h, sl, :] = jnp.broadcast_to(cs[:, h:h + 1] * LOG2E, (tc, LANES))
        carry = cs[tc - 1:tc, :]
        xt = jnp.concatenate([xt_ref[0, :, sl], jnp.zeros((N_HEADS, tc), F32)], axis=0)
        cst = jnp.zeros((2 * N_HEADS, tc), F32)
        for p in _split_bf16(xt, 3):
            cst = cst + _dot(p, upper)
        cst = cst[:N_HEADS] + carry_t
        ct_ref[0, :, sl] = cst * LOG2E
        carry_t = cst[:, tc - 1:tc]


def _cum(logf128, logft, with_rep):
    nb, tk, _ = logf128.shape
    tc = ATT_SUB
    out_specs = [pl.BlockSpec((1, tk, LANES), lambda b: (b, 0, 0)),
                 pl.BlockSpec((1, N_HEADS, tk), lambda b: (b, 0, 0))]
    out_shape = [jax.ShapeDtypeStruct((nb, tk, LANES), F32), jax.ShapeDtypeStruct((nb, N_HEADS, tk), F32)]
    if with_rep:
        out_specs.append(pl.BlockSpec((1, N_HEADS, tk, LANES), lambda b: (b, 0, 0, 0)))
        out_shape.append(jax.ShapeDtypeStruct((nb, N_HEADS, tk, LANES), F32))
    return pl.pallas_call(
        functools.partial(_cum_kernel, n_chunks=tk // tc, tc=tc),
        grid=(nb,),
        in_specs=[pl.BlockSpec((1, tk, LANES), lambda b: (b, 0, 0)),
                  pl.BlockSpec((1, N_HEADS, tk), lambda b: (b, 0, 0))],
        out_specs=out_specs,
        out_shape=out_shape,
        compiler_params=_params(1),
        name="cum",
    )(logf128, logft)


def _flash_kernel(*refs, mode, tq, tk, q_off, n_valid, tail_widths):
    if mode == "fox":
        q_ref, k_ref, v_ref, cq_ref, ckt_ref, o_ref, m_s, l_s, acc_s = refs
    else:
        q_ref, k_ref, v_ref, o_ref, m_s, l_s, acc_s = refs
    i = pl.program_id(2)
    qpos0 = q_off + i * tq
    n_full = qpos0 // tk
    rem = qpos0 + tq - n_full * tk
    lane = lax.broadcasted_iota(jnp.int32, (tq, LANES), 1)

    qs, kcols, cq_tiles = [], [], []
    for hp in range(N_HEADS):
        p, h = divmod(hp, 2)
        pair = slice(p * LANES, (p + 1) * LANES)
        if mode == "fox":
            q = q_ref[0, :, pair]
            qs.append(jnp.where((lane < HEAD_DIM) if h == 0 else (lane >= HEAD_DIM), q, jnp.zeros_like(q)))
            kcols.append(pair)
            cq_tiles.append(jnp.broadcast_to(cq_ref[0, p][:, h:h + 1], (tq, LANES)))
        else:
            qs.append(q_ref[0, :, hp * LANES:(hp + 1) * LANES])
            kcols.append(slice(hp * LANES, (hp + 1) * LANES))
        m_s[hp] = jnp.full((tq, LANES), NEG_INF, F32)
        l_s[hp] = jnp.zeros((tq, LANES), F32)
        acc_s[hp] = jnp.zeros((tq, LANES), F32)

    def step(j, width, masked):
        nrep = width // LANES
        r0 = pl.multiple_of(j * tk, tk)
        if masked:
            kpos = n_full * tk + lax.broadcasted_iota(jnp.int32, (tq, width), 1)
            qpos = qpos0 + lax.broadcasted_iota(jnp.int32, (tq, width), 0)
            if mode == "fox":
                mask = kpos <= qpos
            else:
                mask = (lax.shift_right_logical(kpos, 6) <= lax.shift_right_logical(qpos, 6)) & (kpos < n_valid)
        for hp in range(N_HEADS):
            p, h = divmod(hp, 2)
            s = _dot_nt(qs[hp], k_ref[0, pl.ds(r0, width), kcols[hp]])
            if mode == "fox":
                s = s + (_tile_lanes(cq_tiles[hp], nrep) - ckt_ref[0, p, j][h:h + 1, :width])
            if masked:
                s = jnp.where(mask, s, NEG_INF)
            m_prev = m_s[hp]
            m_next = jnp.maximum(m_prev, jnp.max(s, axis=1, keepdims=True))
            pr = jnp.exp2(s - _tile_lanes(m_next, nrep))
            alpha = jnp.exp2(m_prev - m_next)
            l_s[hp] = alpha * l_s[hp] + jnp.sum(pr, axis=1, keepdims=True)
            v = v_ref[0, pl.ds(r0, width), p * LANES:(p + 1) * LANES]
            acc_s[hp] = alpha * acc_s[hp] + _dot(pr.astype(BF16), v)
            m_s[hp] = m_next

    def body(j, carry):
        step(j, tk, False)
        return carry

    lax.fori_loop(0, n_full, body, 0)
    if len(tail_widths) == 1:
        step(n_full, tail_widths[0], True)
    else:
        lo_w, hi_w = tail_widths
        pl.when(rem <= lo_w)(lambda: step(n_full, lo_w, True))
        pl.when(rem > lo_w)(lambda: step(n_full, hi_w, True))

    for p in range(N_PAIRS):
        o0 = acc_s[2 * p] / l_s[2 * p]
        o1 = acc_s[2 * p + 1] / l_s[2 * p + 1]
        o_ref[0, :, p * LANES:(p + 1) * LANES] = jnp.where(lane < HEAD_DIM, o0, o1).astype(BF16)


def _tail_widths(tk):
    return (ATT_SUB, tk) if tk > ATT_SUB else (tk,)


def _flash(mode, q, k, v, cq=None, ckt=None, *, tq, q_off, n_valid):
    nb, rows, _ = q.shape
    tkk = k.shape[1]
    tk = ATT_BLOCK
    kw = q.shape[2]
    in_specs = [pl.BlockSpec((1, tq, kw), lambda b, p, i: (b, i, 0)),
                pl.BlockSpec((1, tkk, kw), lambda b, p, i: (b, 0, 0)),
                pl.BlockSpec((1, tkk, 512), lambda b, p, i: (b, 0, 0))]
    args = [q, k, v]
    if mode == "fox":
        in_specs += [pl.BlockSpec((1, N_PAIRS, tq, 2), lambda b, p, i: (b, 0, i, 0)),
                     pl.BlockSpec((1, N_PAIRS, tkk // tk, 2, tk), lambda b, p, i: (b, 0, 0, 0, 0))]
        args += [cq, ckt]
    return pl.pallas_call(
        functools.partial(_flash_kernel, mode=mode, tq=tq, tk=tk, q_off=q_off, n_valid=n_valid,
                          tail_widths=_tail_widths(tk)),
        grid=(nb, 1, rows // tq),
        in_specs=in_specs,
        out_specs=pl.BlockSpec((1, tq, 512), lambda b, p, i: (b, i, 0)),
        out_shape=jax.ShapeDtypeStruct((nb, rows, 512), BF16),
        scratch_shapes=[pltpu.VMEM((N_HEADS, tq, LANES), F32)] * 3,
        compiler_params=_params(3),
        name="attn_" + mode,
    )(*args)


def _sb_kernel(q_ref, k_ref, v_ref, o_ref, acc_s, r_s, u_s, *, tq, tk, q_off, tail_widths):
    i = pl.program_id(2)
    qpos0 = q_off + i * tq
    n_full = qpos0 // tk
    rem = qpos0 + tq - n_full * tk
    nrep = ATT_SUB // LANES
    lane = lax.broadcasted_iota(jnp.int32, (tq, LANES), 1)
    ur = lax.broadcasted_iota(jnp.int32, (ATT_SUB, ATT_SUB), 0)
    uc = lax.broadcasted_iota(jnp.int32, (ATT_SUB, ATT_SUB), 1)
    u_s[...] = jnp.where(ur > uc, 1.0, 0.0).astype(BF16)
    qs = []
    for hp in range(N_HEADS):
        p, h = divmod(hp, 2)
        q = q_ref[0, :, p * LANES:(p + 1) * LANES]
        qs.append(jnp.where((lane < HEAD_DIM) if h == 0 else (lane >= HEAD_DIM), q, jnp.zeros_like(q)))
    acc_s[...] = jnp.zeros((N_HEADS, tq, LANES), F32)
    r_s[...] = jnp.zeros((N_HEADS, tq, LANES), F32)

    def step(j, width, masked):
        r0 = pl.multiple_of(j * tk, tk)
        if masked:
            kpos = n_full * tk + lax.broadcasted_iota(jnp.int32, (tq, width), 1)
            qpos = qpos0 + lax.broadcasted_iota(jnp.int32, (tq, width), 0)
            mask = kpos < qpos
        u = u_s[...]
        for h in range(N_HEADS):
            pair = slice(h // 2 * LANES, (h // 2 + 1) * LANES)
            k = k_ref[0, pl.ds(r0, width), pair]
            v = v_ref[0, pl.ds(r0, width), pair]
            z = _dot_nt(qs[h], k)
            lsz, nlk = _log2_sigmoids(z)
            if masked:
                nlk = jnp.where(mask, nlk, 0.0)
            r = r_s[h]
            ws = []
            for sb in reversed(range(width // ATT_SUB)):
                cols = slice(sb * ATT_SUB, (sb + 1) * ATT_SUB)
                nlk_sb = nlk[:, cols]
                hi, lo = _split_bf16(nlk_sb, 2)
                after = _dot(hi, u) + _dot(lo, u) + _tile_lanes(r, nrep)
                w = jnp.exp2(lsz[:, cols] - after)
                if masked:
                    w = jnp.where(mask[:, cols], w, 0.0)
                ws.append(w.astype(BF16))
                r = r + jnp.sum(nlk_sb, axis=1, keepdims=True)
            w_all = ws[0] if len(ws) == 1 else jnp.concatenate(ws[::-1], axis=1)
            acc_s[h] = acc_s[h] + _dot(w_all, v)
            r_s[h] = r

    if len(tail_widths) == 1:
        step(n_full, tail_widths[0], True)
    else:
        lo_w, hi_w = tail_widths
        pl.when(rem <= lo_w)(lambda: step(n_full, lo_w, True))
        pl.when(rem > lo_w)(lambda: step(n_full, hi_w, True))

    def body(jj, carry):
        step(n_full - 1 - jj, tk, False)
        return carry

    lax.fori_loop(0, n_full, body, 0)
    for p in range(N_PAIRS):
        o_ref[0, :, p * LANES:(p + 1) * LANES] = jnp.where(lane < HEAD_DIM, acc_s[2 * p], acc_s[2 * p + 1]).astype(BF16)


def _sb(q, k, v, *, tq, q_off):
    nb, rows, _ = q.shape
    tkk = k.shape[1]
    tk = ATT_BLOCK
    return pl.pallas_call(
        functools.partial(_sb_kernel, tq=tq, tk=tk, q_off=q_off, tail_widths=_tail_widths(tk)),
        grid=(nb, 1, rows // tq),
        in_specs=[pl.BlockSpec((1, tq, 512), lambda b, p, i: (b, i, 0)),
                  pl.BlockSpec((1, tkk, 512), lambda b, p, i: (b, 0, 0)),
                  pl.BlockSpec((1, tkk, 512), lambda b, p, i: (b, 0, 0))],
        out_specs=pl.BlockSpec((1, tq, 512), lambda b, p, i: (b, i, 0)),
        out_shape=jax.ShapeDtypeStruct((nb, rows, 512), BF16),
        scratch_shapes=[pltpu.VMEM((N_HEADS, tq, LANES), F32), pltpu.VMEM((N_HEADS, tq, LANES), F32),
                        pltpu.VMEM((ATT_SUB, ATT_SUB), BF16)],
        compiler_params=_params(3),
        name="attn_sb",
    )(q, k, v)


def _pair_queries(q_ref, mode, tq):
    lane = lax.broadcasted_iota(jnp.int32, (tq, LANES), 1)
    if mode == "mla":
        return [q_ref[0, :, h * LANES:(h + 1) * LANES] for h in range(2)], [slice(h * LANES, (h + 1) * LANES) for h in range(2)]
    q = q_ref[0]
    qs = [jnp.where((lane < HEAD_DIM) if h == 0 else (lane >= HEAD_DIM), q, jnp.zeros_like(q)) for h in range(2)]
    return qs, [slice(0, LANES)] * 2


def _store_heads_t(o_ref, o0, o1):
    o_ref[0] = jnp.concatenate([o0, o1], axis=0).T.astype(BF16)


def _flash_t_kernel(*refs, mode, tq, tk):
    if mode == "fox":
        q_ref, k_ref, vt_ref, cq_ref, ck_ref, o_ref, m_s, acc_s, s_buf = refs
    else:
        q_ref, k_ref, vt_ref, o_ref, m_s, acc_s, s_buf = refs
    n_full = pl.program_id(2)
    qs, kcols = _pair_queries(q_ref, mode, tq)
    m_s[...] = jnp.full((2, 1, tq), NEG_INF, F32)
    acc_s[...] = jnp.zeros((2, HEAD_DIM + ONES_ROWS, tq), F32)
    ones = jnp.ones((ONES_ROWS, tk), BF16)

    def scores(j, slot):
        r0 = pl.multiple_of(j * tk, tk)
        for h in range(2):
            s = _dot_nt(k_ref[0, pl.ds(r0, tk), kcols[h]], qs[h])
            if mode == "fox":
                s = s - _tile_lanes(ck_ref[0, h, pl.ds(r0, tk), :], tq // LANES)
            s_buf[slot, h] = s

    def update(j, slot, h, n_keys, q_lo, q_hi, masked):
        s = s_buf[slot, h, 0:n_keys, q_lo:q_hi]
        if masked:
            kpos = lax.broadcasted_iota(jnp.int32, s.shape, 0)
            qpos = q_lo + lax.broadcasted_iota(jnp.int32, s.shape, 1)
            if mode == "fox":
                mask = kpos <= qpos
            else:
                mask = lax.shift_right_logical(kpos, 6) <= lax.shift_right_logical(qpos, 6)
            s = jnp.where(mask, s, NEG_INF)
        m_prev = m_s[h, :, q_lo:q_hi]
        if mode == "fox":
            cq = cq_ref[0, 0, h:h + 1, q_lo:q_hi]
            m_next = jnp.maximum(m_prev, jnp.max(s, axis=0, keepdims=True) + cq)
            p = jnp.exp2(s - (m_next - cq))
        else:
            m_next = jnp.maximum(m_prev, jnp.max(s, axis=0, keepdims=True))
            p = jnp.exp2(s - m_next)
        alpha = jnp.exp2(m_prev - m_next)
        vt = jnp.concatenate([vt_ref[0, 0, j, h * HEAD_DIM:(h + 1) * HEAD_DIM, 0:n_keys], ones[:, 0:n_keys]], axis=0)
        acc_s[h, :, q_lo:q_hi] = alpha * acc_s[h, :, q_lo:q_hi] + _dot(vt, p.astype(BF16))
        m_s[h, :, q_lo:q_hi] = m_next

    def consume(j, slot, masked):
        for h in range(2):
            if masked:
                update(j, slot, h, tk // 2, 0, tq // 2, True)
                update(j, slot, h, tk, tq // 2, tq, True)
            else:
                update(j, slot, h, tk, 0, tq, False)

    scores(0, 0)

    def body(jj, carry):
        j = 2 * jj
        scores(j + 1, 1)
        consume(j, 0, False)
        scores(j + 2, 0)
        consume(j + 1, 1, False)
        return carry

    lax.fori_loop(0, n_full // 2, body, 0)

    @pl.when(n_full % 2 == 1)
    def _():
        scores(n_full, 1)
        consume(n_full - 1, 0, False)
        consume(n_full, 1, True)

    @pl.when(n_full % 2 == 0)
    def _():
        consume(n_full, 0, True)

    _store_heads_t(o_ref, *[acc_s[h, :HEAD_DIM] / acc_s[h, HEAD_DIM:HEAD_DIM + 1] for h in range(2)])


def _flash_t(mode, q, k, vt, cq=None, ck=None, *, tq):
    nb, rows, _ = q.shape
    tk = ATT_BLOCK
    assert tq == tk
    nkb = rows // tk
    kw = 2 * LANES if mode == "mla" else LANES
    in_specs = [pl.BlockSpec((1, tq, kw), lambda b, p, i: (b, i, p)),
                pl.BlockSpec((1, rows, kw), lambda b, p, i: (b, 0, p)),
                pl.BlockSpec((1, 1, nkb, LANES, tk), lambda b, p, i: (b, p, 0, 0, 0))]
    args = [q, k, vt]
    if mode == "fox":
        in_specs += [pl.BlockSpec((1, 1, 2, tq), lambda b, p, i: (b, p, 0, i)),
                     pl.BlockSpec((1, 2, rows, LANES), lambda b, p, i: (b, p, 0, 0))]
        args += [cq, ck]
    return pl.pallas_call(
        functools.partial(_flash_t_kernel, mode=mode, tq=tq, tk=tk),
        grid=(nb, N_PAIRS, rows // tq),
        in_specs=in_specs,
        out_specs=pl.BlockSpec((1, tq, LANES), lambda b, p, i: (b, i, p)),
        out_shape=jax.ShapeDtypeStruct((nb, rows, 512), BF16),
        scratch_shapes=[pltpu.VMEM((2, 1, tq), F32), pltpu.VMEM((2, HEAD_DIM + ONES_ROWS, tq), F32),
                        pltpu.VMEM((2, 2, tk, tq), F32)],
        compiler_params=_params(3),
        name="attn_" + mode,
    )(*args)


def _sb_t_kernel(q_ref, k_ref, vt_ref, o_ref, acc_s, r_s, ut_s, *, tq, tk):
    i = pl.program_id(2)
    n_diag = tq // tk
    nk = (i + 1) * n_diag
    ur = lax.broadcasted_iota(jnp.int32, (SB_CUM, 2 * SB_CUM), 0)
    uc = lax.broadcasted_iota(jnp.int32, (SB_CUM, 2 * SB_CUM), 1) & (SB_CUM - 1)
    ut_s[...] = jnp.where(uc > ur, 1.0, 0.0).astype(BF16)
    n_heads = 2 * SB_PAIRS
    lane = lax.broadcasted_iota(jnp.int32, (tq, LANES), 1)
    qs = []
    for hp in range(n_heads):
        q = q_ref[0, :, hp // 2 * LANES:(hp // 2 + 1) * LANES]
        qs.append(jnp.where((lane < HEAD_DIM) if hp % 2 == 0 else (lane >= HEAD_DIM), q, jnp.zeros_like(q)))
    acc_s[...] = jnp.zeros((n_heads, HEAD_DIM, tq), F32)
    r_s[...] = jnp.zeros((n_heads, 1, tq), F32)

    def step(j, masked, q_lo=0):
        nq = tq - q_lo
        r0 = pl.multiple_of(j * tk, tk)
        if masked:
            kpos = j * tk + lax.broadcasted_iota(jnp.int32, (tk, nq), 0)
            qpos = i * tq + q_lo + lax.broadcasted_iota(jnp.int32, (tk, nq), 1)
            mask = kpos < qpos
        ut = ut_s[...]
        for h in range(n_heads):
            k = k_ref[0, pl.ds(r0, tk), h // 2 * LANES:(h // 2 + 1) * LANES]
            z = _dot_nt(k, qs[h][q_lo:])
            lsz = jnp.minimum(z, 0.0) - jnp.log2(1.0 + jnp.exp2(-jnp.abs(z)))
            nlk = z - lsz
            if masked:
                nlk = jnp.where(mask, nlk, 0.0)
            r = r_s[h, :, q_lo:]
            ws = [None] * (tk // SB_CUM)
            for sb in reversed(range(tk // SB_CUM)):
                rows = slice(sb * SB_CUM, (sb + 1) * SB_CUM)
                hi, lo = _split_bf16(nlk[rows], 2)
                w = jnp.exp2(lsz[rows] - _dot(ut, jnp.concatenate([hi, lo], axis=0)) - r)
                if masked:
                    w = jnp.where(mask[rows], w, 0.0)
                ws[sb] = w.astype(BF16)
                r = r + jnp.sum(nlk[rows], axis=0, keepdims=True)
            vt = vt_ref[0, h // 2, j, h % 2 * HEAD_DIM:(h % 2 + 1) * HEAD_DIM, :]
            acc_s[h, :, q_lo:] = acc_s[h, :, q_lo:] + _dot(vt, jnp.concatenate(ws, axis=0))
            r_s[h, :, q_lo:] = r

    def diagonal():
        for d in range(n_diag):
            step(nk - 1 - d, True, q_lo=(n_diag - 1 - d) * tk)

    @pl.when(i == 0)
    def _():
        diagonal()

    @pl.when(i > 0)
    def _():
        diagonal()
        step(nk - 1 - n_diag, False)

    def r_min():
        return functools.reduce(jnp.minimum, [jnp.min(r_s[h]) for h in range(n_heads)])

    def cond(c):
        j, rmin = c
        return (j >= 0) & (rmin <= SB_UNDERFLOW_BITS)

    def body(c):
        j, _ = c
        step(j, False)
        return j - 1, r_min()

    lax.while_loop(cond, body, (nk - 1 - n_diag - jnp.minimum(i, 1), r_min()))
    for p in range(SB_PAIRS):
        pair = jnp.concatenate([acc_s[2 * p], acc_s[2 * p + 1]], axis=0)
        o_ref[0, :, p * LANES:(p + 1) * LANES] = pair.T.astype(BF16)


def _sb_t(q, k, vt, *, tq):
    nb, rows, _ = q.shape
    tk = vt.shape[-1]
    w = SB_PAIRS * LANES
    return pl.pallas_call(
        functools.partial(_sb_t_kernel, tq=tq, tk=tk),
        grid=(nb, N_PAIRS // SB_PAIRS, rows // tq),
        in_specs=[pl.BlockSpec((1, tq, w), lambda b, p, i: (b, i, p)),
                  pl.BlockSpec((1, rows, w), lambda b, p, i: (b, 0, p)),
                  pl.BlockSpec((1, SB_PAIRS, rows // tk, LANES, tk), lambda b, p, i: (b, p, 0, 0, 0))],
        out_specs=pl.BlockSpec((1, tq, w), lambda b, p, i: (b, i, p)),
        out_shape=jax.ShapeDtypeStruct((nb, rows, 512), BF16),
        scratch_shapes=[pltpu.VMEM((2 * SB_PAIRS, HEAD_DIM, tq), F32), pltpu.VMEM((2 * SB_PAIRS, 1, tq), F32),
                        pltpu.VMEM((SB_CUM, 2 * SB_CUM), BF16)],
        compiler_params=_params(3),
        name="attn_sb",
    )(q, k, vt)


def _bias_kernel(tab_ref, o_ref, *, nq, nk, q0, k0):
    h = pl.program_id(0)
    qpos = q0 + lax.broadcasted_iota(jnp.int32, (nq, nk), 0)
    kpos = k0 + lax.broadcasted_iota(jnp.int32, (nq, nk), 1)
    qc = lax.shift_right_arithmetic(qpos, 6)
    kc = lax.shift_right_arithmetic(kpos, 6)
    band = (kc <= qc) & (kc >= qc - BAND_CHUNKS)
    width = pl.next_power_of_2(nq + nk)
    x = lax.broadcasted_iota(jnp.int32, (8, width), 1)
    idx = jnp.clip(q0 - k0 + nq - x, -REL_CLIP, REL_CLIP) + REL_CLIP

    def body(r, acc):
        return jnp.where(idx == r, tab_ref[r * N_HEADS + h], acc)

    ext = lax.fori_loop(0, 2 * REL_CLIP + 1, body, jnp.zeros((8, width), F32))
    lines = jnp.broadcast_to(ext[0:1], (nq, width))
    bias = pltpu.roll(lines, width - nq, 1, stride=1, stride_axis=0)[:, :nk]
    o_ref[0] = jnp.where(band, bias * LOG2E, NEG_INF)


def _band_bias(rel_table, nq, nk, q0, k0):
    return pl.pallas_call(
        functools.partial(_bias_kernel, nq=nq, nk=nk, q0=q0, k0=k0),
        grid=(N_HEADS,),
        in_specs=[pl.BlockSpec(memory_space=pltpu.SMEM)],
        out_specs=pl.BlockSpec((1, nq, nk), lambda h: (h, 0, 0)),
        out_shape=jax.ShapeDtypeStruct((N_HEADS, nq, nk), F32),
        compiler_params=_params(1),
        name="band_bias",
    )(rel_table.reshape(-1))


def _band_kernel(q_ref, k_ref, v_ref, bias_ref, o_ref, *, tq, n_sub, win, pad_rows):
    lane = lax.broadcasted_iota(jnp.int32, (tq, LANES), 1)
    col = lax.broadcasted_iota(jnp.int32, (tq, win), 1)
    for c in range(n_sub):
        i = pl.program_id(2) * n_sub + c
        start = pl.multiple_of(i * tq, tq)
        k = k_ref[0, pl.ds(start, win), :]
        v = v_ref[0, pl.ds(start, win), :]
        v1 = jnp.concatenate([v, jnp.ones_like(v)], axis=1)
        valid = col >= pad_rows - i * tq
        q = q_ref[0, c * tq:(c + 1) * tq, :]
        outs = []
        for h in range(2):
            qh = jnp.where((lane < HEAD_DIM) if h == 0 else (lane >= HEAD_DIM), q, jnp.zeros_like(q))
            s = jnp.where(valid, _dot_nt(qh, k) + bias_ref[h], NEG_INF)
            p = jnp.exp2(s - jnp.max(s, axis=1, keepdims=True))
            pv = _dot(p.astype(BF16), v1)
            outs.append(pv[:, :LANES] / pv[:, LANES:])
        o_ref[0, c * tq:(c + 1) * tq, :] = jnp.where(lane < HEAD_DIM, outs[0], outs[1]).astype(BF16)


def _band(q, k_pad, v_pad, bias, *, tq, n_sub, pad_rows):
    nb, rows, _ = q.shape
    tkk = k_pad.shape[1]
    win = bias.shape[2]
    tqq = tq * n_sub
    return pl.pallas_call(
        functools.partial(_band_kernel, tq=tq, n_sub=n_sub, win=win, pad_rows=pad_rows),
        grid=(nb, N_PAIRS, rows // tqq),
        in_specs=[pl.BlockSpec((1, tqq, LANES), lambda b, p, i: (b, i, p)),
                  pl.BlockSpec((1, tkk, LANES), lambda b, p, i: (b, 0, p)),
                  pl.BlockSpec((1, tkk, LANES), lambda b, p, i: (b, 0, p)),
                  pl.BlockSpec((2, tq, win), lambda b, p, i: (p, 0, 0))],
        out_specs=pl.BlockSpec((1, tqq, LANES), lambda b, p, i: (b, i, p)),
        out_shape=jax.ShapeDtypeStruct((nb, rows, 512), BF16),
        compiler_params=_params(3),
        name="attn_band",
    )(q, k_pad, v_pad, bias)


def _in_cd_kernel(x_ref, sh_ref, sc_ref, gpre_ref, w_ref,
                  qc_ref, kc16_ref, vc16_ref, qd_ref, kd16_ref, vd16_ref, kc32_ref, vc32_ref, kd32_ref, vd32_ref,
                  vdt_ref=None):
    x = x_ref[0]
    h = _rms(x, gpre_ref[...]) * (1.0 + sc_ref[0]) + sh_ref[0]
    y = _dot(h.astype(BF16), w_ref[...])
    qc_ref[0] = (y[:, 0:512] * QK_SCALE_64).astype(BF16)
    qd_ref[0] = (y[:, 1536:2048] * QK_SCALE_64).astype(BF16)
    for lo, r32, r16 in ((512, kc32_ref, kc16_ref), (1024, vc32_ref, vc16_ref),
                         (2048, kd32_ref, kd16_ref), (2560, vd32_ref, vd16_ref)):
        part = y[:, lo:lo + 512]
        r32[0] = part
        r16[0] = part.astype(BF16)
    if vdt_ref is not None:
        _store_vt(vdt_ref, y[:, 2560:3072])


def _in_cd(x, sh, sc, gpre, w, tm, with_vt):
    nb, rows, _ = x.shape
    row3 = lambda wd: pl.BlockSpec((1, tm, wd), lambda b, t: (b, t, 0))
    mod = pl.BlockSpec((1, 1, D_MODEL), lambda b, t: (b, 0, 0))
    const = lambda a: pl.BlockSpec(a.shape, lambda b, t: (0,) * a.ndim)
    out_specs = [row3(512)] * 10
    out_shape = [jax.ShapeDtypeStruct((nb, rows, 512), BF16)] * 6 + [jax.ShapeDtypeStruct((nb, rows, 512), F32)] * 4
    if with_vt:
        vt_shape, vt_spec = _vt_out(nb, rows, tm, ATT_SUB)
        out_shape.append(vt_shape)
        out_specs.append(vt_spec)
    return pl.pallas_call(
        _in_cd_kernel,
        grid=(nb, rows // tm),
        in_specs=[row3(D_MODEL), mod, mod, const(gpre), const(w)],
        out_specs=out_specs,
        out_shape=out_shape,
        compiler_params=_params(2),
        name="in_cd",
    )(x, sh, sc, gpre, w)


def _ffn_kernel(o1_ref, o2_ref, wo1_ref, wo2_ref, gate_m_ref, gpost_m_ref,
                x_ref, sh_ref, sc_ref, gate_ref, gpre_ref, gpost_ref, prev_ref, wg_ref, wu_ref, cw_ref, cb_ref, wd_ref,
                xo_ref, conv_ref, gbuf_ref, *, tm):
    t = pl.program_id(1)
    o = _dot(o1_ref[0], wo1_ref[...]) + _dot(o2_ref[0], wo2_ref[...])
    x = x_ref[0] + gate_m_ref[0] * _rms(o, gpost_m_ref[...])
    h = (_rms(x, gpre_ref[...]) * (1.0 + sc_ref[0]) + sh_ref[0]).astype(BF16)

    @pl.when(t == 0)
    def _():
        gbuf_ref[0:8, :] = jnp.zeros((8, D_FF), F32)
        gbuf_ref[8 - (CONV_W - 1):8, :] = prev_ref[0]

    acc = jnp.zeros((tm, D_MODEL), F32)
    for c in range(N_FF_CHUNKS):
        cols = slice(c * FF_CHUNK, (c + 1) * FF_CHUNK)
        g = _dot(h, wg_ref[c])
        u = _dot(h, wu_ref[c])
        gbuf_ref[8:8 + tm, cols] = g
        g1 = gbuf_ref[7:7 + tm, cols]
        g2 = gbuf_ref[6:6 + tm, cols]
        cw = cw_ref[c]
        gc = cb_ref[c] + (cw[0:1] * g2 + cw[1:2] * g1 + cw[2:3] * g)
        a = (gc * jax.nn.sigmoid(gc)) * u
        acc = acc + _dot(a.astype(BF16), wd_ref[c])
    conv_ref[0] = gbuf_ref[tm + 6:tm + 8, :]
    gbuf_ref[0:8, :] = gbuf_ref[tm:tm + 8, :]
    xo_ref[0] = x + gate_ref[0] * _rms(acc, gpost_ref[...])


def _out_ffn(o1, o2, wo1, wo2, gate_m, gpost_m, x, sh, sc, gate, gpre, gpost, prev, P, tm):
    nb, rows, _ = x.shape
    row3 = lambda wd: pl.BlockSpec((1, tm, wd), lambda b, t: (b, t, 0))
    mod = pl.BlockSpec((1, 1, D_MODEL), lambda b, t: (b, 0, 0))
    const = lambda a: pl.BlockSpec(a.shape, lambda b, t: (0,) * a.ndim)
    conv = pl.BlockSpec((1, CONV_W - 1, D_FF), lambda b, t: (b, 0, 0))
    consts = [P["wg"], P["wu"], P["cw"], P["cb"], P["wd"]]
    resident = lambda a: pl.BlockSpec(a.shape, lambda b, t: (0,) * a.ndim, pipeline_mode=pl.Buffered(1))
    return pl.pallas_call(
        functools.partial(_ffn_kernel, tm=tm),
        grid=(nb, rows // tm),
        in_specs=[row3(512), row3(512), const(wo1), const(wo2), mod, const(gpost_m),
                  row3(D_MODEL), mod, mod, mod, const(gpre), const(gpost), conv] + [resident(a) for a in consts],
        out_specs=[row3(D_MODEL), conv],
        out_shape=[jax.ShapeDtypeStruct((nb, rows, D_MODEL), F32),
                   jax.ShapeDtypeStruct((nb, CONV_W - 1, D_FF), F32)],
        scratch_shapes=[pltpu.VMEM((tm + 8, D_FF), F32)],
        compiler_params=_params(2),
        name="out_ffn",
    )(o1, o2, wo1, wo2, gate_m, gpost_m, x, sh, sc, gate, gpre, gpost, prev, *consts)


def _rot_half_cols(w):
    half = ROPE_DIM // 2
    return jnp.concatenate([-w[..., half:], w[..., :half]], axis=-1)


def _prep_ab(w_in_ab, b_f, q_a_g, kv_a_g, w_uq, w_ukv, w_out_ab):
    z = lambda n: jnp.zeros((D_MODEL, n), F32)
    c_q, c_kv, k_r = w_in_ab[:, 0:256], w_in_ab[:, 256:384], w_in_ab[:, 384:416]
    q_b, k_b, v_b, f_b = w_in_ab[:, 416:928], w_in_ab[:, 928:1440], w_in_ab[:, 1440:1952], w_in_ab[:, 1952:1960]
    w_ab = jnp.concatenate([c_q, c_kv, k_r, z(LANES - ROPE_DIM), _rot_half_cols(k_r), z(LANES - ROPE_DIM),
                            q_b, k_b, v_b, f_b, z(LANES - N_HEADS)], axis=1).astype(BF16)
    w_ft = jnp.concatenate([f_b.T, jnp.zeros((N_HEADS, D_MODEL), F32)], axis=0).astype(BF16)
    uq = w_uq.reshape(Q_RANK, N_HEADS, NOPE_DIM + ROPE_DIM)
    nope, rope_w = uq[..., :NOPE_DIM], uq[..., NOPE_DIM:]
    zq = lambda n: jnp.zeros((Q_RANK, N_HEADS, n), F32)
    wq_main = jnp.concatenate([nope, rope_w, zq(LANES - NOPE_DIM - ROPE_DIM)], axis=-1)
    wq_rot = jnp.concatenate([zq(NOPE_DIM), _rot_half_cols(rope_w), zq(LANES - NOPE_DIM - ROPE_DIM)], axis=-1)
    wq = jnp.concatenate([wq_main.reshape(Q_RANK, -1), wq_rot.reshape(Q_RANK, -1)], axis=1).astype(BF16)
    ukv = w_ukv.reshape(KV_RANK, N_HEADS, 2 * HEAD_DIM)
    wk = jnp.concatenate([ukv[..., :NOPE_DIM], jnp.zeros((KV_RANK, N_HEADS, LANES - NOPE_DIM), F32)], axis=-1)
    place = np.zeros((LANES, N_HEADS, LANES), np.float32)
    for r in range(ROPE_DIM):
        place[r, :, NOPE_DIM + r] = 1.0
    wkv = jnp.concatenate([
        jnp.concatenate([wk.reshape(KV_RANK, -1), ukv[..., NOPE_DIM:].reshape(KV_RANK, -1)], axis=1),
        jnp.concatenate([jnp.asarray(place.reshape(LANES, -1)), jnp.zeros((LANES, 512), F32)], axis=1),
    ], axis=0).astype(BF16)
    return {
        "w_ab": w_ab, "w_ft": w_ft, "wq": wq, "wkv": wkv,
        "bf128": jnp.concatenate([b_f, jnp.zeros((LANES - N_HEADS,), F32)]).reshape(1, LANES),
        "bft": b_f.reshape(N_HEADS, 1),
        "gq": q_a_g.reshape(1, Q_RANK), "gkv": kv_a_g.reshape(1, KV_RANK),
        "wo1": w_out_ab[:512].astype(BF16), "wo2": w_out_ab[512:].astype(BF16),
    }


def _prep_ffn(w_gate, w_up, conv_w, conv_b, w_down):
    chunk_cols = lambda w: w.reshape(D_MODEL, N_FF_CHUNKS, FF_CHUNK).transpose(1, 0, 2).astype(BF16)
    return {
        "wg": chunk_cols(w_gate), "wu": chunk_cols(w_up),
        "cw": conv_w.reshape(CONV_W, N_FF_CHUNKS, FF_CHUNK).transpose(1, 0, 2),
        "cb": conv_b.reshape(N_FF_CHUNKS, 1, FF_CHUNK),
        "wd": w_down.reshape(N_FF_CHUNKS, FF_CHUNK, D_MODEL).astype(BF16),
    }


def _rope_tables(pos):
    half = ROPE_DIM // 2
    inv = ROPE_BASE ** (-jnp.arange(half, dtype=F32) / half)
    ang = pos.astype(F32)[:, None] * inv[None, :]
    cos2 = jnp.concatenate([jnp.cos(ang)] * 2, axis=1)
    sin2 = jnp.concatenate([jnp.sin(ang)] * 2, axis=1)
    n = pos.shape[0]
    scale = QK_SCALE_A
    zeros = lambda w: jnp.zeros((n, w), F32)
    tqc = scale * jnp.concatenate([jnp.ones((n, NOPE_DIM), F32), cos2, zeros(LANES - NOPE_DIM - ROPE_DIM)], axis=1)
    tqs = scale * jnp.concatenate([zeros(NOPE_DIM), sin2, zeros(LANES - NOPE_DIM - ROPE_DIM)], axis=1)
    tkc = jnp.concatenate([cos2, zeros(LANES - ROPE_DIM)], axis=1)
    tks = jnp.concatenate([sin2, zeros(LANES - ROPE_DIM)], axis=1)
    return tqc, tqs, tkc, tks


def _pad_rows(a, front, total):
    return jnp.pad(a, ((0, 0), (front, total - front - a.shape[1])) + ((0, 0),) * (a.ndim - 2))


def _trunk(x, mods, past, PA, PF, w_cd, wo_cd, rel, gains, tm, tq):
    nb, rows, _ = x.shape
    q_off = 0 if past is None else PAST_LEN
    pos = q_off + jnp.arange(rows)
    tabs = _rope_tables(pos)
    mix_pre_g, mix_post_g, ffn_pre_g, ffn_post_g = gains
    out = {}

    def mod6(l):
        m = mods[l].reshape(nb, 6, 1, D_MODEL)
        return [m[:, j] for j in range(6)]

    sh_m, sc_m, g_m, sh_f, sc_f, g_f = mod6(0)
    prompt = past is None
    (qcomb, ckv, krope, ckr, qb, kb16, vb16, kb32, vb32, logf, logf128, logft, *vbt) = _in_ab(
        x, sh_m, sc_m, mix_pre_g[0:1], PA, tabs, tm, prompt)
    out["a_ckv"], out["a_krope"], out["b_k"], out["b_v"], out["b_logf"] = ckv, krope, kb32, vb32, logf
    if prompt:
        tkk = rows
        ckr_all, kb_all, vb_all, lf_all, lft_all = ckr, kb16, vb16, logf128, logft
    else:
        tkk = -(-(PAST_LEN + rows) // ATT_BLOCK) * ATT_BLOCK
        cat = lambda p, n: _pad_rows(jnp.concatenate([p, n], axis=1), 0, tkk)
        ckr_past = jnp.concatenate([past["a_ckv"].astype(BF16),
                                    jnp.pad(past["a_krope"].astype(BF16), ((0, 0), (0, 0), (0, LANES - ROPE_DIM)))],
                                   axis=-1)
        ckr_all = cat(ckr_past, ckr)
        kb_all = cat(past["b_k"].reshape(nb, PAST_LEN, 512).astype(BF16), kb16)
        vb_all = cat(past["b_v"].reshape(nb, PAST_LEN, 512).astype(BF16), vb16)
        lf_all = cat(jnp.pad(past["b_logf"], ((0, 0), (0, 0), (0, LANES - N_HEADS))), logf128)
        lft_all = jnp.pad(jnp.concatenate([jnp.swapaxes(past["b_logf"], 1, 2), logft], axis=2),
                          ((0, 0), (0, 0), (0, tkk - PAST_LEN - rows)))
    if prompt:
        kcomb, vat = _kvup(ckr_all, PA["wkv"], ATT_BLOCK, True)
        _, cumt, ck_rep = _cum(lf_all, lft_all, True)
        o_a = _flash_t("mla", qcomb, kcomb, vat, tq=ATT_BLOCK)
        o_b = _flash_t("fox", qb, kb_all, vbt[0], cumt.reshape(nb, N_PAIRS, 2, rows), ck_rep, tq=ATT_BLOCK)
    else:
        kcomb, va = _kvup(ckr_all, PA["wkv"], tkk, False)
        cum, cumt = _cum(lf_all, lft_all, False)
        cq = cum[:, q_off:q_off + rows, :N_HEADS].reshape(nb, rows, N_PAIRS, 2).transpose(0, 2, 1, 3)
        nkb = tkk // ATT_BLOCK
        ckt = cumt.reshape(nb, N_PAIRS, 2, nkb, ATT_BLOCK).transpose(0, 1, 3, 2, 4)
        o_a = _flash("mla", qcomb, kcomb, va, tq=tq, q_off=q_off, n_valid=q_off + rows)
        o_b = _flash("fox", qb, kb_all, vb_all, cq, ckt, tq=tq, q_off=q_off, n_valid=q_off + rows)
    prev = jnp.zeros((nb, CONV_W - 1, D_FF), F32) if past is None else past["ffn_conv"][0]
    x, conv0 = _out_ffn(o_a, o_b, PA["wo1"], PA["wo2"], g_m, mix_post_g[0:1],
                        x, sh_f, sc_f, g_f, ffn_pre_g[0:1], ffn_post_g[0:1], prev, PF[0], tm)

    sh_m, sc_m, g_m, sh_f, sc_f, g_f = mod6(1)
    qc, kc16, vc16, qd, kd16, vd16, kc32, vc32, kd32, vd32, *vdt = _in_cd(x, sh_m, sc_m, mix_pre_g[1:2], w_cd, tm,
                                                                          prompt)
    out["d_k"], out["d_v"] = kd32, vd32
    if prompt:
        keep = min(BAND_CHUNKS * CHUNK, rows)
        out["c_k"], out["c_v"] = kc32[:, rows - keep:], vc32[:, rows - keep:]
        front = BAND_CHUNKS * CHUNK
        kc_all = _pad_rows(kc16, front, front + rows)
        vc_all = _pad_rows(vc16, front, front + rows)
        bias = _band_bias(rel, tq, front + tq, 0, -front)
        o_c = _band(qc, kc_all, vc_all, bias, tq=tq, n_sub=2, pad_rows=front)
        o_d = _sb_t(qd, kd16, vdt[0], tq=ATT_BLOCK)
    else:
        out["c_k"], out["c_v"] = kc32, vc32
        n_c = past["c_k"].shape[1]
        front = BAND_SAMPLE_WIN - n_c - rows
        kc_all = _pad_rows(jnp.concatenate([past["c_k"].reshape(nb, n_c, 512).astype(BF16), kc16], axis=1), front,
                           BAND_SAMPLE_WIN)
        vc_all = _pad_rows(jnp.concatenate([past["c_v"].reshape(nb, n_c, 512).astype(BF16), vc16], axis=1), front,
                           BAND_SAMPLE_WIN)
        bias = _band_bias(rel, rows, BAND_SAMPLE_WIN, q_off, PAST_LEN - n_c - front)
        o_c = _band(qc, kc_all, vc_all, bias, tq=rows, n_sub=1, pad_rows=0)
        kd_all = _pad_rows(jnp.concatenate([past["d_k"].reshape(nb, PAST_LEN, 512).astype(BF16), kd16], axis=1), 0, tkk)
        vd_all = _pad_rows(jnp.concatenate([past["d_v"].reshape(nb, PAST_LEN, 512).astype(BF16), vd16], axis=1), 0, tkk)
        o_d = _sb(qd, kd_all, vd_all, tq=tq, q_off=q_off)
    prev = jnp.zeros((nb, CONV_W - 1, D_FF), F32) if past is None else past["ffn_conv"][1]
    x, conv1 = _out_ffn(o_c, o_d, wo_cd[0], wo_cd[1], g_m, mix_post_g[1:2],
                        x, sh_f, sc_f, g_f, ffn_pre_g[1:2], ffn_post_g[1:2], prev, PF[1], tm)
    out["ffn_conv"] = jnp.stack([conv0, conv1])
    return x, out


def kernel(x_prompt, x_sample, c_prompt, c_sample, cache_a_ckv, cache_a_krope, cache_b_k, cache_b_v, cache_b_logf, cache_c_k, cache_c_v, cache_d_k, cache_d_v, state_ffn_conv, ada_w, ada_b, mix_pre_g, mix_post_g, ffn_pre_g, ffn_post_g, w_in_ab, b_f, q_a_g, kv_a_g, w_uq, w_ukv, w_out_ab, w_in_cd, rel_bias_c, w_out_cd, ffn_w_gate, ffn_w_up, ffn_conv_w, ffn_conv_b, ffn_w_down):
    nbp, nbs = x_prompt.shape[0], x_sample.shape[0]
    c_all = jnp.concatenate([c_prompt, c_sample, jnp.zeros((32 - nbp - nbs, D_MODEL), F32)], axis=0)
    mods = _ada(c_all, ada_w, ada_b)
    PA = _prep_ab(w_in_ab[0], b_f[0], q_a_g[0], kv_a_g[0], w_uq[0], w_ukv[0], w_out_ab[0])
    PF = [_prep_ffn(ffn_w_gate[l], ffn_w_up[l], ffn_conv_w[l], ffn_conv_b[l], ffn_w_down[l]) for l in range(DEPTH)]
    w_cd = w_in_cd[0].astype(BF16)
    wo_cd = (w_out_cd[0][:512].astype(BF16), w_out_cd[0][512:].astype(BF16))
    gains = (mix_pre_g, mix_post_g, ffn_pre_g, ffn_post_g)
    past = {"a_ckv": cache_a_ckv[0], "a_krope": cache_a_krope[0], "b_k": cache_b_k[0], "b_v": cache_b_v[0],
            "b_logf": cache_b_logf[0], "c_k": cache_c_k[0], "c_v": cache_c_v[0], "d_k": cache_d_k[0],
            "d_v": cache_d_v[0], "ffn_conv": state_ffn_conv}
    y_p, sp = _trunk(x_prompt, mods[:, :nbp], None, PA, PF, w_cd, wo_cd, rel_bias_c[0], gains, 512, ATT_SUB)
    y_s, ss = _trunk(x_sample, mods[:, nbp:nbp + nbs], past, PA, PF, w_cd, wo_cd, rel_bias_c[0], gains,
                     x_sample.shape[1], x_sample.shape[1])

    def heads(a):
        return a.reshape(a.shape[0], a.shape[1], N_HEADS, HEAD_DIM)[None]

    return (y_p, y_s,
            sp["a_ckv"][None], ss["a_ckv"][None], sp["a_krope"][None], ss["a_krope"][None],
            heads(sp["b_k"]), heads(ss["b_k"]), heads(sp["b_v"]), heads(ss["b_v"]),
            sp["b_logf"][None], ss["b_logf"][None],
            heads(sp["c_k"]), heads(ss["c_k"]), heads(sp["c_v"]), heads(ss["c_v"]),
            heads(sp["d_k"]), heads(ss["d_k"]), heads(sp["d_v"]), heads(ss["d_v"]),
            sp["ffn_conv"], ss["ffn_conv"])
```

```python
import functools

import jax
import jax.numpy as jnp
import numpy as np
from jax import lax
from jax.experimental import pallas as pl
from jax.experimental.pallas import tpu as pltpu

D_MODEL = 1024
DEPTH = 2
PAST_LEN = 1024
CHUNK = 64
HEAD_DIM = 64
N_HEADS = 8
N_PAIRS = N_HEADS // 2
Q_RANK = 256
KV_RANK = 128
NOPE_DIM = 64
ROPE_DIM = 32
ROPE_BASE = 10000.0
BAND_CHUNKS = 8
REL_CLIP = 128
D_FF = 2816
CONV_W = 3
EPS = 1e-6
NEG_INF = -1e30
LOG2E = 1.4426950408889634
QK_SCALE_64 = HEAD_DIM ** -0.5 * LOG2E
QK_SCALE_A = (NOPE_DIM + ROPE_DIM) ** -0.5 * LOG2E
SB_UNDERFLOW_BITS = 160.0
SB_PAIRS = 4
SB_CUM = 128
ONES_ROWS = 16

LANES = 128
FF_CHUNK = 2816
N_FF_CHUNKS = D_FF // FF_CHUNK
ATT_BLOCK = 512
ATT_SUB = 256
BAND_SAMPLE_WIN = 640
VMEM_LIMIT_BYTES = 56 * 1024 * 1024

F32 = jnp.float32
BF16 = jnp.bfloat16

AB_CQ, AB_CKV, AB_KR, AB_KRROT, AB_QB, AB_KB, AB_VB, AB_FB, AB_COLS = 0, 256, 384, 512, 640, 1152, 1664, 2176, 2304


def _params(n_axes):
    return pltpu.CompilerParams(dimension_semantics=("arbitrary",) * n_axes, vmem_limit_bytes=VMEM_LIMIT_BYTES)


def _rms(x, g):
    return x * lax.rsqrt(jnp.mean(x * x, axis=-1, keepdims=True) + EPS) * g


def _log_sigmoid(x):
    return jnp.minimum(x, 0.0) - jnp.log1p(jnp.exp(-jnp.abs(x)))


def _log2_sigmoids(z2):
    l2 = jnp.log2(1.0 + jnp.exp2(-jnp.abs(z2)))
    return jnp.minimum(z2, 0.0) - l2, jnp.maximum(z2, 0.0) + l2


def _dot(a, b):
    return jnp.dot(a, b, preferred_element_type=F32)


def _dot_nt(a, b):
    return lax.dot_general(a, b, (((1,), (1,)), ((), ())), preferred_element_type=F32)


def _split_bf16(x, n):
    parts = []
    for _ in range(n):
        p = x.astype(BF16)
        parts.append(p)
        x = x - p.astype(F32)
    return parts


def _tile_lanes(x, n):
    return x if n == 1 else jnp.concatenate([x] * n, axis=1)


def _ada_kernel(c_ref, w_ref, b_ref, o_ref):
    c = c_ref[...]
    cond = c * jax.nn.sigmoid(c)
    o_ref[0] = _dot(cond.astype(BF16), w_ref[0].astype(BF16)) + b_ref[0]


def _ada(c_all, ada_w, ada_b):
    rows = c_all.shape[0]
    tn = 1536
    return pl.pallas_call(
        _ada_kernel,
        grid=(DEPTH, 6 * D_MODEL // tn),
        in_specs=[
            pl.BlockSpec((rows, D_MODEL), lambda l, j: (0, 0)),
            pl.BlockSpec((1, D_MODEL, tn), lambda l, j: (l, 0, j)),
            pl.BlockSpec((1, 1, tn), lambda l, j: (l, 0, j)),
        ],
        out_specs=pl.BlockSpec((1, rows, tn), lambda l, j: (l, 0, j)),
        out_shape=jax.ShapeDtypeStruct((DEPTH, rows, 6 * D_MODEL), F32),
        compiler_params=_params(2),
        name="ada",
    )(c_all, ada_w, ada_b.reshape(DEPTH, 1, 6 * D_MODEL))


def _store_vt(vt_ref, v):
    vt = v.T
    blk = vt_ref.shape[-1]
    for p in range(N_PAIRS):
        for c in range(vt_ref.shape[2]):
            vt_ref[0, p, c] = vt[p * LANES:(p + 1) * LANES, c * blk:(c + 1) * blk].astype(BF16)


def _in_ab_kernel(x_ref, sh_ref, sc_ref, gpre_ref, w_ref, wft_ref, bf_ref, bft_ref, gq_ref, gkv_ref, wq_ref,
                  tqc_ref, tqs_ref, tkc_ref, tks_ref,
                  qcomb_ref, ckv_ref, krope_ref, ckr_ref, qb_ref, kb16_ref, vb16_ref, kb32_ref, vb32_ref,
                  logf_ref, logf128_ref, logft_ref, vbt_ref=None):
    x = x_ref[0]
    h = _rms(x, gpre_ref[...]) * (1.0 + sc_ref[0]) + sh_ref[0]
    hb = h.astype(BF16)
    y = _dot(hb, w_ref[...])
    cq = _rms(y[:, AB_CQ:AB_CQ + Q_RANK], gq_ref[...]).astype(BF16)
    qa = _dot(cq, wq_ref[...])
    tqc = tqc_ref[...]
    tqs = tqs_ref[...]
    half = N_HEADS * LANES
    for hh in range(N_HEADS):
        lo = hh * LANES
        qcomb_ref[0, :, lo:lo + LANES] = (qa[:, lo:lo + LANES] * tqc + qa[:, half + lo:half + lo + LANES] * tqs).astype(BF16)
    ckv = _rms(y[:, AB_CKV:AB_CKV + KV_RANK], gkv_ref[...])
    ckv_ref[0] = ckv
    kr = y[:, AB_KR:AB_KR + LANES] * tkc_ref[...] + y[:, AB_KRROT:AB_KRROT + LANES] * tks_ref[...]
    krope_ref[0] = kr[:, :ROPE_DIM]
    ckr_ref[0, :, 0:LANES] = ckv.astype(BF16)
    ckr_ref[0, :, LANES:2 * LANES] = kr.astype(BF16)
    qb_ref[0] = (y[:, AB_QB:AB_QB + 512] * QK_SCALE_64).astype(BF16)
    kb = y[:, AB_KB:AB_KB + 512]
    vb = y[:, AB_VB:AB_VB + 512]
    kb32_ref[0] = kb
    vb32_ref[0] = vb
    kb16_ref[0] = kb.astype(BF16)
    vb16_ref[0] = vb.astype(BF16)
    if vbt_ref is not None:
        _store_vt(vbt_ref, vb)
    logf = _log_sigmoid(y[:, AB_FB:AB_FB + LANES] + bf_ref[...])
    logf128_ref[0] = logf
    logf_ref[0] = logf[:, :N_HEADS]
    ft = _dot_nt(wft_ref[...], hb)
    logft_ref[0] = _log_sigmoid(ft[:N_HEADS] + bft_ref[...])


def _vt_out(nb, rows, tm, blk=None):
    blk = tm if blk is None else blk
    return (jax.ShapeDtypeStruct((nb, N_PAIRS, rows // blk, LANES, blk), BF16),
            pl.BlockSpec((1, N_PAIRS, tm // blk, LANES, blk), lambda b, t: (b, 0, t, 0, 0)))


def _in_ab(x, sh, sc, gpre, P, tabs, tm, with_vt):
    nb, rows, _ = x.shape
    nt = rows // tm
    row3 = lambda w: pl.BlockSpec((1, tm, w), lambda b, t: (b, t, 0))
    mod = pl.BlockSpec((1, 1, D_MODEL), lambda b, t: (b, 0, 0))
    const = lambda a: pl.BlockSpec(a.shape, lambda b, t: (0,) * a.ndim)
    tab = pl.BlockSpec((tm, LANES), lambda b, t: (t, 0))
    out_shapes = [
        jax.ShapeDtypeStruct((nb, rows, N_HEADS * LANES), BF16),
        jax.ShapeDtypeStruct((nb, rows, KV_RANK), F32),
        jax.ShapeDtypeStruct((nb, rows, ROPE_DIM), F32),
        jax.ShapeDtypeStruct((nb, rows, 2 * LANES), BF16),
        jax.ShapeDtypeStruct((nb, rows, 512), BF16),
        jax.ShapeDtypeStruct((nb, rows, 512), BF16),
        jax.ShapeDtypeStruct((nb, rows, 512), BF16),
        jax.ShapeDtypeStruct((nb, rows, 512), F32),
        jax.ShapeDtypeStruct((nb, rows, 512), F32),
        jax.ShapeDtypeStruct((nb, rows, N_HEADS), F32),
        jax.ShapeDtypeStruct((nb, rows, LANES), F32),
        jax.ShapeDtypeStruct((nb, N_HEADS, rows), F32),
    ]
    out_specs = [row3(N_HEADS * LANES), row3(KV_RANK), row3(ROPE_DIM), row3(2 * LANES), row3(512), row3(512),
                 row3(512), row3(512), row3(512), row3(N_HEADS), row3(LANES),
                 pl.BlockSpec((1, N_HEADS, tm), lambda b, t: (b, 0, t))]
    if with_vt:
        vt_shape, vt_spec = _vt_out(nb, rows, tm)
        out_shapes.append(vt_shape)
        out_specs.append(vt_spec)
    consts = [P["w_ab"], P["w_ft"], P["bf128"], P["bft"], P["gq"], P["gkv"], P["wq"]]
    return pl.pallas_call(
        _in_ab_kernel,
        grid=(nb, nt),
        in_specs=[row3(D_MODEL), mod, mod, const(gpre)] + [const(a) for a in consts] + [tab] * 4,
        out_specs=out_specs,
        out_shape=out_shapes,
        compiler_params=_params(2),
        name="in_ab",
    )(x, sh, sc, gpre, *consts, *tabs)


def _kvup_kernel(ckr_ref, w_ref, k_ref, v_ref):
    y = _dot(ckr_ref[0], w_ref[...])
    k_ref[0] = y[:, :N_HEADS * LANES].astype(BF16)
    v_ref[0] = y[:, N_HEADS * LANES:].astype(BF16)


def _kvup_t_kernel(ckr_ref, w_ref, k_ref, vt_ref):
    y = _dot(ckr_ref[0], w_ref[...])
    k_ref[0] = y[:, :N_HEADS * LANES].astype(BF16)
    _store_vt(vt_ref, y[:, N_HEADS * LANES:])


def _kvup(ckr, w, tm, with_vt):
    nb, rows, _ = ckr.shape
    if with_vt:
        v_shape, v_spec = _vt_out(nb, rows, tm)
    else:
        v_shape = jax.ShapeDtypeStruct((nb, rows, 512), BF16)
        v_spec = pl.BlockSpec((1, tm, 512), lambda b, t: (b, t, 0))
    return pl.pallas_call(
        _kvup_t_kernel if with_vt else _kvup_kernel,
        grid=(nb, rows // tm),
        in_specs=[pl.BlockSpec((1, tm, 2 * LANES), lambda b, t: (b, t, 0)),
                  pl.BlockSpec(w.shape, lambda b, t: (0, 0))],
        out_specs=[pl.BlockSpec((1, tm, N_HEADS * LANES), lambda b, t: (b, t, 0)), v_spec],
        out_shape=[jax.ShapeDtypeStruct((nb, rows, N_HEADS * LANES), BF16), v_shape],
        compiler_params=_params(2),
        name="kvup",
    )(ckr, w)


def _cum_kernel(x_ref, xt_ref, c_ref, ct_ref, rep_ref=None, *, n_chunks, tc):
    r = lax.broadcasted_iota(jnp.int32, (tc, tc), 0)
    c = lax.broadcasted_iota(jnp.int32, (tc, tc), 1)
    lower = jnp.where(c <= r, 1.0, 0.0).astype(BF16)
    upper = jnp.where(r <= c, 1.0, 0.0).astype(BF16)
    carry = jnp.zeros((1, LANES), F32)
    carry_t = jnp.zeros((N_HEADS, 1), F32)
    for ci in range(n_chunks):
        sl = slice(ci * tc, (ci + 1) * tc)
        cs = carry
        for p in _split_bf16(x_ref[0, sl, :], 3):
            cs = cs + _dot(lower, p)
        c_ref[0, sl, :] = cs * LOG2E
        if rep_ref is not None:
            for h in range(N_HEADS):
                rep_ref[0, h, sl, :] = jnp.broadcast_to(cs[:, h:h + 1] * LOG2E, (tc, LANES))
        carry = cs[tc - 1:tc, :]
        xt = jnp.concatenate([xt_ref[0, :, sl], jnp.zeros((N_HEADS, tc), F32)], axis=0)
        cst = jnp.zeros((2 * N_HEADS, tc), F32)
        for p in _split_bf16(xt, 3):
            cst = cst + _dot(p, upper)
        cst = cst[:N_HEADS] + carry_t
        ct_ref[0, :, sl] = cst * LOG2E
        carry_t = cst[:, tc - 1:tc]


def _cum(logf128, logft, with_rep):
    nb, tk, _ = logf128.shape
    tc = ATT_SUB
    out_specs = [pl.BlockSpec((1, tk, LANES), lambda b: (b, 0, 0)),
                 pl.BlockSpec((1, N_HEADS, tk), lambda b: (b, 0, 0))]
    out_shape = [jax.ShapeDtypeStruct((nb, tk, LANES), F32), jax.ShapeDtypeStruct((nb, N_HEADS, tk), F32)]
    if with_rep:
        out_specs.append(pl.BlockSpec((1, N_HEADS, tk, LANES), lambda b: (b, 0, 0, 0)))
        out_shape.append(jax.ShapeDtypeStruct((nb, N_HEADS, tk, LANES), F32))
    return pl.pallas_call(
        functools.partial(_cum_kernel, n_chunks=tk // tc, tc=tc),
        grid=(nb,),
        in_specs=[pl.BlockSpec((1, tk, LANES), lambda b: (b, 0, 0)),
                  pl.BlockSpec((1, N_HEADS, tk), lambda b: (b, 0, 0))],
        out_specs=out_specs,
        out_shape=out_shape,
        compiler_params=_params(1),
        name="cum",
    )(logf128, logft)


def _flash_kernel(*refs, mode, tq, tk, q_off, n_valid, tail_widths):
    if mode == "fox":
        q_ref, k_ref, v_ref, cq_ref, ckt_ref, o_ref, m_s, l_s, acc_s = refs
    else:
        q_ref, k_ref, v_ref, o_ref, m_s, l_s, acc_s = refs
    i = pl.program_id(2)
    qpos0 = q_off + i * tq
    n_full = qpos0 // tk
    rem = qpos0 + tq - n_full * tk
    lane = lax.broadcasted_iota(jnp.int32, (tq, LANES), 1)

    qs, kcols, cq_tiles = [], [], []
    for hp in range(N_HEADS):
        p, h = divmod(hp, 2)
        pair = slice(p * LANES, (p + 1) * LANES)
        if mode == "fox":
            q = q_ref[0, :, pair]
            qs.append(jnp.where((lane < HEAD_DIM) if h == 0 else (lane >= HEAD_DIM), q, jnp.zeros_like(q)))
            kcols.append(pair)
            cq_tiles.append(jnp.broadcast_to(cq_ref[0, p][:, h:h + 1], (tq, LANES)))
        else:
            qs.append(q_ref[0, :, hp * LANES:(hp + 1) * LANES])
            kcols.append(slice(hp * LANES, (hp + 1) * LANES))
        m_s[hp] = jnp.full((tq, LANES), NEG_INF, F32)
        l_s[hp] = jnp.zeros((tq, LANES), F32)
        acc_s[hp] = jnp.zeros((tq, LANES), F32)

    def step(j, width, masked):
        nrep = width // LANES
        r0 = pl.multiple_of(j * tk, tk)
        if masked:
            kpos = n_full * tk + lax.broadcasted_iota(jnp.int32, (tq, width), 1)
            qpos = qpos0 + lax.broadcasted_iota(jnp.int32, (tq, width), 0)
            if mode == "fox":
                mask = kpos <= qpos
            else:
                mask = (lax.shift_right_logical(kpos, 6) <= lax.shift_right_logical(qpos, 6)) & (kpos < n_valid)
        for hp in range(N_HEADS):
            p, h = divmod(hp, 2)
            s = _dot_nt(qs[hp], k_ref[0, pl.ds(r0, width), kcols[hp]])
            if mode == "fox":
                s = s + (_tile_lanes(cq_tiles[hp], nrep) - ckt_ref[0, p, j][h:h + 1, :width])
            if masked:
                s = jnp.where(mask, s, NEG_INF)
            m_prev = m_s[hp]
            m_next = jnp.maximum(m_prev, jnp.max(s, axis=1, keepdims=True))
            pr = jnp.exp2(s - _tile_lanes(m_next, nrep))
            alpha = jnp.exp2(m_prev - m_next)
            l_s[hp] = alpha * l_s[hp] + jnp.sum(pr, axis=1, keepdims=True)
            v = v_ref[0, pl.ds(r0, width), p * LANES:(p + 1) * LANES]
            acc_s[hp] = alpha * acc_s[hp] + _dot(pr.astype(BF16), v)
            m_s[hp] = m_next

    def body(j, carry):
        step(j, tk, False)
        return carry

    lax.fori_loop(0, n_full, body, 0)
    if len(tail_widths) == 1:
        step(n_full, tail_widths[0], True)
    else:
        lo_w, hi_w = tail_widths
        pl.when(rem <= lo_w)(lambda: step(n_full, lo_w, True))
        pl.when(rem > lo_w)(lambda: step(n_full, hi_w, True))

    for p in range(N_PAIRS):
        o0 = acc_s[2 * p] / l_s[2 * p]
        o1 = acc_s[2 * p + 1] / l_s[2 * p + 1]
        o_ref[0, :, p * LANES:(p + 1) * LANES] = jnp.where(lane < HEAD_DIM, o0, o1).astype(BF16)


def _tail_widths(tk):
    return (ATT_SUB, tk) if tk > ATT_SUB else (tk,)


def _flash(mode, q, k, v, cq=None, ckt=None, *, tq, q_off, n_valid):
    nb, rows, _ = q.shape
    tkk = k.shape[1]
    tk = ATT_BLOCK
    kw = q.shape[2]
    in_specs = [pl.BlockSpec((1, tq, kw), lambda b, p, i: (b, i, 0)),
                pl.BlockSpec((1, tkk, kw), lambda b, p, i: (b, 0, 0)),
                pl.BlockSpec((1, tkk, 512), lambda b, p, i: (b, 0, 0))]
    args = [q, k, v]
    if mode == "fox":
        in_specs += [pl.BlockSpec((1, N_PAIRS, tq, 2), lambda b, p, i: (b, 0, i, 0)),
                     pl.BlockSpec((1, N_PAIRS, tkk // tk, 2, tk), lambda b, p, i: (b, 0, 0, 0, 0))]
        args += [cq, ckt]
    return pl.pallas_call(
        functools.partial(_flash_kernel, mode=mode, tq=tq, tk=tk, q_off=q_off, n_valid=n_valid,
                          tail_widths=_tail_widths(tk)),
        grid=(nb, 1, rows // tq),
        in_specs=in_specs,
        out_specs=pl.BlockSpec((1, tq, 512), lambda b, p, i: (b, i, 0)),
        out_shape=jax.ShapeDtypeStruct((nb, rows, 512), BF16),
        scratch_shapes=[pltpu.VMEM((N_HEADS, tq, LANES), F32)] * 3,
        compiler_params=_params(3),
        name="attn_" + mode,
    )(*args)


def _sb_kernel(q_ref, k_ref, v_ref, o_ref, acc_s, r_s, u_s, *, tq, tk, q_off, tail_widths):
    i = pl.program_id(2)
    qpos0 = q_off + i * tq
    n_full = qpos0 // tk
    rem = qpos0 + tq - n_full * tk
    nrep = ATT_SUB // LANES
    lane = lax.broadcasted_iota(jnp.int32, (tq, LANES), 1)
    ur = lax.broadcasted_iota(jnp.int32, (ATT_SUB, ATT_SUB), 0)
    uc = lax.broadcasted_iota(jnp.int32, (ATT_SUB, ATT_SUB), 1)
    u_s[...] = jnp.where(ur > uc, 1.0, 0.0).astype(BF16)
    qs = []
    for hp in range(N_HEADS):
        p, h = divmod(hp, 2)
        q = q_ref[0, :, p * LANES:(p + 1) * LANES]
        qs.append(jnp.where((lane < HEAD_DIM) if h == 0 else (lane >= HEAD_DIM), q, jnp.zeros_like(q)))
    acc_s[...] = jnp.zeros((N_HEADS, tq, LANES), F32)
    r_s[...] = jnp.zeros((N_HEADS, tq, LANES), F32)

    def step(j, width, masked):
        r0 = pl.multiple_of(j * tk, tk)
        if masked:
            kpos = n_full * tk + lax.broadcasted_iota(jnp.int32, (tq, width), 1)
            qpos = qpos0 + lax.broadcasted_iota(jnp.int32, (tq, width), 0)
            mask = kpos < qpos
        u = u_s[...]
        for h in range(N_HEADS):
            pair = slice(h // 2 * LANES, (h // 2 + 1) * LANES)
            k = k_ref[0, pl.ds(r0, width), pair]
            v = v_ref[0, pl.ds(r0, width), pair]
            z = _dot_nt(qs[h], k)
            lsz, nlk = _log2_sigmoids(z)
            if masked:
                nlk = jnp.where(mask, nlk, 0.0)
            r = r_s[h]
            ws = []
            for sb in reversed(range(width // ATT_SUB)):
                cols = slice(sb * ATT_SUB, (sb + 1) * ATT_SUB)
                nlk_sb = nlk[:, cols]
                hi, lo = _split_bf16(nlk_sb, 2)
                after = _dot(hi, u) + _dot(lo, u) + _tile_lanes(r, nrep)
                w = jnp.exp2(lsz[:, cols] - after)
                if masked:
                    w = jnp.where(mask[:, cols], w, 0.0)
                ws.append(w.astype(BF16))
                r = r + jnp.sum(nlk_sb, axis=1, keepdims=True)
            w_all = ws[0] if len(ws) == 1 else jnp.concatenate(ws[::-1], axis=1)
            acc_s[h] = acc_s[h] + _dot(w_all, v)
            r_s[h] = r

    if len(tail_widths) == 1:
        step(n_full, tail_widths[0], True)
    else:
        lo_w, hi_w = tail_widths
        pl.when(rem <= lo_w)(lambda: step(n_full, lo_w, True))
        pl.when(rem > lo_w)(lambda: step(n_full, hi_w, True))

    def body(jj, carry):
        step(n_full - 1 - jj, tk, False)
        return carry

    lax.fori_loop(0, n_full, body, 0)
    for p in range(N_PAIRS):
        o_ref[0, :, p * LANES:(p + 1) * LANES] = jnp.where(lane < HEAD_DIM, acc_s[2 * p], acc_s[2 * p + 1]).astype(BF16)


def _sb(q, k, v, *, tq, q_off):
    nb, rows, _ = q.shape
    tkk = k.shape[1]
    tk = ATT_BLOCK
    return pl.pallas_call(
        functools.partial(_sb_kernel, tq=tq, tk=tk, q_off=q_off, tail_widths=_tail_widths(tk)),
        grid=(nb, 1, rows // tq),
        in_specs=[pl.BlockSpec((1, tq, 512), lambda b, p, i: (b, i, 0)),
                  pl.BlockSpec((1, tkk, 512), lambda b, p, i: (b, 0, 0)),
                  pl.BlockSpec((1, tkk, 512), lambda b, p, i: (b, 0, 0))],
        out_specs=pl.BlockSpec((1, tq, 512), lambda b, p, i: (b, i, 0)),
        out_shape=jax.ShapeDtypeStruct((nb, rows, 512), BF16),
        scratch_shapes=[pltpu.VMEM((N_HEADS, tq, LANES), F32), pltpu.VMEM((N_HEADS, tq, LANES), F32),
                        pltpu.VMEM((ATT_SUB, ATT_SUB), BF16)],
        compiler_params=_params(3),
        name="attn_sb",
    )(q, k, v)


def _pair_queries(q_ref, mode, tq):
    lane = lax.broadcasted_iota(jnp.int32, (tq, LANES), 1)
    if mode == "mla":
        return [q_ref[0, :, h * LANES:(h + 1) * LANES] for h in range(2)], [slice(h * LANES, (h + 1) * LANES) for h in range(2)]
    q = q_ref[0]
    qs = [jnp.where((lane < HEAD_DIM) if h == 0 else (lane >= HEAD_DIM), q, jnp.zeros_like(q)) for h in range(2)]
    return qs, [slice(0, LANES)] * 2


def _store_heads_t(o_ref, o0, o1):
    o_ref[0] = jnp.concatenate([o0, o1], axis=0).T.astype(BF16)


def _flash_t_kernel(*refs, mode, tq, tk):
    if mode == "fox":
        q_ref, k_ref, vt_ref, cq_ref, ck_ref, o_ref, m_s, acc_s, s_buf = refs
    else:
        q_ref, k_ref, vt_ref, o_ref, m_s, acc_s, s_buf = refs
    n_full = pl.program_id(2)
    qs, kcols = _pair_queries(q_ref, mode, tq)
    m_s[...] = jnp.full((2, 1, tq), NEG_INF, F32)
    acc_s[...] = jnp.zeros((2, HEAD_DIM + ONES_ROWS, tq), F32)
    ones = jnp.ones((ONES_ROWS, tk), BF16)

    def scores(j, slot):
        r0 = pl.multiple_of(j * tk, tk)
        for h in range(2):
            s = _dot_nt(k_ref[0, pl.ds(r0, tk), kcols[h]], qs[h])
            if mode == "fox":
                s = s - _tile_lanes(ck_ref[0, h, pl.ds(r0, tk), :], tq // LANES)
            s_buf[slot, h] = s

    def update(j, slot, h, n_keys, q_lo, q_hi, masked):
        s = s_buf[slot, h, 0:n_keys, q_lo:q_hi]
        if masked:
            kpos = lax.broadcasted_iota(jnp.int32, s.shape, 0)
            qpos = q_lo + lax.broadcasted_iota(jnp.int32, s.shape, 1)
            if mode == "fox":
                mask = kpos <= qpos
            else:
                mask = lax.shift_right_logical(kpos, 6) <= lax.shift_right_logical(qpos, 6)
            s = jnp.where(mask, s, NEG_INF)
        m_prev = m_s[h, :, q_lo:q_hi]
        if mode == "fox":
            cq = cq_ref[0, 0, h:h + 1, q_lo:q_hi]
            m_next = jnp.maximum(m_prev, jnp.max(s, axis=0, keepdims=True) + cq)
            p = jnp.exp2(s - (m_next - cq))
        else:
            m_next = jnp.maximum(m_prev, jnp.max(s, axis=0, keepdims=True))
            p = jnp.exp2(s - m_next)
        alpha = jnp.exp2(m_prev - m_next)
        vt = jnp.concatenate([vt_ref[0, 0, j, h * HEAD_DIM:(h + 1) * HEAD_DIM, 0:n_keys], ones[:, 0:n_keys]], axis=0)
        acc_s[h, :, q_lo:q_hi] = alpha * acc_s[h, :, q_lo:q_hi] + _dot(vt, p.astype(BF16))
        m_s[h, :, q_lo:q_hi] = m_next

    def consume(j, slot, masked):
        for h in range(2):
            if masked:
                update(j, slot, h, tk // 2, 0, tq // 2, True)
                update(j, slot, h, tk, tq // 2, tq, True)
            else:
                update(j, slot, h, tk, 0, tq, False)

    scores(0, 0)

    def body(jj, carry):
        j = 2 * jj
        scores(j + 1, 1)
        consume(j, 0, False)
        scores(j + 2, 0)
        consume(j + 1, 1, False)
        return carry

    lax.fori_loop(0, n_full // 2, body, 0)

    @pl.when(n_full % 2 == 1)
    def _():
        scores(n_full, 1)
        consume(n_full - 1, 0, False)
        consume(n_full, 1, True)

    @pl.when(n_full % 2 == 0)
    def _():
        consume(n_full, 0, True)

    _store_heads_t(o_ref, *[acc_s[h, :HEAD_DIM] / acc_s[h, HEAD_DIM:HEAD_DIM + 1] for h in range(2)])


def _flash_t(mode, q, k, vt, cq=None, ck=None, *, tq):
    nb, rows, _ = q.shape
    tk = ATT_BLOCK
    assert tq == tk
    nkb = rows // tk
    kw = 2 * LANES if mode == "mla" else LANES
    in_specs = [pl.BlockSpec((1, tq, kw), lambda b, p, i: (b, i, p)),
                pl.BlockSpec((1, rows, kw), lambda b, p, i: (b, 0, p)),
                pl.BlockSpec((1, 1, nkb, LANES, tk), lambda b, p, i: (b, p, 0, 0, 0))]
    args = [q, k, vt]
    if mode == "fox":
        in_specs += [pl.BlockSpec((1, 1, 2, tq), lambda b, p, i: (b, p, 0, i)),
                     pl.BlockSpec((1, 2, rows, LANES), lambda b, p, i: (b, p, 0, 0))]
        args += [cq, ck]
    return pl.pallas_call(
        functools.partial(_flash_t_kernel, mode=mode, tq=tq, tk=tk),
        grid=(nb, N_PAIRS, rows // tq),
        in_specs=in_specs,
        out_specs=pl.BlockSpec((1, tq, LANES), lambda b, p, i: (b, i, p)),
        out_shape=jax.ShapeDtypeStruct((nb, rows, 512), BF16),
        scratch_shapes=[pltpu.VMEM((2, 1, tq), F32), pltpu.VMEM((2, HEAD_DIM + ONES_ROWS, tq), F32),
                        pltpu.VMEM((2, 2, tk, tq), F32)],
        compiler_params=_params(3),
        name="attn_" + mode,
    )(*args)


def _sb_t_kernel(q_ref, k_ref, vt_ref, o_ref, acc_s, r_s, ut_s, *, tq, tk):
    i = pl.program_id(2)
    n_diag = tq // tk
    nk = (i + 1) * n_diag
    ur = lax.broadcasted_iota(jnp.int32, (SB_CUM, 2 * SB_CUM), 0)
    uc = lax.broadcasted_iota(jnp.int32, (SB_CUM, 2 * SB_CUM), 1) & (SB_CUM - 1)
    ut_s[...] = jnp.where(uc > ur, 1.0, 0.0).astype(BF16)
    n_heads = 2 * SB_PAIRS
    lane = lax.broadcasted_iota(jnp.int32, (tq, LANES), 1)
    qs = []
    for hp in range(n_heads):
        q = q_ref[0, :, hp // 2 * LANES:(hp // 2 + 1) * LANES]
        qs.append(jnp.where((lane < HEAD_DIM) if hp % 2 == 0 else (lane >= HEAD_DIM), q, jnp.zeros_like(q)))
    acc_s[...] = jnp.zeros((n_heads, HEAD_DIM, tq), F32)
    r_s[...] = jnp.zeros((n_heads, 1, tq), F32)

    def step(j, masked, q_lo=0):
        nq = tq - q_lo
        r0 = pl.multiple_of(j * tk, tk)
        if masked:
            kpos = j * tk + lax.broadcasted_iota(jnp.int32, (tk, nq), 0)
            qpos = i * tq + q_lo + lax.broadcasted_iota(jnp.int32, (tk, nq), 1)
            mask = kpos < qpos
        ut = ut_s[...]
        for h in range(n_heads):
            k = k_ref[0, pl.ds(r0, tk), h // 2 * LANES:(h // 2 + 1) * LANES]
            z = _dot_nt(k, qs[h][q_lo:])
            lsz = jnp.minimum(z, 0.0) - jnp.log2(1.0 + jnp.exp2(-jnp.abs(z)))
            nlk = z - lsz
            if masked:
                nlk = jnp.where(mask, nlk, 0.0)
            r = r_s[h, :, q_lo:]
            ws = [None] * (tk // SB_CUM)
            for sb in reversed(range(tk // SB_CUM)):
                rows = slice(sb * SB_CUM, (sb + 1) * SB_CUM)
                hi, lo = _split_bf16(nlk[rows], 2)
                w = jnp.exp2(lsz[rows] - _dot(ut, jnp.concatenate([hi, lo], axis=0)) - r)
                if masked:
                    w = jnp.where(mask[rows], w, 0.0)
                ws[sb] = w.astype(BF16)
                r = r + jnp.sum(nlk[rows], axis=0, keepdims=True)
            vt = vt_ref[0, h // 2, j, h % 2 * HEAD_DIM:(h % 2 + 1) * HEAD_DIM, :]
            acc_s[h, :, q_lo:] = acc_s[h, :, q_lo:] + _dot(vt, jnp.concatenate(ws, axis=0))
            r_s[h, :, q_lo:] = r

    def diagonal():
        for d in range(n_diag):
            step(nk - 1 - d, True, q_lo=(n_diag - 1 - d) * tk)

    @pl.when(i == 0)
    def _():
        diagonal()

    @pl.when(i > 0)
    def _():
        diagonal()
        step(nk - 1 - n_diag, False)

    def r_min():
        return functools.reduce(jnp.minimum, [jnp.min(r_s[h]) for h in range(n_heads)])

    def cond(c):
        j, rmin = c
        return (j >= 0) & (rmin <= SB_UNDERFLOW_BITS)

    def body(c):
        j, _ = c
        step(j, False)
        return j - 1, r_min()

    lax.while_loop(cond, body, (nk - 1 - n_diag - jnp.minimum(i, 1), r_min()))
    for p in range(SB_PAIRS):
        pair = jnp.concatenate([acc_s[2 * p], acc_s[2 * p + 1]], axis=0)
        o_ref[0, :, p * LANES:(p + 1) * LANES] = pair.T.astype(BF16)


def _sb_t(q, k, vt, *, tq):
    nb, rows, _ = q.shape
    tk = vt.shape[-1]
    w = SB_PAIRS * LANES
    return pl.pallas_call(
        functools.partial(_sb_t_kernel, tq=tq, tk=tk),
        grid=(nb, N_PAIRS // SB_PAIRS, rows // tq),
        in_specs=[pl.BlockSpec((1, tq, w), lambda b, p, i: (b, i, p)),
                  pl.BlockSpec((1, rows, w), lambda b, p, i: (b, 0, p)),
                  pl.BlockSpec((1, SB_PAIRS, rows // tk, LANES, tk), lambda b, p, i: (b, p, 0, 0, 0))],
        out_specs=pl.BlockSpec((1, tq, w), lambda b, p, i: (b, i, p)),
        out_shape=jax.ShapeDtypeStruct((nb, rows, 512), BF16),
        scratch_shapes=[pltpu.VMEM((2 * SB_PAIRS, HEAD_DIM, tq), F32), pltpu.VMEM((2 * SB_PAIRS, 1, tq), F32),
                        pltpu.VMEM((SB_CUM, 2 * SB_CUM), BF16)],
        compiler_params=_params(3),
        name="attn_sb",
    )(q, k, vt)


def _bias_kernel(tab_ref, o_ref, *, nq, nk, q0, k0):
    h = pl.program_id(0)
    qpos = q0 + lax.broadcasted_iota(jnp.int32, (nq, nk), 0)
    kpos = k0 + lax.broadcasted_iota(jnp.int32, (nq, nk), 1)
    qc = lax.shift_right_arithmetic(qpos, 6)
    kc = lax.shift_right_arithmetic(kpos, 6)
    band = (kc <= qc) & (kc >= qc - BAND_CHUNKS)
    width = pl.next_power_of_2(nq + nk)
    x = lax.broadcasted_iota(jnp.int32, (8, width), 1)
    idx = jnp.clip(q0 - k0 + nq - x, -REL_CLIP, REL_CLIP) + REL_CLIP

    def body(r, acc):
        return jnp.where(idx == r, tab_ref[r * N_HEADS + h], acc)

    ext = lax.fori_loop(0, 2 * REL_CLIP + 1, body, jnp.zeros((8, width), F32))
    lines = jnp.broadcast_to(ext[0:1], (nq, width))
    bias = pltpu.roll(lines, width - nq, 1, stride=1, stride_axis=0)[:, :nk]
    o_ref[0] = jnp.where(band, bias * LOG2E, NEG_INF)


def _band_bias(rel_table, nq, nk, q0, k0):
    return pl.pallas_call(
        functools.partial(_bias_kernel, nq=nq, nk=nk, q0=q0, k0=k0),
        grid=(N_HEADS,),
        in_specs=[pl.BlockSpec(memory_space=pltpu.SMEM)],
        out_specs=pl.BlockSpec((1, nq, nk), lambda h: (h, 0, 0)),
        out_shape=jax.ShapeDtypeStruct((N_HEADS, nq, nk), F32),
        compiler_params=_params(1),
        name="band_bias",
    )(rel_table.reshape(-1))


def _band_kernel(q_ref, k_ref, v_ref, bias_ref, o_ref, *, tq, n_sub, win, pad_rows):
    lane = lax.broadcasted_iota(jnp.int32, (tq, LANES), 1)
    col = lax.broadcasted_iota(jnp.int32, (tq, win), 1)
    for c in range(n_sub):
        i = pl.program_id(2) * n_sub + c
        start = pl.multiple_of(i * tq, tq)
        k = k_ref[0, pl.ds(start, win), :]
        v = v_ref[0, pl.ds(start, win), :]
        v1 = jnp.concatenate([v, jnp.ones_like(v)], axis=1)
        valid = col >= pad_rows - i * tq
        q = q_ref[0, c * tq:(c + 1) * tq, :]
        outs = []
        for h in range(2):
            qh = jnp.where((lane < HEAD_DIM) if h == 0 else (lane >= HEAD_DIM), q, jnp.zeros_like(q))
            s = jnp.where(valid, _dot_nt(qh, k) + bias_ref[h], NEG_INF)
            p = jnp.exp2(s - jnp.max(s, axis=1, keepdims=True))
            pv = _dot(p.astype(BF16), v1)
            outs.append(pv[:, :LANES] / pv[:, LANES:])
        o_ref[0, c * tq:(c + 1) * tq, :] = jnp.where(lane < HEAD_DIM, outs[0], outs[1]).astype(BF16)


def _band(q, k_pad, v_pad, bias, *, tq, n_sub, pad_rows):
    nb, rows, _ = q.shape
    tkk = k_pad.shape[1]
    win = bias.shape[2]
    tqq = tq * n_sub
    return pl.pallas_call(
        functools.partial(_band_kernel, tq=tq, n_sub=n_sub, win=win, pad_rows=pad_rows),
        grid=(nb, N_PAIRS, rows // tqq),
        in_specs=[pl.BlockSpec((1, tqq, LANES), lambda b, p, i: (b, i, p)),
                  pl.BlockSpec((1, tkk, LANES), lambda b, p, i: (b, 0, p)),
                  pl.BlockSpec((1, tkk, LANES), lambda b, p, i: (b, 0, p)),
                  pl.BlockSpec((2, tq, win), lambda b, p, i: (p, 0, 0))],
        out_specs=pl.BlockSpec((1, tqq, LANES), lambda b, p, i: (b, i, p)),
        out_shape=jax.ShapeDtypeStruct((nb, rows, 512), BF16),
        compiler_params=_params(3),
        name="attn_band",
    )(q, k_pad, v_pad, bias)


def _in_cd_kernel(x_ref, sh_ref, sc_ref, gpre_ref, w_ref,
                  qc_ref, kc16_ref, vc16_ref, qd_ref, kd16_ref, vd16_ref, kc32_ref, vc32_ref, kd32_ref, vd32_ref,
                  vdt_ref=None):
    x = x_ref[0]
    h = _rms(x, gpre_ref[...]) * (1.0 + sc_ref[0]) + sh_ref[0]
    y = _dot(h.astype(BF16), w_ref[...])
    qc_ref[0] = (y[:, 0:512] * QK_SCALE_64).astype(BF16)
    qd_ref[0] = (y[:, 1536:2048] * QK_SCALE_64).astype(BF16)
    for lo, r32, r16 in ((512, kc32_ref, kc16_ref), (1024, vc32_ref, vc16_ref),
                         (2048, kd32_ref, kd16_ref), (2560, vd32_ref, vd16_ref)):
        part = y[:, lo:lo + 512]
        r32[0] = part
        r16[0] = part.astype(BF16)
    if vdt_ref is not None:
        _store_vt(vdt_ref, y[:, 2560:3072])


def _in_cd(x, sh, sc, gpre, w, tm, with_vt):
    nb, rows, _ = x.shape
    row3 = lambda wd: pl.BlockSpec((1, tm, wd), lambda b, t: (b, t, 0))
    mod = pl.BlockSpec((1, 1, D_MODEL), lambda b, t: (b, 0, 0))
    const = lambda a: pl.BlockSpec(a.shape, lambda b, t: (0,) * a.ndim)
    out_specs = [row3(512)] * 10
    out_shape = [jax.ShapeDtypeStruct((nb, rows, 512), BF16)] * 6 + [jax.ShapeDtypeStruct((nb, rows, 512), F32)] * 4
    if with_vt:
        vt_shape, vt_spec = _vt_out(nb, rows, tm, ATT_SUB)
        out_shape.append(vt_shape)
        out_specs.append(vt_spec)
    return pl.pallas_call(
        _in_cd_kernel,
        grid=(nb, rows // tm),
        in_specs=[row3(D_MODEL), mod, mod, const(gpre), const(w)],
        out_specs=out_specs,
        out_shape=out_shape,
        compiler_params=_params(2),
        name="in_cd",
    )(x, sh, sc, gpre, w)


def _ffn_kernel(o1_ref, o2_ref, wo1_ref, wo2_ref, gate_m_ref, gpost_m_ref,
                x_ref, sh_ref, sc_ref, gate_ref, gpre_ref, gpost_ref, prev_ref, wg_ref, wu_ref, cw_ref, cb_ref, wd_ref,
                xo_ref, conv_ref, gbuf_ref, *, tm):
    t = pl.program_id(1)
    o = _dot(o1_ref[0], wo1_ref[...]) + _dot(o2_ref[0], wo2_ref[...])
    x = x_ref[0] + gate_m_ref[0] * _rms(o, gpost_m_ref[...])
    h = (_rms(x, gpre_ref[...]) * (1.0 + sc_ref[0]) + sh_ref[0]).astype(BF16)

    @pl.when(t == 0)
    def _():
        gbuf_ref[0:8, :] = jnp.zeros((8, D_FF), F32)
        gbuf_ref[8 - (CONV_W - 1):8, :] = prev_ref[0]

    acc = jnp.zeros((tm, D_MODEL), F32)
    for c in range(N_FF_CHUNKS):
        cols = slice(c * FF_CHUNK, (c + 1) * FF_CHUNK)
        g = _dot(h, wg_ref[c])
        u = _dot(h, wu_ref[c])
        gbuf_ref[8:8 + tm, cols] = g
        g1 = gbuf_ref[7:7 + tm, cols]
        g2 = gbuf_ref[6:6 + tm, cols]
        cw = cw_ref[c]
        gc = cb_ref[c] + (cw[0:1] * g2 + cw[1:2] * g1 + cw[2:3] * g)
        a = (gc * jax.nn.sigmoid(gc)) * u
        acc = acc + _dot(a.astype(BF16), wd_ref[c])
    conv_ref[0] = gbuf_ref[tm + 6:tm + 8, :]
    gbuf_ref[0:8, :] = gbuf_ref[tm:tm + 8, :]
    xo_ref[0] = x + gate_ref[0] * _rms(acc, gpost_ref[...])


def _out_ffn(o1, o2, wo1, wo2, gate_m, gpost_m, x, sh, sc, gate, gpre, gpost, prev, P, tm):
    nb, rows, _ = x.shape
    row3 = lambda wd: pl.BlockSpec((1, tm, wd), lambda b, t: (b, t, 0))
    mod = pl.BlockSpec((1, 1, D_MODEL), lambda b, t: (b, 0, 0))
    const = lambda a: pl.BlockSpec(a.shape, lambda b, t: (0,) * a.ndim)
    conv = pl.BlockSpec((1, CONV_W - 1, D_FF), lambda b, t: (b, 0, 0))
    consts = [P["wg"], P["wu"], P["cw"], P["cb"], P["wd"]]
    resident = lambda a: pl.BlockSpec(a.shape, lambda b, t: (0,) * a.ndim, pipeline_mode=pl.Buffered(1))
    return pl.pallas_call(
        functools.partial(_ffn_kernel, tm=tm),
        grid=(nb, rows // tm),
        in_specs=[row3(512), row3(512), const(wo1), const(wo2), mod, const(gpost_m),
                  row3(D_MODEL), mod, mod, mod, const(gpre), const(gpost), conv] + [resident(a) for a in consts],
        out_specs=[row3(D_MODEL), conv],
        out_shape=[jax.ShapeDtypeStruct((nb, rows, D_MODEL), F32),
                   jax.ShapeDtypeStruct((nb, CONV_W - 1, D_FF), F32)],
        scratch_shapes=[pltpu.VMEM((tm + 8, D_FF), F32)],
        compiler_params=_params(2),
        name="out_ffn",
    )(o1, o2, wo1, wo2, gate_m, gpost_m, x, sh, sc, gate, gpre, gpost, prev, *consts)


def _rot_half_cols(w):
    half = ROPE_DIM // 2
    return jnp.concatenate([-w[..., half:], w[..., :half]], axis=-1)


def _prep_ab(w_in_ab, b_f, q_a_g, kv_a_g, w_uq, w_ukv, w_out_ab):
    z = lambda n: jnp.zeros((D_MODEL, n), F32)
    c_q, c_kv, k_r = w_in_ab[:, 0:256], w_in_ab[:, 256:384], w_in_ab[:, 384:416]
    q_b, k_b, v_b, f_b = w_in_ab[:, 416:928], w_in_ab[:, 928:1440], w_in_ab[:, 1440:1952], w_in_ab[:, 1952:1960]
    w_ab = jnp.concatenate([c_q, c_kv, k_r, z(LANES - ROPE_DIM), _rot_half_cols(k_r), z(LANES - ROPE_DIM),
                            q_b, k_b, v_b, f_b, z(LANES - N_HEADS)], axis=1).astype(BF16)
    w_ft = jnp.concatenate([f_b.T, jnp.zeros((N_HEADS, D_MODEL), F32)], axis=0).astype(BF16)
    uq = w_uq.reshape(Q_RANK, N_HEADS, NOPE_DIM + ROPE_DIM)
    nope, rope_w = uq[..., :NOPE_DIM], uq[..., NOPE_DIM:]
    zq = lambda n: jnp.zeros((Q_RANK, N_HEADS, n), F32)
    wq_main = jnp.concatenate([nope, rope_w, zq(LANES - NOPE_DIM - ROPE_DIM)], axis=-1)
    wq_rot = jnp.concatenate([zq(NOPE_DIM), _rot_half_cols(rope_w), zq(LANES - NOPE_DIM - ROPE_DIM)], axis=-1)
    wq = jnp.concatenate([wq_main.reshape(Q_RANK, -1), wq_rot.reshape(Q_RANK, -1)], axis=1).astype(BF16)
    ukv = w_ukv.reshape(KV_RANK, N_HEADS, 2 * HEAD_DIM)
    wk = jnp.concatenate([ukv[..., :NOPE_DIM], jnp.zeros((KV_RANK, N_HEADS, LANES - NOPE_DIM), F32)], axis=-1)
    place = np.zeros((LANES, N_HEADS, LANES), np.float32)
    for r in range(ROPE_DIM):
        place[r, :, NOPE_DIM + r] = 1.0
    wkv = jnp.concatenate([
        jnp.concatenate([wk.reshape(KV_RANK, -1), ukv[..., NOPE_DIM:].reshape(KV_RANK, -1)], axis=1),
        jnp.concatenate([jnp.asarray(place.reshape(LANES, -1)), jnp.zeros((LANES, 512), F32)], axis=1),
    ], axis=0).astype(BF16)
    return {
        "w_ab": w_ab, "w_ft": w_ft, "wq": wq, "wkv": wkv,
        "bf128": jnp.concatenate([b_f, jnp.zeros((LANES - N_HEADS,), F32)]).reshape(1, LANES),
        "bft": b_f.reshape(N_HEADS, 1),
        "gq": q_a_g.reshape(1, Q_RANK), "gkv": kv_a_g.reshape(1, KV_RANK),
        "wo1": w_out_ab[:512].astype(BF16), "wo2": w_out_ab[512:].astype(BF16),
    }


def _prep_ffn(w_gate, w_up, conv_w, conv_b, w_down):
    chunk_cols = lambda w: w.reshape(D_MODEL, N_FF_CHUNKS, FF_CHUNK).transpose(1, 0, 2).astype(BF16)
    return {
        "wg": chunk_cols(w_gate), "wu": chunk_cols(w_up),
        "cw": conv_w.reshape(CONV_W, N_FF_CHUNKS, FF_CHUNK).transpose(1, 0, 2),
        "cb": conv_b.reshape(N_FF_CHUNKS, 1, FF_CHUNK),
        "wd": w_down.reshape(N_FF_CHUNKS, FF_CHUNK, D_MODEL).astype(BF16),
    }


def _rope_tables(pos):
    half = ROPE_DIM // 2
    inv = ROPE_BASE ** (-jnp.arange(half, dtype=F32) / half)
    ang = pos.astype(F32)[:, None] * inv[None, :]
    cos2 = jnp.concatenate([jnp.cos(ang)] * 2, axis=1)
    sin2 = jnp.concatenate([jnp.sin(ang)] * 2, axis=1)
    n = pos.shape[0]
    scale = QK_SCALE_A
    zeros = lambda w: jnp.zeros((n, w), F32)
    tqc = scale * jnp.concatenate([jnp.ones((n, NOPE_DIM), F32), cos2, zeros(LANES - NOPE_DIM - ROPE_DIM)], axis=1)
    tqs = scale * jnp.concatenate([zeros(NOPE_DIM), sin2, zeros(LANES - NOPE_DIM - ROPE_DIM)], axis=1)
    tkc = jnp.concatenate([cos2, zeros(LANES - ROPE_DIM)], axis=1)
    tks = jnp.concatenate([sin2, zeros(LANES - ROPE_DIM)], axis=1)
    return tqc, tqs, tkc, tks


def _pad_rows(a, front, total):
    return jnp.pad(a, ((0, 0), (front, total - front - a.shape[1])) + ((0, 0),) * (a.ndim - 2))


def _trunk(x, mods, past, PA, PF, w_cd, wo_cd, rel, gains, tm, tq):
    nb, rows, _ = x.shape
    q_off = 0 if past is None else PAST_LEN
    pos = q_off + jnp.arange(rows)
    tabs = _rope_tables(pos)
    mix_pre_g, mix_post_g, ffn_pre_g, ffn_post_g = gains
    out = {}

    def mod6(l):
        m = mods[l].reshape(nb, 6, 1, D_MODEL)
        return [m[:, j] for j in range(6)]

    sh_m, sc_m, g_m, sh_f, sc_f, g_f = mod6(0)
    prompt = past is None
    (qcomb, ckv, krope, ckr, qb, kb16, vb16, kb32, vb32, logf, logf128, logft, *vbt) = _in_ab(
        x, sh_m, sc_m, mix_pre_g[0:1], PA, tabs, tm, prompt)
    out["a_ckv"], out["a_krope"], out["b_k"], out["b_v"], out["b_logf"] = ckv, krope, kb32, vb32, logf
    if prompt:
        tkk = rows
        ckr_all, kb_all, vb_all, lf_all, lft_all = ckr, kb16, vb16, logf128, logft
    else:
        tkk = -(-(PAST_LEN + rows) // ATT_BLOCK) * ATT_BLOCK
        cat = lambda p, n: _pad_rows(jnp.concatenate([p, n], axis=1), 0, tkk)
        ckr_past = jnp.concatenate([past["a_ckv"].astype(BF16),
                                    jnp.pad(past["a_krope"].astype(BF16), ((0, 0), (0, 0), (0, LANES - ROPE_DIM)))],
                                   axis=-1)
        ckr_all = cat(ckr_past, ckr)
        kb_all = cat(past["b_k"].reshape(nb, PAST_LEN, 512).astype(BF16), kb16)
        vb_all = cat(past["b_v"].reshape(nb, PAST_LEN, 512).astype(BF16), vb16)
        lf_all = cat(jnp.pad(past["b_logf"], ((0, 0), (0, 0), (0, LANES - N_HEADS))), logf128)
        lft_all = jnp.pad(jnp.concatenate([jnp.swapaxes(past["b_logf"], 1, 2), logft], axis=2),
                          ((0, 0), (0, 0), (0, tkk - PAST_LEN - rows)))
    if prompt:
        kcomb, vat = _kvup(ckr_all, PA["wkv"], ATT_BLOCK, True)
        _, cumt, ck_rep = _cum(lf_all, lft_all, True)
        o_a = _flash_t("mla", qcomb, kcomb, vat, tq=ATT_BLOCK)
        o_b = _flash_t("fox", qb, kb_all, vbt[0], cumt.reshape(nb, N_PAIRS, 2, rows), ck_rep, tq=ATT_BLOCK)
    else:
        kcomb, va = _kvup(ckr_all, PA["wkv"], tkk, False)
        cum, cumt = _cum(lf_all, lft_all, False)
        cq = cum[:, q_off:q_off + rows, :N_HEADS].reshape(nb, rows, N_PAIRS, 2).transpose(0, 2, 1, 3)
        nkb = tkk // ATT_BLOCK
        ckt = cumt.reshape(nb, N_PAIRS, 2, nkb, ATT_BLOCK).transpose(0, 1, 3, 2, 4)
        o_a = _flash("mla", qcomb, kcomb, va, tq=tq, q_off=q_off, n_valid=q_off + rows)
        o_b = _flash("fox", qb, kb_all, vb_all, cq, ckt, tq=tq, q_off=q_off, n_valid=q_off + rows)
    prev = jnp.zeros((nb, CONV_W - 1, D_FF), F32) if past is None else past["ffn_conv"][0]
    x, conv0 = _out_ffn(o_a, o_b, PA["wo1"], PA["wo2"], g_m, mix_post_g[0:1],
                        x, sh_f, sc_f, g_f, ffn_pre_g[0:1], ffn_post_g[0:1], prev, PF[0], tm)

    sh_m, sc_m, g_m, sh_f, sc_f, g_f = mod6(1)
    qc, kc16, vc16, qd, kd16, vd16, kc32, vc32, kd32, vd32, *vdt = _in_cd(x, sh_m, sc_m, mix_pre_g[1:2], w_cd, tm,
                                                                          prompt)
    out["d_k"], out["d_v"] = kd32, vd32
    if prompt:
        keep = min(BAND_CHUNKS * CHUNK, rows)
        out["c_k"], out["c_v"] = kc32[:, rows - keep:], vc32[:, rows - keep:]
        front = BAND_CHUNKS * CHUNK
        kc_all = _pad_rows(kc16, front, front + rows)
        vc_all = _pad_rows(vc16, front, front + rows)
        bias = _band_bias(rel, tq, front + tq, 0, -front)
        o_c = _band(qc, kc_all, vc_all, bias, tq=tq, n_sub=16, pad_rows=front)
        o_d = _sb_t(qd, kd16, vdt[0], tq=ATT_BLOCK)
    else:
        out["c_k"], out["c_v"] = kc32, vc32
        n_c = past["c_k"].shape[1]
        front = BAND_SAMPLE_WIN - n_c - rows
        kc_all = _pad_rows(jnp.concatenate([past["c_k"].reshape(nb, n_c, 512).astype(BF16), kc16], axis=1), front,
                           BAND_SAMPLE_WIN)
        vc_all = _pad_rows(jnp.concatenate([past["c_v"].reshape(nb, n_c, 512).astype(BF16), vc16], axis=1), front,
                           BAND_SAMPLE_WIN)
        bias = _band_bias(rel, rows, BAND_SAMPLE_WIN, q_off, PAST_LEN - n_c - front)
        o_c = _band(qc, kc_all, vc_all, bias, tq=rows, n_sub=1, pad_rows=0)
        kd_all = _pad_rows(jnp.concatenate([past["d_k"].reshape(nb, PAST_LEN, 512).astype(BF16), kd16], axis=1), 0, tkk)
        vd_all = _pad_rows(jnp.concatenate([past["d_v"].reshape(nb, PAST_LEN, 512).astype(BF16), vd16], axis=1), 0, tkk)
        o_d = _sb(qd, kd_all, vd_all, tq=tq, q_off=q_off)
    prev = jnp.zeros((nb, CONV_W - 1, D_FF), F32) if past is None else past["ffn_conv"][1]
    x, conv1 = _out_ffn(o_c, o_d, wo_cd[0], wo_cd[1], g_m, mix_post_g[1:2],
                        x, sh_f, sc_f, g_f, ffn_pre_g[1:2], ffn_post_g[1:2], prev, PF[1], tm)
    out["ffn_conv"] = jnp.stack([conv0, conv1])
    return x, out


def kernel(x_prompt, x_sample, c_prompt, c_sample, cache_a_ckv, cache_a_krope, cache_b_k, cache_b_v, cache_b_logf, cache_c_k, cache_c_v, cache_d_k, cache_d_v, state_ffn_conv, ada_w, ada_b, mix_pre_g, mix_post_g, ffn_pre_g, ffn_post_g, w_in_ab, b_f, q_a_g, kv_a_g, w_uq, w_ukv, w_out_ab, w_in_cd, rel_bias_c, w_out_cd, ffn_w_gate, ffn_w_up, ffn_conv_w, ffn_conv_b, ffn_w_down):
    nbp, nbs = x_prompt.shape[0], x_sample.shape[0]
    c_all = jnp.concatenate([c_prompt, c_sample, jnp.zeros((32 - nbp - nbs, D_MODEL), F32)], axis=0)
    mods = _ada(c_all, ada_w, ada_b)
    PA = _prep_ab(w_in_ab[0], b_f[0], q_a_g[0], kv_a_g[0], w_uq[0], w_ukv[0], w_out_ab[0])
    PF = [_prep_ffn(ffn_w_gate[l], ffn_w_up[l], ffn_conv_w[l], ffn_conv_b[l], ffn_w_down[l]) for l in range(DEPTH)]
    w_cd = w_in_cd[0].astype(BF16)
    wo_cd = (w_out_cd[0][:512].astype(BF16), w_out_cd[0][512:].astype(BF16))
    gains = (mix_pre_g, mix_post_g, ffn_pre_g, ffn_post_g)
    past = {"a_ckv": cache_a_ckv[0], "a_krope": cache_a_krope[0], "b_k": cache_b_k[0], "b_v": cache_b_v[0],
            "b_logf": cache_b_logf[0], "c_k": cache_c_k[0], "c_v": cache_c_v[0], "d_k": cache_d_k[0],
            "d_v": cache_d_v[0], "ffn_conv": state_ffn_conv}
    y_p, sp = _trunk(x_prompt, mods[:, :nbp], None, PA, PF, w_cd, wo_cd, rel_bias_c[0], gains, 512, ATT_SUB)
    y_s, ss = _trunk(x_sample, mods[:, nbp:nbp + nbs], past, PA, PF, w_cd, wo_cd, rel_bias_c[0], gains,
                     x_sample.shape[1], x_sample.shape[1])

    def heads(a):
        return a.reshape(a.shape[0], a.shape[1], N_HEADS, HEAD_DIM)[None]

    return (y_p, y_s,
            sp["a_ckv"][None], ss["a_ckv"][None], sp["a_krope"][None], ss["a_krope"][None],
            heads(sp["b_k"]), heads(ss["b_k"]), heads(sp["b_v"]), heads(ss["b_v"]),
            sp["b_logf"][None], ss["b_logf"][None],
            heads(sp["c_k"]), heads(ss["c_k"]), heads(sp["c_v"]), heads(ss["c_v"]),
            heads(sp["d_k"]), heads(ss["d_k"]), heads(sp["d_v"]), heads(ss["d_v"]),
            sp["ffn_conv"], ss["ffn_conv"])
```

```python
import functools

import jax
import jax.numpy as jnp
import numpy as np
from jax import lax
from jax.experimental import pallas as pl
from jax.experimental.pallas import tpu as pltpu

D_MODEL = 1024
DEPTH = 2
PAST_LEN = 1024
CHUNK = 64
HEAD_DIM = 64
N_HEADS = 8
N_PAIRS = N_HEADS // 2
Q_RANK = 256
KV_RANK = 128
NOPE_DIM = 64
ROPE_DIM = 32
ROPE_BASE = 10000.0
BAND_CHUNKS = 8
REL_CLIP = 128
D_FF = 2816
CONV_W = 3
EPS = 1e-6
NEG_INF = -1e30
LOG2E = 1.4426950408889634
QK_SCALE_64 = HEAD_DIM ** -0.5 * LOG2E
QK_SCALE_A = (NOPE_DIM + ROPE_DIM) ** -0.5 * LOG2E
SB_UNDERFLOW_BITS = 160.0
SB_PAIRS = 2
SB_CUM = 128
ONES_ROWS = 16

LANES = 128
FF_CHUNK = 2816
N_FF_CHUNKS = D_FF // FF_CHUNK
ATT_BLOCK = 512
ATT_SUB = 256
BAND_SAMPLE_WIN = 640
VMEM_LIMIT_BYTES = 56 * 1024 * 1024

F32 = jnp.float32
BF16 = jnp.bfloat16

AB_CQ, AB_CKV, AB_KR, AB_KRROT, AB_QB, AB_KB, AB_VB, AB_FB, AB_COLS = 0, 256, 384, 512, 640, 1152, 1664, 2176, 2304


def _params(n_axes, fuse_inputs=None):
    return pltpu.CompilerParams(dimension_semantics=("arbitrary",) * n_axes, vmem_limit_bytes=VMEM_LIMIT_BYTES,
                                allow_input_fusion=fuse_inputs)


def _rms(x, g):
    return x * lax.rsqrt(jnp.mean(x * x, axis=-1, keepdims=True) + EPS) * g


def _log_sigmoid(x):
    return jnp.minimum(x, 0.0) - jnp.log1p(jnp.exp(-jnp.abs(x)))


def _log2_sigmoids(z2):
    l2 = jnp.log2(1.0 + jnp.exp2(-jnp.abs(z2)))
    return jnp.minimum(z2, 0.0) - l2, jnp.maximum(z2, 0.0) + l2


def _dot(a, b):
    return jnp.dot(a, b, preferred_element_type=F32)


def _dot_nt(a, b):
    return lax.dot_general(a, b, (((1,), (1,)), ((), ())), preferred_element_type=F32)


def _split_bf16(x, n):
    parts = []
    for _ in range(n):
        p = x.astype(BF16)
        parts.append(p)
        x = x - p.astype(F32)
    return parts


def _tile_lanes(x, n):
    return x if n == 1 else jnp.concatenate([x] * n, axis=1)


def _ada_kernel(c_ref, w_ref, b_ref, o_ref):
    c = c_ref[...]
    cond = c * jax.nn.sigmoid(c)
    o_ref[0] = _dot(cond.astype(BF16), w_ref[0].astype(BF16)) + b_ref[0]


def _ada(c_all, ada_w, ada_b):
    rows = c_all.shape[0]
    tn = 1536
    return pl.pallas_call(
        _ada_kernel,
        grid=(DEPTH, 6 * D_MODEL // tn),
        in_specs=[
            pl.BlockSpec((rows, D_MODEL), lambda l, j: (0, 0)),
            pl.BlockSpec((1, D_MODEL, tn), lambda l, j: (l, 0, j)),
            pl.BlockSpec((1, 1, tn), lambda l, j: (l, 0, j)),
        ],
        out_specs=pl.BlockSpec((1, rows, tn), lambda l, j: (l, 0, j)),
        out_shape=jax.ShapeDtypeStruct((DEPTH, rows, 6 * D_MODEL), F32),
        compiler_params=_params(2),
        name="ada",
    )(c_all, ada_w, ada_b.reshape(DEPTH, 1, 6 * D_MODEL))


def _store_vt(vt_ref, v):
    vt = v.T
    blk = vt_ref.shape[-1]
    for p in range(N_PAIRS):
        for c in range(vt_ref.shape[2]):
            vt_ref[0, p, c] = vt[p * LANES:(p + 1) * LANES, c * blk:(c + 1) * blk].astype(BF16)


def _in_ab_kernel(x_ref, sh_ref, sc_ref, gpre_ref, w_ref, wft_ref, bf_ref, bft_ref, gq_ref, gkv_ref, wq_ref,
                  tqc_ref, tqs_ref, tkc_ref, tks_ref,
                  qcomb_ref, ckv_ref, krope_ref, ckr_ref, qb_ref, kb16_ref, vb16_ref, kb32_ref, vb32_ref,
                  logf_ref, logf128_ref, logft_ref, vbt_ref=None):
    x = x_ref[0]
    h = _rms(x, gpre_ref[...]) * (1.0 + sc_ref[0]) + sh_ref[0]
    hb = h.astype(BF16)
    y = _dot(hb, w_ref[...])
    cq = _rms(y[:, AB_CQ:AB_CQ + Q_RANK], gq_ref[...]).astype(BF16)
    qa = _dot(cq, wq_ref[...])
    tqc = tqc_ref[...]
    tqs = tqs_ref[...]
    half = N_HEADS * LANES
    for hh in range(N_HEADS):
        lo = hh * LANES
        qcomb_ref[0, :, lo:lo + LANES] = (qa[:, lo:lo + LANES] * tqc + qa[:, half + lo:half + lo + LANES] * tqs).astype(BF16)
    ckv = _rms(y[:, AB_CKV:AB_CKV + KV_RANK], gkv_ref[...])
    ckv_ref[0] = ckv
    kr = y[:, AB_KR:AB_KR + LANES] * tkc_ref[...] + y[:, AB_KRROT:AB_KRROT + LANES] * tks_ref[...]
    krope_ref[0] = kr[:, :ROPE_DIM]
    ckr_ref[0, :, 0:LANES] = ckv.astype(BF16)
    ckr_ref[0, :, LANES:2 * LANES] = kr.astype(BF16)
    qb_ref[0] = (y[:, AB_QB:AB_QB + 512] * QK_SCALE_64).astype(BF16)
    kb = y[:, AB_KB:AB_KB + 512]
    vb = y[:, AB_VB:AB_VB + 512]
    kb32_ref[0] = kb
    vb32_ref[0] = vb
    kb16_ref[0] = kb.astype(BF16)
    vb16_ref[0] = vb.astype(BF16)
    if vbt_ref is not None:
        _store_vt(vbt_ref, vb)
    logf = _log_sigmoid(y[:, AB_FB:AB_FB + LANES] + bf_ref[...])
    logf128_ref[0] = logf
    logf_ref[0] = logf[:, :N_HEADS]
    ft = _dot_nt(wft_ref[...], hb)
    logft_ref[0] = _log_sigmoid(ft[:N_HEADS] + bft_ref[...])


def _vt_out(nb, rows, tm, blk=None):
    blk = tm if blk is None else blk
    return (jax.ShapeDtypeStruct((nb, N_PAIRS, rows // blk, LANES, blk), BF16),
            pl.BlockSpec((1, N_PAIRS, tm // blk, LANES, blk), lambda b, t: (b, 0, t, 0, 0)))


def _in_ab(x, sh, sc, gpre, P, tabs, tm, with_vt):
    nb, rows, _ = x.shape
    nt = rows // tm
    row3 = lambda w: pl.BlockSpec((1, tm, w), lambda b, t: (b, t, 0))
    mod = pl.BlockSpec((1, 1, D_MODEL), lambda b, t: (b, 0, 0))
    const = lambda a: pl.BlockSpec(a.shape, lambda b, t: (0,) * a.ndim)
    tab = pl.BlockSpec((tm, LANES), lambda b, t: (t, 0))
    out_shapes = [
        jax.ShapeDtypeStruct((nb, rows, N_HEADS * LANES), BF16),
        jax.ShapeDtypeStruct((nb, rows, KV_RANK), F32),
        jax.ShapeDtypeStruct((nb, rows, ROPE_DIM), F32),
        jax.ShapeDtypeStruct((nb, rows, 2 * LANES), BF16),
        jax.ShapeDtypeStruct((nb, rows, 512), BF16),
        jax.ShapeDtypeStruct((nb, rows, 512), BF16),
        jax.ShapeDtypeStruct((nb, rows, 512), BF16),
        jax.ShapeDtypeStruct((nb, rows, 512), F32),
        jax.ShapeDtypeStruct((nb, rows, 512), F32),
        jax.ShapeDtypeStruct((nb, rows, N_HEADS), F32),
        jax.ShapeDtypeStruct((nb, rows, LANES), F32),
        jax.ShapeDtypeStruct((nb, N_HEADS, rows), F32),
    ]
    out_specs = [row3(N_HEADS * LANES), row3(KV_RANK), row3(ROPE_DIM), row3(2 * LANES), row3(512), row3(512),
                 row3(512), row3(512), row3(512), row3(N_HEADS), row3(LANES),
                 pl.BlockSpec((1, N_HEADS, tm), lambda b, t: (b, 0, t))]
    if with_vt:
        vt_shape, vt_spec = _vt_out(nb, rows, tm)
        out_shapes.append(vt_shape)
        out_specs.append(vt_spec)
    consts = [P["w_ab"], P["w_ft"], P["bf128"], P["bft"], P["gq"], P["gkv"], P["wq"]]
    return pl.pallas_call(
        _in_ab_kernel,
        grid=(nb, nt),
        in_specs=[row3(D_MODEL), mod, mod, const(gpre)] + [const(a) for a in consts] + [tab] * 4,
        out_specs=out_specs,
        out_shape=out_shapes,
        compiler_params=_params(2),
        name="in_ab",
    )(x, sh, sc, gpre, *consts, *tabs)


def _kvup_kernel(ckr_ref, w_ref, k_ref, v_ref):
    y = _dot(ckr_ref[0], w_ref[...])
    k_ref[0] = y[:, :N_HEADS * LANES].astype(BF16)
    v_ref[0] = y[:, N_HEADS * LANES:].astype(BF16)


def _kvup_t_kernel(ckr_ref, w_ref, k_ref, vt_ref):
    y = _dot(ckr_ref[0], w_ref[...])
    k_ref[0] = y[:, :N_HEADS * LANES].astype(BF16)
    _store_vt(vt_ref, y[:, N_HEADS * LANES:])


def _kvup(ckr, w, tm, with_vt):
    nb, rows, _ = ckr.shape
    if with_vt:
        v_shape, v_spec = _vt_out(nb, rows, tm)
    else:
        v_shape = jax.ShapeDtypeStruct((nb, rows, 512), BF16)
        v_spec = pl.BlockSpec((1, tm, 512), lambda b, t: (b, t, 0))
    return pl.pallas_call(
        _kvup_t_kernel if with_vt else _kvup_kernel,
        grid=(nb, rows // tm),
        in_specs=[pl.BlockSpec((1, tm, 2 * LANES), lambda b, t: (b, t, 0)),
                  pl.BlockSpec(w.shape, lambda b, t: (0, 0))],
        out_specs=[pl.BlockSpec((1, tm, N_HEADS * LANES), lambda b, t: (b, t, 0)), v_spec],
        out_shape=[jax.ShapeDtypeStruct((nb, rows, N_HEADS * LANES), BF16), v_shape],
        compiler_params=_params(2),
        name="kvup",
    )(ckr, w)


def _cum_kernel(x_ref, xt_ref, c_ref, ct_ref, rep_ref=None, *, n_chunks, tc):
    r = lax.broadcasted_iota(jnp.int32, (tc, tc), 0)
    c = lax.broadcasted_iota(jnp.int32, (tc, tc), 1)
    lower = jnp.where(c <= r, 1.0, 0.0).astype(BF16)
    upper = jnp.where(r <= c, 1.0, 0.0).astype(BF16)
    carry = jnp.zeros((1, LANES), F32)
    carry_t = jnp.zeros((N_HEADS, 1), F32)
    for ci in range(n_chunks):
        sl = slice(ci * tc, (ci + 1) * tc)
        cs = carry
        for p in _split_bf16(x_ref[0, sl, :], 3):
            cs = cs + _dot(lower, p)
        c_ref[0, sl, :] = cs * LOG2E
        if rep_ref is not None:
            for h in range(N_HEADS):
                rep_ref[0, h, sl, :] = jnp.broadcast_to(cs[:, h:h + 1] * LOG2E, (tc, LANES))
        carry = cs[tc - 1:tc, :]
        xt = jnp.concatenate([xt_ref[0, :, sl], jnp.zeros((N_HEADS, tc), F32)], axis=0)
        cst = jnp.zeros((2 * N_HEADS, tc), F32)
        for p in _split_bf16(xt, 3):
            cst = cst + _dot(p, upper)
        cst = cst[:N_HEADS] + carry_t
        ct_ref[0, :, sl] = cst * LOG2E
        carry_t = cst[:, tc - 1:tc]


def _cum(logf128, logft, with_rep):
    nb, tk, _ = logf128.shape
    tc = ATT_SUB
    out_specs = [pl.BlockSpec((1, tk, LANES), lambda b: (b, 0, 0)),
                 pl.BlockSpec((1, N_HEADS, tk), lambda b: (b, 0, 0))]
    out_shape = [jax.ShapeDtypeStruct((nb, tk, LANES), F32), jax.ShapeDtypeStruct((nb, N_HEADS, tk), F32)]
    if with_rep:
        out_specs.append(pl.BlockSpec((1, N_HEADS, tk, LANES), lambda b: (b, 0, 0, 0)))
        out_shape.append(jax.ShapeDtypeStruct((nb, N_HEADS, tk, LANES), F32))
    return pl.pallas_call(
        functools.partial(_cum_kernel, n_chunks=tk // tc, tc=tc),
        grid=(nb,),
        in_specs=[pl.BlockSpec((1, tk, LANES), lambda b: (b, 0, 0)),
                  pl.BlockSpec((1, N_HEADS, tk), lambda b: (b, 0, 0))],
        out_specs=out_specs,
        out_shape=out_shape,
        compiler_params=_params(1),
        name="cum",
    )(logf128, logft)


def _flash_kernel(*refs, mode, tq, tk, q_off, n_valid, tail_widths):
    if mode == "fox":
        q_ref, k_ref, v_ref, cq_ref, ckt_ref, o_ref, m_s, l_s, acc_s = refs
    else:
        q_ref, k_ref, v_ref, o_ref, m_s, l_s, acc_s = refs
    i = pl.program_id(2)
    qpos0 = q_off + i * tq
    n_full = qpos0 // tk
    rem = qpos0 + tq - n_full * tk
    lane = lax.broadcasted_iota(jnp.int32, (tq, LANES), 1)

    qs, kcols, cq_tiles = [], [], []
    for hp in range(N_HEADS):
        p, h = divmod(hp, 2)
        pair = slice(p * LANES, (p + 1) * LANES)
        if mode == "fox":
            q = q_ref[0, :, pair]
            qs.append(jnp.where((lane < HEAD_DIM) if h == 0 else (lane >= HEAD_DIM), q, jnp.zeros_like(q)))
            kcols.append(pair)
            cq_tiles.append(jnp.broadcast_to(cq_ref[0, p][:, h:h + 1], (tq, LANES)))
        else:
            qs.append(q_ref[0, :, hp * LANES:(hp + 1) * LANES])
            kcols.append(slice(hp * LANES, (hp + 1) * LANES))
        m_s[hp] = jnp.full((tq, LANES), NEG_INF, F32)
        l_s[hp] = jnp.zeros((tq, LANES), F32)
        acc_s[hp] = jnp.zeros((tq, LANES), F32)

    def step(j, width, masked):
        nrep = width // LANES
        r0 = pl.multiple_of(j * tk, tk)
        if masked:
            kpos = n_full * tk + lax.broadcasted_iota(jnp.int32, (tq, width), 1)
            qpos = qpos0 + lax.broadcasted_iota(jnp.int32, (tq, width), 0)
            if mode == "fox":
                mask = kpos <= qpos
            else:
                mask = (lax.shift_right_logical(kpos, 6) <= lax.shift_right_logical(qpos, 6)) & (kpos < n_valid)
        for hp in range(N_HEADS):
            p, h = divmod(hp, 2)
            s = _dot_nt(qs[hp], k_ref[0, pl.ds(r0, width), kcols[hp]])
            if mode == "fox":
                s = s + (_tile_lanes(cq_tiles[hp], nrep) - ckt_ref[0, p, j][h:h + 1, :width])
            if masked:
                s = jnp.where(mask, s, NEG_INF)
            m_prev = m_s[hp]
            m_next = jnp.maximum(m_prev, jnp.max(s, axis=1, keepdims=True))
            pr = jnp.exp2(s - _tile_lanes(m_next, nrep))
            alpha = jnp.exp2(m_prev - m_next)
            l_s[hp] = alpha * l_s[hp] + jnp.sum(pr, axis=1, keepdims=True)
            v = v_ref[0, pl.ds(r0, width), p * LANES:(p + 1) * LANES]
            acc_s[hp] = alpha * acc_s[hp] + _dot(pr.astype(BF16), v)
            m_s[hp] = m_next

    def body(j, carry):
        step(j, tk, False)
        return carry

    lax.fori_loop(0, n_full, body, 0)
    if len(tail_widths) == 1:
        step(n_full, tail_widths[0], True)
    else:
        lo_w, hi_w = tail_widths
        pl.when(rem <= lo_w)(lambda: step(n_full, lo_w, True))
        pl.when(rem > lo_w)(lambda: step(n_full, hi_w, True))

    for p in range(N_PAIRS):
        o0 = acc_s[2 * p] / l_s[2 * p]
        o1 = acc_s[2 * p + 1] / l_s[2 * p + 1]
        o_ref[0, :, p * LANES:(p + 1) * LANES] = jnp.where(lane < HEAD_DIM, o0, o1).astype(BF16)


def _tail_widths(tk):
    return (ATT_SUB, tk) if tk > ATT_SUB else (tk,)


def _flash(mode, q, k, v, cq=None, ckt=None, *, tq, q_off, n_valid):
    nb, rows, _ = q.shape
    tkk = k.shape[1]
    tk = ATT_BLOCK
    kw = q.shape[2]
    in_specs = [pl.BlockSpec((1, tq, kw), lambda b, p, i: (b, i, 0)),
                pl.BlockSpec((1, tkk, kw), lambda b, p, i: (b, 0, 0)),
                pl.BlockSpec((1, tkk, 512), lambda b, p, i: (b, 0, 0))]
    args = [q, k, v]
    if mode == "fox":
        in_specs += [pl.BlockSpec((1, N_PAIRS, tq, 2), lambda b, p, i: (b, 0, i, 0)),
                     pl.BlockSpec((1, N_PAIRS, tkk // tk, 2, tk), lambda b, p, i: (b, 0, 0, 0, 0))]
        args += [cq, ckt]
    return pl.pallas_call(
        functools.partial(_flash_kernel, mode=mode, tq=tq, tk=tk, q_off=q_off, n_valid=n_valid,
                          tail_widths=_tail_widths(tk)),
        grid=(nb, 1, rows // tq),
        in_specs=in_specs,
        out_specs=pl.BlockSpec((1, tq, 512), lambda b, p, i: (b, i, 0)),
        out_shape=jax.ShapeDtypeStruct((nb, rows, 512), BF16),
        scratch_shapes=[pltpu.VMEM((N_HEADS, tq, LANES), F32)] * 3,
        compiler_params=_params(3),
        name="attn_" + mode,
    )(*args)


def _sb_kernel(q_ref, k_ref, v_ref, o_ref, acc_s, r_s, u_s, *, tq, tk, q_off, tail_widths):
    i = pl.program_id(2)
    qpos0 = q_off + i * tq
    n_full = qpos0 // tk
    rem = qpos0 + tq - n_full * tk
    nrep = ATT_SUB // LANES
    lane = lax.broadcasted_iota(jnp.int32, (tq, LANES), 1)
    ur = lax.broadcasted_iota(jnp.int32, (ATT_SUB, ATT_SUB), 0)
    uc = lax.broadcasted_iota(jnp.int32, (ATT_SUB, ATT_SUB), 1)
    u_s[...] = jnp.where(ur > uc, 1.0, 0.0).astype(BF16)
    qs = []
    for hp in range(N_HEADS):
        p, h = divmod(hp, 2)
        q = q_ref[0, :, p * LANES:(p + 1) * LANES]
        qs.append(jnp.where((lane < HEAD_DIM) if h == 0 else (lane >= HEAD_DIM), q, jnp.zeros_like(q)))
    acc_s[...] = jnp.zeros((N_HEADS, tq, LANES), F32)
    r_s[...] = jnp.zeros((N_HEADS, tq, LANES), F32)

    def step(j, width, masked):
        r0 = pl.multiple_of(j * tk, tk)
        if masked:
            kpos = n_full * tk + lax.broadcasted_iota(jnp.int32, (tq, width), 1)
            qpos = qpos0 + lax.broadcasted_iota(jnp.int32, (tq, width), 0)
            mask = kpos < qpos
        u = u_s[...]
        for h in range(N_HEADS):
            pair = slice(h // 2 * LANES, (h // 2 + 1) * LANES)
            k = k_ref[0, pl.ds(r0, width), pair]
            v = v_ref[0, pl.ds(r0, width), pair]
            z = _dot_nt(qs[h], k)
            lsz, nlk = _log2_sigmoids(z)
            if masked:
                nlk = jnp.where(mask, nlk, 0.0)
            r = r_s[h]
            ws = []
            for sb in reversed(range(width // ATT_SUB)):
                cols = slice(sb * ATT_SUB, (sb + 1) * ATT_SUB)
                nlk_sb = nlk[:, cols]
                hi, lo = _split_bf16(nlk_sb, 2)
                after = _dot(hi, u) + _dot(lo, u) + _tile_lanes(r, nrep)
                w = jnp.exp2(lsz[:, cols] - after)
                if masked:
                    w = jnp.where(mask[:, cols], w, 0.0)
                ws.append(w.astype(BF16))
                r = r + jnp.sum(nlk_sb, axis=1, keepdims=True)
            w_all = ws[0] if len(ws) == 1 else jnp.concatenate(ws[::-1], axis=1)
            acc_s[h] = acc_s[h] + _dot(w_all, v)
            r_s[h] = r

    if len(tail_widths) == 1:
        step(n_full, tail_widths[0], True)
    else:
        lo_w, hi_w = tail_widths
        pl.when(rem <= lo_w)(lambda: step(n_full, lo_w, True))
        pl.when(rem > lo_w)(lambda: step(n_full, hi_w, True))

    def body(jj, carry):
        step(n_full - 1 - jj, tk, False)
        return carry

    lax.fori_loop(0, n_full, body, 0)
    for p in range(N_PAIRS):
        o_ref[0, :, p * LANES:(p + 1) * LANES] = jnp.where(lane < HEAD_DIM, acc_s[2 * p], acc_s[2 * p + 1]).astype(BF16)


def _sb(q, k, v, *, tq, q_off):
    nb, rows, _ = q.shape
    tkk = k.shape[1]
    tk = ATT_BLOCK
    return pl.pallas_call(
        functools.partial(_sb_kernel, tq=tq, tk=tk, q_off=q_off, tail_widths=_tail_widths(tk)),
        grid=(nb, 1, rows // tq),
        in_specs=[pl.BlockSpec((1, tq, 512), lambda b, p, i: (b, i, 0)),
                  pl.BlockSpec((1, tkk, 512), lambda b, p, i: (b, 0, 0)),
                  pl.BlockSpec((1, tkk, 512), lambda b, p, i: (b, 0, 0))],
        out_specs=pl.BlockSpec((1, tq, 512), lambda b, p, i: (b, i, 0)),
        out_shape=jax.ShapeDtypeStruct((nb, rows, 512), BF16),
        scratch_shapes=[pltpu.VMEM((N_HEADS, tq, LANES), F32), pltpu.VMEM((N_HEADS, tq, LANES), F32),
                        pltpu.VMEM((ATT_SUB, ATT_SUB), BF16)],
        compiler_params=_params(3),
        name="attn_sb",
    )(q, k, v)


def _pair_queries(q_ref, mode, tq):
    lane = lax.broadcasted_iota(jnp.int32, (tq, LANES), 1)
    if mode == "mla":
        return [q_ref[0, :, h * LANES:(h + 1) * LANES] for h in range(2)], [slice(h * LANES, (h + 1) * LANES) for h in range(2)]
    q = q_ref[0]
    qs = [jnp.where((lane < HEAD_DIM) if h == 0 else (lane >= HEAD_DIM), q, jnp.zeros_like(q)) for h in range(2)]
    return qs, [slice(0, LANES)] * 2


def _store_heads_t(o_ref, o0, o1):
    o_ref[0] = jnp.concatenate([o0, o1], axis=0).T.astype(BF16)


def _flash_t_kernel(*refs, mode, tq, tk):
    if mode == "fox":
        q_ref, k_ref, vt_ref, cq_ref, ck_ref, o_ref, m_s, acc_s, s_buf = refs
    else:
        q_ref, k_ref, vt_ref, o_ref, m_s, acc_s, s_buf = refs
    n_full = pl.program_id(2)
    qs, kcols = _pair_queries(q_ref, mode, tq)
    m_s[...] = jnp.full((2, 1, tq), NEG_INF, F32)
    acc_s[...] = jnp.zeros((2, HEAD_DIM + ONES_ROWS, tq), F32)
    ones = jnp.ones((ONES_ROWS, tk), BF16)

    def scores(j, slot):
        r0 = pl.multiple_of(j * tk, tk)
        for h in range(2):
            s = _dot_nt(k_ref[0, pl.ds(r0, tk), kcols[h]], qs[h])
            if mode == "fox":
                s = s - _tile_lanes(ck_ref[0, h, pl.ds(r0, tk), :], tq // LANES)
            s_buf[slot, h] = s

    def update(j, slot, h, n_keys, q_lo, q_hi, masked):
        s = s_buf[slot, h, 0:n_keys, q_lo:q_hi]
        if masked:
            kpos = lax.broadcasted_iota(jnp.int32, s.shape, 0)
            qpos = q_lo + lax.broadcasted_iota(jnp.int32, s.shape, 1)
            if mode == "fox":
                mask = kpos <= qpos
            else:
                mask = lax.shift_right_logical(kpos, 6) <= lax.shift_right_logical(qpos, 6)
            s = jnp.where(mask, s, NEG_INF)
        m_prev = m_s[h, :, q_lo:q_hi]
        if mode == "fox":
            cq = cq_ref[0, 0, h:h + 1, q_lo:q_hi]
            m_next = jnp.maximum(m_prev, jnp.max(s, axis=0, keepdims=True) + cq)
            p = jnp.exp2(s - (m_next - cq))
        else:
            m_next = jnp.maximum(m_prev, jnp.max(s, axis=0, keepdims=True))
            p = jnp.exp2(s - m_next)
        alpha = jnp.exp2(m_prev - m_next)
        vt = jnp.concatenate([vt_ref[0, 0, j, h * HEAD_DIM:(h + 1) * HEAD_DIM, 0:n_keys], ones[:, 0:n_keys]], axis=0)
        acc_s[h, :, q_lo:q_hi] = alpha * acc_s[h, :, q_lo:q_hi] + _dot(vt, p.astype(BF16))
        m_s[h, :, q_lo:q_hi] = m_next

    def consume(j, slot, masked):
        for h in range(2):
            if masked:
                update(j, slot, h, tk // 2, 0, tq // 2, True)
                update(j, slot, h, tk, tq // 2, tq, True)
            else:
                update(j, slot, h, tk, 0, tq, False)

    scores(0, 0)

    def body(jj, carry):
        j = 2 * jj
        scores(j + 1, 1)
        consume(j, 0, False)
        scores(j + 2, 0)
        consume(j + 1, 1, False)
        return carry

    lax.fori_loop(0, n_full // 2, body, 0)

    @pl.when(n_full % 2 == 1)
    def _():
        scores(n_full, 1)
        consume(n_full - 1, 0, False)
        consume(n_full, 1, True)

    @pl.when(n_full % 2 == 0)
    def _():
        consume(n_full, 0, True)

    _store_heads_t(o_ref, *[acc_s[h, :HEAD_DIM] / acc_s[h, HEAD_DIM:HEAD_DIM + 1] for h in range(2)])


def _flash_t(mode, q, k, vt, cq=None, ck=None, *, tq):
    nb, rows, _ = q.shape
    tk = ATT_BLOCK
    assert tq == tk
    nkb = rows // tk
    kw = 2 * LANES if mode == "mla" else LANES
    in_specs = [pl.BlockSpec((1, tq, kw), lambda b, p, i: (b, i, p)),
                pl.BlockSpec((1, rows, kw), lambda b, p, i: (b, 0, p)),
                pl.BlockSpec((1, 1, nkb, LANES, tk), lambda b, p, i: (b, p, 0, 0, 0))]
    args = [q, k, vt]
    if mode == "fox":
        in_specs += [pl.BlockSpec((1, 1, 2, tq), lambda b, p, i: (b, p, 0, i)),
                     pl.BlockSpec((1, 2, rows, LANES), lambda b, p, i: (b, p, 0, 0))]
        args += [cq, ck]
    return pl.pallas_call(
        functools.partial(_flash_t_kernel, mode=mode, tq=tq, tk=tk),
        grid=(nb, N_PAIRS, rows // tq),
        in_specs=in_specs,
        out_specs=pl.BlockSpec((1, tq, LANES), lambda b, p, i: (b, i, p)),
        out_shape=jax.ShapeDtypeStruct((nb, rows, 512), BF16),
        scratch_shapes=[pltpu.VMEM((2, 1, tq), F32), pltpu.VMEM((2, HEAD_DIM + ONES_ROWS, tq), F32),
                        pltpu.VMEM((2, 2, tk, tq), F32)],
        compiler_params=_params(3),
        name="attn_" + mode,
    )(*args)


def _sb_t_kernel(q_ref, k_ref, vt_ref, o_ref, acc_s, r_s, ut_s, *, tq, tk):
    i = pl.program_id(2)
    n_diag = tq // tk
    nk = (i + 1) * n_diag
    ur = lax.broadcasted_iota(jnp.int32, (SB_CUM, 2 * SB_CUM), 0)
    uc = lax.broadcasted_iota(jnp.int32, (SB_CUM, 2 * SB_CUM), 1) & (SB_CUM - 1)
    ut_s[...] = jnp.where(uc > ur, 1.0, 0.0).astype(BF16)
    n_heads = 2 * SB_PAIRS
    lane = lax.broadcasted_iota(jnp.int32, (tq, LANES), 1)
    qs = []
    for hp in range(n_heads):
        q = q_ref[0, :, hp // 2 * LANES:(hp // 2 + 1) * LANES]
        qs.append(jnp.where((lane < HEAD_DIM) if hp % 2 == 0 else (lane >= HEAD_DIM), q, jnp.zeros_like(q)))
    acc_s[...] = jnp.zeros((n_heads, HEAD_DIM, tq), F32)
    r_s[...] = jnp.zeros((n_heads, 1, tq), F32)

    def step(j, masked, q_lo=0):
        nq = tq - q_lo
        r0 = pl.multiple_of(j * tk, tk)
        if masked:
            kpos = j * tk + lax.broadcasted_iota(jnp.int32, (tk, nq), 0)
            qpos = i * tq + q_lo + lax.broadcasted_iota(jnp.int32, (tk, nq), 1)
            mask = kpos < qpos
        ut = ut_s[...]
        for h in range(n_heads):
            k = k_ref[0, pl.ds(r0, tk), h // 2 * LANES:(h // 2 + 1) * LANES]
            z = _dot_nt(k, qs[h][q_lo:])
            lsz = jnp.minimum(z, 0.0) - jnp.log2(1.0 + jnp.exp2(-jnp.abs(z)))
            nlk = z - lsz
            if masked:
                nlk = jnp.where(mask, nlk, 0.0)
            r = r_s[h, :, q_lo:]
            ws = [None] * (tk // SB_CUM)
            for sb in reversed(range(tk // SB_CUM)):
                rows = slice(sb * SB_CUM, (sb + 1) * SB_CUM)
                hi, lo = _split_bf16(nlk[rows], 2)
                w = jnp.exp2(lsz[rows] - _dot(ut, jnp.concatenate([hi, lo], axis=0)) - r)
                if masked:
                    w = jnp.where(mask[rows], w, 0.0)
                ws[sb] = w.astype(BF16)
                r = r + jnp.sum(nlk[rows], axis=0, keepdims=True)
            vt = vt_ref[0, h // 2, j, h % 2 * HEAD_DIM:(h % 2 + 1) * HEAD_DIM, :]
            acc_s[h, :, q_lo:] = acc_s[h, :, q_lo:] + _dot(vt, jnp.concatenate(ws, axis=0))
            r_s[h, :, q_lo:] = r

    def diagonal():
        for d in range(n_diag):
            step(nk - 1 - d, True, q_lo=(n_diag - 1 - d) * tk)

    @pl.when(i == 0)
    def _():
        diagonal()

    @pl.when(i > 0)
    def _():
        diagonal()
        step(nk - 1 - n_diag, False)

    def r_min():
        return functools.reduce(jnp.minimum, [jnp.min(r_s[h]) for h in range(n_heads)])

    def cond(c):
        j, rmin = c
        return (j >= 0) & (rmin <= SB_UNDERFLOW_BITS)

    def body(c):
        j, _ = c
        step(j, False)
        return j - 1, r_min()

    lax.while_loop(cond, body, (nk - 1 - n_diag - jnp.minimum(i, 1), r_min()))
    for p in range(SB_PAIRS):
        pair = jnp.concatenate([acc_s[2 * p], acc_s[2 * p + 1]], axis=0)
        o_ref[0, :, p * LANES:(p + 1) * LANES] = pair.T.astype(BF16)


def _sb_t(q, k, vt, *, tq):
    nb, rows, _ = q.shape
    tk = vt.shape[-1]
    w = SB_PAIRS * LANES
    return pl.pallas_call(
        functools.partial(_sb_t_kernel, tq=tq, tk=tk),
        grid=(nb, N_PAIRS // SB_PAIRS, rows // tq),
        in_specs=[pl.BlockSpec((1, tq, w), lambda b, p, i: (b, i, p)),
                  pl.BlockSpec((1, rows, w), lambda b, p, i: (b, 0, p)),
                  pl.BlockSpec((1, SB_PAIRS, rows // tk, LANES, tk), lambda b, p, i: (b, p, 0, 0, 0))],
        out_specs=pl.BlockSpec((1, tq, w), lambda b, p, i: (b, i, p)),
        out_shape=jax.ShapeDtypeStruct((nb, rows, 512), BF16),
        scratch_shapes=[pltpu.VMEM((2 * SB_PAIRS, HEAD_DIM, tq), F32), pltpu.VMEM((2 * SB_PAIRS, 1, tq), F32),
                        pltpu.VMEM((SB_CUM, 2 * SB_CUM), BF16)],
        compiler_params=_params(3),
        name="attn_sb",
    )(q, k, vt)


def _bias_kernel(tab_ref, o_ref, *, nq, nk, q0, k0):
    h = pl.program_id(0)
    qpos = q0 + lax.broadcasted_iota(jnp.int32, (nq, nk), 0)
    kpos = k0 + lax.broadcasted_iota(jnp.int32, (nq, nk), 1)
    qc = lax.shift_right_arithmetic(qpos, 6)
    kc = lax.shift_right_arithmetic(kpos, 6)
    band = (kc <= qc) & (kc >= qc - BAND_CHUNKS)
    width = pl.next_power_of_2(nq + nk)
    x = lax.broadcasted_iota(jnp.int32, (8, width), 1)
    idx = jnp.clip(q0 - k0 + nq - x, -REL_CLIP, REL_CLIP) + REL_CLIP

    def body(r, acc):
        return jnp.where(idx == r, tab_ref[r * N_HEADS + h], acc)

    ext = lax.fori_loop(0, 2 * REL_CLIP + 1, body, jnp.zeros((8, width), F32))
    lines = jnp.broadcast_to(ext[0:1], (nq, width))
    bias = pltpu.roll(lines, width - nq, 1, stride=1, stride_axis=0)[:, :nk]
    o_ref[0] = jnp.where(band, bias * LOG2E, NEG_INF)


def _band_bias(rel_table, nq, nk, q0, k0):
    return pl.pallas_call(
        functools.partial(_bias_kernel, nq=nq, nk=nk, q0=q0, k0=k0),
        grid=(N_HEADS,),
        in_specs=[pl.BlockSpec(memory_space=pltpu.SMEM)],
        out_specs=pl.BlockSpec((1, nq, nk), lambda h: (h, 0, 0)),
        out_shape=jax.ShapeDtypeStruct((N_HEADS, nq, nk), F32),
        compiler_params=_params(1),
        name="band_bias",
    )(rel_table.reshape(-1))


def _band_kernel(q_ref, k_ref, v_ref, bias_ref, o_ref, *, tq, n_sub, win, pad_rows):
    lane = lax.broadcasted_iota(jnp.int32, (tq, LANES), 1)
    col = lax.broadcasted_iota(jnp.int32, (tq, win), 1)
    for c in range(n_sub):
        i = pl.program_id(2) * n_sub + c
        start = pl.multiple_of(i * tq, tq)
        k = k_ref[0, pl.ds(start, win), :]
        v = v_ref[0, pl.ds(start, win), :]
        v1 = jnp.concatenate([v, jnp.ones_like(v)], axis=1)
        valid = col >= pad_rows - i * tq
        q = q_ref[0, c * tq:(c + 1) * tq, :]
        outs = []
        for h in range(2):
            qh = jnp.where((lane < HEAD_DIM) if h == 0 else (lane >= HEAD_DIM), q, jnp.zeros_like(q))
            s = jnp.where(valid, _dot_nt(qh, k) + bias_ref[h], NEG_INF)
            p = jnp.exp2(s - jnp.max(s, axis=1, keepdims=True))
            pv = _dot(p.astype(BF16), v1)
            outs.append(pv[:, :LANES] / pv[:, LANES:])
        o_ref[0, c * tq:(c + 1) * tq, :] = jnp.where(lane < HEAD_DIM, outs[0], outs[1]).astype(BF16)


def _band(q, k_pad, v_pad, bias, *, tq, n_sub, pad_rows):
    nb, rows, _ = q.shape
    tkk = k_pad.shape[1]
    win = bias.shape[2]
    tqq = tq * n_sub
    return pl.pallas_call(
        functools.partial(_band_kernel, tq=tq, n_sub=n_sub, win=win, pad_rows=pad_rows),
        grid=(nb, N_PAIRS, rows // tqq),
        in_specs=[pl.BlockSpec((1, tqq, LANES), lambda b, p, i: (b, i, p)),
                  pl.BlockSpec((1, tkk, LANES), lambda b, p, i: (b, 0, p)),
                  pl.BlockSpec((1, tkk, LANES), lambda b, p, i: (b, 0, p)),
                  pl.BlockSpec((2, tq, win), lambda b, p, i: (p, 0, 0))],
        out_specs=pl.BlockSpec((1, tqq, LANES), lambda b, p, i: (b, i, p)),
        out_shape=jax.ShapeDtypeStruct((nb, rows, 512), BF16),
        compiler_params=_params(3),
        name="attn_band",
    )(q, k_pad, v_pad, bias)


def _in_cd_kernel(x_ref, sh_ref, sc_ref, gpre_ref, w_ref,
                  qc_ref, kc16_ref, vc16_ref, qd_ref, kd16_ref, vd16_ref, kc32_ref, vc32_ref, kd32_ref, vd32_ref,
                  vdt_ref=None):
    x = x_ref[0]
    h = _rms(x, gpre_ref[...]) * (1.0 + sc_ref[0]) + sh_ref[0]
    y = _dot(h.astype(BF16), w_ref[...])
    qc_ref[0] = (y[:, 0:512] * QK_SCALE_64).astype(BF16)
    qd_ref[0] = (y[:, 1536:2048] * QK_SCALE_64).astype(BF16)
    for lo, r32, r16 in ((512, kc32_ref, kc16_ref), (1024, vc32_ref, vc16_ref),
                         (2048, kd32_ref, kd16_ref), (2560, vd32_ref, vd16_ref)):
        part = y[:, lo:lo + 512]
        r32[0] = part
        r16[0] = part.astype(BF16)
    if vdt_ref is not None:
        _store_vt(vdt_ref, y[:, 2560:3072])


def _in_cd(x, sh, sc, gpre, w, tm, with_vt):
    nb, rows, _ = x.shape
    row3 = lambda wd: pl.BlockSpec((1, tm, wd), lambda b, t: (b, t, 0))
    mod = pl.BlockSpec((1, 1, D_MODEL), lambda b, t: (b, 0, 0))
    const = lambda a: pl.BlockSpec(a.shape, lambda b, t: (0,) * a.ndim)
    out_specs = [row3(512)] * 10
    out_shape = [jax.ShapeDtypeStruct((nb, rows, 512), BF16)] * 6 + [jax.ShapeDtypeStruct((nb, rows, 512), F32)] * 4
    if with_vt:
        vt_shape, vt_spec = _vt_out(nb, rows, tm, ATT_SUB)
        out_shape.append(vt_shape)
        out_specs.append(vt_spec)
    return pl.pallas_call(
        _in_cd_kernel,
        grid=(nb, rows // tm),
        in_specs=[row3(D_MODEL), mod, mod, const(gpre), const(w)],
        out_specs=out_specs,
        out_shape=out_shape,
        compiler_params=_params(2, [False, False, False, False, True]),
        name="in_cd",
    )(x, sh, sc, gpre, w)


def _ffn_kernel(o1_ref, o2_ref, wo1_ref, wo2_ref, gate_m_ref, gpost_m_ref,
                x_ref, sh_ref, sc_ref, gate_ref, gpre_ref, gpost_ref, prev_ref, wg_ref, wu_ref, cw_ref, cb_ref, wd_ref,
                xo_ref, conv_ref, gbuf_ref, *, tm):
    t = pl.program_id(1)
    o = _dot(o1_ref[0], wo1_ref[...]) + _dot(o2_ref[0], wo2_ref[...])
    x = x_ref[0] + gate_m_ref[0] * _rms(o, gpost_m_ref[...])
    h = (_rms(x, gpre_ref[...]) * (1.0 + sc_ref[0]) + sh_ref[0]).astype(BF16)

    @pl.when(t == 0)
    def _():
        gbuf_ref[0:8, :] = jnp.zeros((8, D_FF), F32)
        gbuf_ref[8 - (CONV_W - 1):8, :] = prev_ref[0]

    acc = jnp.zeros((tm, D_MODEL), F32)
    for c in range(N_FF_CHUNKS):
        cols = slice(c * FF_CHUNK, (c + 1) * FF_CHUNK)
        g = _dot(h, wg_ref[c])
        u = _dot(h, wu_ref[c])
        gbuf_ref[8:8 + tm, cols] = g
        g1 = gbuf_ref[7:7 + tm, cols]
        g2 = gbuf_ref[6:6 + tm, cols]
        cw = cw_ref[c]
        gc = cb_ref[c] + (cw[0:1] * g2 + cw[1:2] * g1 + cw[2:3] * g)
        a = (gc * jax.nn.sigmoid(gc)) * u
        acc = acc + _dot(a.astype(BF16), wd_ref[c])
    conv_ref[0] = gbuf_ref[tm + 6:tm + 8, :]
    gbuf_ref[0:8, :] = gbuf_ref[tm:tm + 8, :]
    xo_ref[0] = x + gate_ref[0] * _rms(acc, gpost_ref[...])


def _out_ffn(o1, o2, wo1, wo2, gate_m, gpost_m, x, sh, sc, gate, gpre, gpost, prev, P, tm):
    nb, rows, _ = x.shape
    row3 = lambda wd: pl.BlockSpec((1, tm, wd), lambda b, t: (b, t, 0))
    mod = pl.BlockSpec((1, 1, D_MODEL), lambda b, t: (b, 0, 0))
    const = lambda a: pl.BlockSpec(a.shape, lambda b, t: (0,) * a.ndim)
    conv = pl.BlockSpec((1, CONV_W - 1, D_FF), lambda b, t: (b, 0, 0))
    consts = [P["wg"], P["wu"], P["cw"], P["cb"], P["wd"]]
    resident = lambda a: pl.BlockSpec(a.shape, lambda b, t: (0,) * a.ndim, pipeline_mode=pl.Buffered(1))
    return pl.pallas_call(
        functools.partial(_ffn_kernel, tm=tm),
        grid=(nb, rows // tm),
        in_specs=[row3(512), row3(512), const(wo1), const(wo2), mod, const(gpost_m),
                  row3(D_MODEL), mod, mod, mod, const(gpre), const(gpost), conv] + [resident(a) for a in consts],
        out_specs=[row3(D_MODEL), conv],
        out_shape=[jax.ShapeDtypeStruct((nb, rows, D_MODEL), F32),
                   jax.ShapeDtypeStruct((nb, CONV_W - 1, D_FF), F32)],
        scratch_shapes=[pltpu.VMEM((tm + 8, D_FF), F32)],
        compiler_params=_params(2, [k in (2, 3, 13, 14, 17) for k in range(18)]),
        name="out_ffn",
    )(o1, o2, wo1, wo2, gate_m, gpost_m, x, sh, sc, gate, gpre, gpost, prev, *consts)


def _rot_half_cols(w):
    half = ROPE_DIM // 2
    return jnp.concatenate([-w[..., half:], w[..., :half]], axis=-1)


def _prep_ab(w_in_ab, b_f, q_a_g, kv_a_g, w_uq, w_ukv, w_out_ab):
    z = lambda n: jnp.zeros((D_MODEL, n), F32)
    c_q, c_kv, k_r = w_in_ab[:, 0:256], w_in_ab[:, 256:384], w_in_ab[:, 384:416]
    q_b, k_b, v_b, f_b = w_in_ab[:, 416:928], w_in_ab[:, 928:1440], w_in_ab[:, 1440:1952], w_in_ab[:, 1952:1960]
    w_ab = jnp.concatenate([c_q, c_kv, k_r, z(LANES - ROPE_DIM), _rot_half_cols(k_r), z(LANES - ROPE_DIM),
                            q_b, k_b, v_b, f_b, z(LANES - N_HEADS)], axis=1).astype(BF16)
    w_ft = jnp.concatenate([f_b.T, jnp.zeros((N_HEADS, D_MODEL), F32)], axis=0).astype(BF16)
    uq = w_uq.reshape(Q_RANK, N_HEADS, NOPE_DIM + ROPE_DIM)
    nope, rope_w = uq[..., :NOPE_DIM], uq[..., NOPE_DIM:]
    zq = lambda n: jnp.zeros((Q_RANK, N_HEADS, n), F32)
    wq_main = jnp.concatenate([nope, rope_w, zq(LANES - NOPE_DIM - ROPE_DIM)], axis=-1)
    wq_rot = jnp.concatenate([zq(NOPE_DIM), _rot_half_cols(rope_w), zq(LANES - NOPE_DIM - ROPE_DIM)], axis=-1)
    wq = jnp.concatenate([wq_main.reshape(Q_RANK, -1), wq_rot.reshape(Q_RANK, -1)], axis=1).astype(BF16)
    ukv = w_ukv.reshape(KV_RANK, N_HEADS, 2 * HEAD_DIM)
    wk = jnp.concatenate([ukv[..., :NOPE_DIM], jnp.zeros((KV_RANK, N_HEADS, LANES - NOPE_DIM), F32)], axis=-1)
    place = np.zeros((LANES, N_HEADS, LANES), np.float32)
    for r in range(ROPE_DIM):
        place[r, :, NOPE_DIM + r] = 1.0
    wkv = jnp.concatenate([
        jnp.concatenate([wk.reshape(KV_RANK, -1), ukv[..., NOPE_DIM:].reshape(KV_RANK, -1)], axis=1),
        jnp.concatenate([jnp.asarray(place.reshape(LANES, -1)), jnp.zeros((LANES, 512), F32)], axis=1),
    ], axis=0).astype(BF16)
    return {
        "w_ab": w_ab, "w_ft": w_ft, "wq": wq, "wkv": wkv,
        "bf128": jnp.concatenate([b_f, jnp.zeros((LANES - N_HEADS,), F32)]).reshape(1, LANES),
        "bft": b_f.reshape(N_HEADS, 1),
        "gq": q_a_g.reshape(1, Q_RANK), "gkv": kv_a_g.reshape(1, KV_RANK),
        "wo1": w_out_ab[:512].astype(BF16), "wo2": w_out_ab[512:].astype(BF16),
    }


def _prep_ffn(w_gate, w_up, conv_w, conv_b, w_down):
    chunk_cols = lambda w: w.reshape(D_MODEL, N_FF_CHUNKS, FF_CHUNK).transpose(1, 0, 2).astype(BF16)
    return {
        "wg": chunk_cols(w_gate), "wu": chunk_cols(w_up),
        "cw": conv_w.reshape(CONV_W, N_FF_CHUNKS, FF_CHUNK).transpose(1, 0, 2),
        "cb": conv_b.reshape(N_FF_CHUNKS, 1, FF_CHUNK),
        "wd": w_down.reshape(N_FF_CHUNKS, FF_CHUNK, D_MODEL).astype(BF16),
    }


def _rope_tables(pos):
    half = ROPE_DIM // 2
    inv = ROPE_BASE ** (-jnp.arange(half, dtype=F32) / half)
    ang = pos.astype(F32)[:, None] * inv[None, :]
    cos2 = jnp.concatenate([jnp.cos(ang)] * 2, axis=1)
    sin2 = jnp.concatenate([jnp.sin(ang)] * 2, axis=1)
    n = pos.shape[0]
    scale = QK_SCALE_A
    zeros = lambda w: jnp.zeros((n, w), F32)
    tqc = scale * jnp.concatenate([jnp.ones((n, NOPE_DIM), F32), cos2, zeros(LANES - NOPE_DIM - ROPE_DIM)], axis=1)
    tqs = scale * jnp.concatenate([zeros(NOPE_DIM), sin2, zeros(LANES - NOPE_DIM - ROPE_DIM)], axis=1)
    tkc = jnp.concatenate([cos2, zeros(LANES - ROPE_DIM)], axis=1)
    tks = jnp.concatenate([sin2, zeros(LANES - ROPE_DIM)], axis=1)
    return tqc, tqs, tkc, tks


def _pad_rows(a, front, total):
    return jnp.pad(a, ((0, 0), (front, total - front - a.shape[1])) + ((0, 0),) * (a.ndim - 2))


def _trunk(x, mods, past, PA, PF, w_cd, wo_cd, rel, gains, tm, tq):
    nb, rows, _ = x.shape
    q_off = 0 if past is None else PAST_LEN
    pos = q_off + jnp.arange(rows)
    tabs = _rope_tables(pos)
    mix_pre_g, mix_post_g, ffn_pre_g, ffn_post_g = gains
    out = {}

    def mod6(l):
        m = mods[l].reshape(nb, 6, 1, D_MODEL)
        return [m[:, j] for j in range(6)]

    sh_m, sc_m, g_m, sh_f, sc_f, g_f = mod6(0)
    prompt = past is None
    (qcomb, ckv, krope, ckr, qb, kb16, vb16, kb32, vb32, logf, logf128, logft, *vbt) = _in_ab(
        x, sh_m, sc_m, mix_pre_g[0:1], PA, tabs, tm, prompt)
    out["a_ckv"], out["a_krope"], out["b_k"], out["b_v"], out["b_logf"] = ckv, krope, kb32, vb32, logf
    if prompt:
        tkk = rows
        ckr_all, kb_all, vb_all, lf_all, lft_all = ckr, kb16, vb16, logf128, logft
    else:
        tkk = -(-(PAST_LEN + rows) // ATT_BLOCK) * ATT_BLOCK
        cat = lambda p, n: _pad_rows(jnp.concatenate([p, n], axis=1), 0, tkk)
        ckr_past = jnp.concatenate([past["a_ckv"].astype(BF16),
                                    jnp.pad(past["a_krope"].astype(BF16), ((0, 0), (0, 0), (0, LANES - ROPE_DIM)))],
                                   axis=-1)
        ckr_all = cat(ckr_past, ckr)
        kb_all = cat(past["b_k"].reshape(nb, PAST_LEN, 512).astype(BF16), kb16)
        vb_all = cat(past["b_v"].reshape(nb, PAST_LEN, 512).astype(BF16), vb16)
        lf_all = cat(jnp.pad(past["b_logf"], ((0, 0), (0, 0), (0, LANES - N_HEADS))), logf128)
        lft_all = jnp.pad(jnp.concatenate([jnp.swapaxes(past["b_logf"], 1, 2), logft], axis=2),
                          ((0, 0), (0, 0), (0, tkk - PAST_LEN - rows)))
    if prompt:
        kcomb, vat = _kvup(ckr_all, PA["wkv"], ATT_BLOCK, True)
        _, cumt, ck_rep = _cum(lf_all, lft_all, True)
        o_a = _flash_t("mla", qcomb, kcomb, vat, tq=ATT_BLOCK)
        o_b = _flash_t("fox", qb, kb_all, vbt[0], cumt.reshape(nb, N_PAIRS, 2, rows), ck_rep, tq=ATT_BLOCK)
    else:
        kcomb, va = _kvup(ckr_all, PA["wkv"], tkk, False)
        cum, cumt = _cum(lf_all, lft_all, False)
        cq = cum[:, q_off:q_off + rows, :N_HEADS].reshape(nb, rows, N_PAIRS, 2).transpose(0, 2, 1, 3)
        nkb = tkk // ATT_BLOCK
        ckt = cumt.reshape(nb, N_PAIRS, 2, nkb, ATT_BLOCK).transpose(0, 1, 3, 2, 4)
        o_a = _flash("mla", qcomb, kcomb, va, tq=tq, q_off=q_off, n_valid=q_off + rows)
        o_b = _flash("fox", qb, kb_all, vb_all, cq, ckt, tq=tq, q_off=q_off, n_valid=q_off + rows)
    prev = jnp.zeros((nb, CONV_W - 1, D_FF), F32) if past is None else past["ffn_conv"][0]
    x, conv0 = _out_ffn(o_a, o_b, PA["wo1"], PA["wo2"], g_m, mix_post_g[0:1],
                        x, sh_f, sc_f, g_f, ffn_pre_g[0:1], ffn_post_g[0:1], prev, PF[0], tm)

    sh_m, sc_m, g_m, sh_f, sc_f, g_f = mod6(1)
    qc, kc16, vc16, qd, kd16, vd16, kc32, vc32, kd32, vd32, *vdt = _in_cd(x, sh_m, sc_m, mix_pre_g[1:2], w_cd, tm,
                                                                          prompt)
    out["d_k"], out["d_v"] = kd32, vd32
    if prompt:
        keep = min(BAND_CHUNKS * CHUNK, rows)
        out["c_k"], out["c_v"] = kc32[:, rows - keep:], vc32[:, rows - keep:]
        front = BAND_CHUNKS * CHUNK
        kc_all = _pad_rows(kc16, front, front + rows)
        vc_all = _pad_rows(vc16, front, front + rows)
        bias = _band_bias(rel, tq, front + tq, 0, -front)
        o_c = _band(qc, kc_all, vc_all, bias, tq=tq, n_sub=8, pad_rows=front)
        o_d = _sb_t(qd, kd16, vdt[0], tq=ATT_BLOCK)
    else:
        out["c_k"], out["c_v"] = kc32, vc32
        n_c = past["c_k"].shape[1]
        front = BAND_SAMPLE_WIN - n_c - rows
        kc_all = _pad_rows(jnp.concatenate([past["c_k"].reshape(nb, n_c, 512).astype(BF16), kc16], axis=1), front,
                           BAND_SAMPLE_WIN)
        vc_all = _pad_rows(jnp.concatenate([past["c_v"].reshape(nb, n_c, 512).astype(BF16), vc16], axis=1), front,
                           BAND_SAMPLE_WIN)
        bias = _band_bias(rel, rows, BAND_SAMPLE_WIN, q_off, PAST_LEN - n_c - front)
        o_c = _band(qc, kc_all, vc_all, bias, tq=rows, n_sub=1, pad_rows=0)
        kd_all = _pad_rows(jnp.concatenate([past["d_k"].reshape(nb, PAST_LEN, 512).astype(BF16), kd16], axis=1), 0, tkk)
        vd_all = _pad_rows(jnp.concatenate([past["d_v"].reshape(nb, PAST_LEN, 512).astype(BF16), vd16], axis=1), 0, tkk)
        o_d = _sb(qd, kd_all, vd_all, tq=tq, q_off=q_off)
    prev = jnp.zeros((nb, CONV_W - 1, D_FF), F32) if past is None else past["ffn_conv"][1]
    x, conv1 = _out_ffn(o_c, o_d, wo_cd[0], wo_cd[1], g_m, mix_post_g[1:2],
                        x, sh_f, sc_f, g_f, ffn_pre_g[1:2], ffn_post_g[1:2], prev, PF[1], tm)
    out["ffn_conv"] = jnp.stack([conv0, conv1])
    return x, out


def kernel(x_prompt, x_sample, c_prompt, c_sample, cache_a_ckv, cache_a_krope, cache_b_k, cache_b_v, cache_b_logf, cache_c_k, cache_c_v, cache_d_k, cache_d_v, state_ffn_conv, ada_w, ada_b, mix_pre_g, mix_post_g, ffn_pre_g, ffn_post_g, w_in_ab, b_f, q_a_g, kv_a_g, w_uq, w_ukv, w_out_ab, w_in_cd, rel_bias_c, w_out_cd, ffn_w_gate, ffn_w_up, ffn_conv_w, ffn_conv_b, ffn_w_down):
    nbp, nbs = x_prompt.shape[0], x_sample.shape[0]
    c_all = jnp.concatenate([c_prompt, c_sample, jnp.zeros((32 - nbp - nbs, D_MODEL), F32)], axis=0)
    mods = _ada(c_all, ada_w, ada_b)
    PA = _prep_ab(w_in_ab[0], b_f[0], q_a_g[0], kv_a_g[0], w_uq[0], w_ukv[0], w_out_ab[0])
    PF = [_prep_ffn(ffn_w_gate[l], ffn_w_up[l], ffn_conv_w[l], ffn_conv_b[l], ffn_w_down[l]) for l in range(DEPTH)]
    w_cd = w_in_cd[0].astype(BF16)
    wo_cd = (w_out_cd[0][:512].astype(BF16), w_out_cd[0][512:].astype(BF16))
    gains = (mix_pre_g, mix_post_g, ffn_pre_g, ffn_post_g)
    past = {"a_ckv": cache_a_ckv[0], "a_krope": cache_a_krope[0], "b_k": cache_b_k[0], "b_v": cache_b_v[0],
            "b_logf": cache_b_logf[0], "c_k": cache_c_k[0], "c_v": cache_c_v[0], "d_k": cache_d_k[0],
            "d_v": cache_d_v[0], "ffn_conv": state_ffn_conv}
    y_p, sp = _trunk(x_prompt, mods[:, :nbp], None, PA, PF, w_cd, wo_cd, rel_bias_c[0], gains, 512, ATT_SUB)
    y_s, ss = _trunk(x_sample, mods[:, nbp:nbp + nbs], past, PA, PF, w_cd, wo_cd, rel_bias_c[0], gains,
                     x_sample.shape[1], x_sample.shape[1])

    def heads(a):
        return a.reshape(a.shape[0], a.shape[1], N_HEADS, HEAD_DIM)[None]

    return (y_p, y_s,
            sp["a_ckv"][None], ss["a_ckv"][None], sp["a_krope"][None], ss["a_krope"][None],
            heads(sp["b_k"]), heads(ss["b_k"]), heads(sp["b_v"]), heads(ss["b_v"]),
            sp["b_logf"][None], ss["b_logf"][None],
            heads(sp["c_k"]), heads(ss["c_k"]), heads(sp["c_v"]), heads(ss["c_v"]),
            heads(sp["d_k"]), heads(ss["d_k"]), heads(sp["d_v"]), heads(ss["d_v"]),
            sp["ffn_conv"], ss["ffn_conv"])
```
